```python
import math
import jax, jax.numpy as jnp
from jax import lax
import numpy as np

D_MODEL = 2048
BATCH = 8
SEQ = 2048
DEPTH = 4
DEC_BATCH = 8
DEC_SEQ = 64
PAST_LEN = 2048

CHUNK = 64
N_META = 16
D_CONV = D_MODEL // 2
CONV_WIDTH = 31
N_HEADS = 16
HEAD_DIM = 64
D_ATT = N_HEADS * HEAD_DIM
IDX_HEADS = 16
IDX_DIM = 64
TOPK_MAX = 256
N_BUCKETS = 32
MAX_DISTANCE = 128
QBLOCK = 32
NORM_EPS = 1e-6
NEG = -1e30
SPLIT_SIZES = (D_CONV, D_CONV, D_CONV, D_ATT, D_ATT, D_ATT, D_ATT, IDX_HEADS * IDX_DIM, IDX_DIM, IDX_HEADS)
D_IN = 3 * D_CONV + 4 * D_ATT + IDX_HEADS * IDX_DIM + IDX_DIM + IDX_HEADS

kernel_name = "hymba_conformer_dsa_stream_step"


def rms_norm(x, g):
    x32 = x.astype(jnp.float32)
    y = x32 * lax.rsqrt(jnp.mean(x32 * x32, axis=-1, keepdims=True) + NORM_EPS)
    return (y * g.astype(jnp.float32)).astype(x.dtype)


def layer_norm(x, g, b):
    x32 = x.astype(jnp.float32)
    mu = jnp.mean(x32, axis=-1, keepdims=True)
    xc = x32 - mu
    y = xc * lax.rsqrt(jnp.mean(xc * xc, axis=-1, keepdims=True) + NORM_EPS)
    return (y * g.astype(jnp.float32) + b.astype(jnp.float32)).astype(x.dtype)


def chunk_id(pos):
    return jnp.where(pos < N_META, 0, 1 + (pos - N_META) // CHUNK)


def t5_bucket(rel):
    half = N_BUCKETS // 2
    max_exact = half // 2
    n = jnp.abs(rel)
    large = max_exact + (jnp.log(jnp.maximum(n, 1).astype(jnp.float32) / max_exact)
                         / math.log(MAX_DISTANCE / max_exact) * (half - max_exact)).astype(jnp.int32)
    large = jnp.minimum(large, half - 1)
    return jnp.where(rel > 0, half, 0) + jnp.where(n < max_exact, n, large)


def sparse_attention(q, k, v, q_idx, w_idx, k_idx, q_pos, k_pos, rel_bias, top_k):
    b, t = q.shape[0], q.shape[1]
    n_blk = -(-t // QBLOCK)
    pad = n_blk * QBLOCK - t

    def blocks(a):
        a = jnp.pad(a, [(0, 0), (0, pad)] + [(0, 0)] * (a.ndim - 2))
        return jnp.moveaxis(a.reshape((b, n_blk, QBLOCK) + a.shape[2:]), 1, 0)

    pos_blk = jnp.pad(q_pos, (0, pad), mode='edge').reshape(n_blk, QBLOCK)
    k_chunk = chunk_id(k_pos)
    k_idx32 = k_idx.astype(jnp.float32)
    gather_rows = jax.vmap(lambda rows, sel: rows[sel])

    def one_block(args):
        qb, qib, wb, pb = args
        q_chunk = chunk_id(pb)
        admissible = k_chunk[None, :] <= q_chunk[:, None]
        dots = jax.nn.relu(jnp.einsum('bqhd,bsd->bqhs', qib.astype(jnp.float32), k_idx32)) * (IDX_DIM ** -0.5)
        score = jnp.einsum('bqh,bqhs->bqs', wb.astype(jnp.float32) * (IDX_HEADS ** -0.5), dots)
        score = jnp.where(admissible[None], score, NEG)
        _, sel = lax.top_k(score, top_k)
        kpos_sel = k_pos[sel]
        ok = chunk_id(kpos_sel) <= q_chunk[None, :, None]
        bias = rel_bias[t5_bucket(kpos_sel - pb[None, :, None])]
        kg = gather_rows(k, sel)
        vg = gather_rows(v, sel)
        logits = (jnp.einsum('bqhd,bqkhd->bqhk', qb, kg).astype(jnp.float32) * (HEAD_DIM ** -0.5)
                  + jnp.moveaxis(bias, -1, 2).astype(jnp.float32))
        logits = jnp.where(ok[:, :, None, :], logits, NEG)
        p = jax.nn.softmax(logits, axis=-1).astype(v.dtype)
        return jnp.einsum('bqhk,bqkhd->bqhd', p, vg)

    out = lax.map(one_block, (blocks(q), blocks(q_idx), blocks(w_idx), pos_blk))
    return jnp.moveaxis(out, 0, 1).reshape(b, n_blk * QBLOCK, N_HEADS, HEAD_DIM)[:, :t]


def mixer_layer(h, conv_hist, k_past, v_past, kidx_past, q_pos, k_pos, top_k,
                norm_g, w_in, conv_w, conv_b, ln_g, ln_b, w_out, rel_bias):
    b, t, _ = h.shape
    xn = rms_norm(h, norm_g)
    proj = jnp.einsum('btd,de->bte', xn, w_in)
    offs = np.cumsum(SPLIT_SIZES)[:-1].tolist()
    glu_a, glu_b, z_conv, q, k, v, z_att, q_idx, k_idx, w_idx = jnp.split(proj, offs, axis=-1)

    u = glu_a * jax.nn.sigmoid(glu_b)
    u_ext = jnp.concatenate([conv_hist.astype(u.dtype), u], axis=1)
    c = lax.conv_general_dilated(u_ext, conv_w[:, None, :].astype(u.dtype), (1,), 'VALID',
                                 dimension_numbers=('NWC', 'WIO', 'NWC'),
                                 feature_group_count=D_CONV) + conv_b.astype(u.dtype)
    c = jax.nn.silu(layer_norm(c, ln_g, ln_b)) * jax.nn.silu(z_conv)
    new_conv = u_ext[:, -(CONV_WIDTH - 1):]

    q = q.reshape(b, t, N_HEADS, HEAD_DIM)
    k = k.reshape(b, t, N_HEADS, HEAD_DIM)
    v = v.reshape(b, t, N_HEADS, HEAD_DIM)
    q_idx = q_idx.reshape(b, t, IDX_HEADS, IDX_DIM)
    if k_past is None:
        k_all, v_all, kidx_all = k, v, k_idx
    else:
        k_all = jnp.concatenate([k_past.astype(k.dtype), k], axis=1)
        v_all = jnp.concatenate([v_past.astype(v.dtype), v], axis=1)
        kidx_all = jnp.concatenate([kidx_past.astype(k_idx.dtype), k_idx], axis=1)
    a = sparse_attention(q, k_all, v_all, q_idx, w_idx, kidx_all, q_pos, k_pos, rel_bias, top_k)
    a = a.reshape(b, t, D_ATT) * jax.nn.silu(z_att)

    out = jnp.einsum('bte,ed->btd', jnp.concatenate([c, a], axis=-1), w_out)
    return h + out, k, v, k_idx, new_conv


def setup_inputs(seed: int = 0) -> dict:
    key = jax.random.key(seed)
    ks = jax.random.split(key, 16)
    f32 = jnp.float32

    def nrm(k, shape, s):
        return jax.random.normal(k, shape, f32) * s

    return {
        "x_prompt": nrm(ks[0], (BATCH, SEQ, D_MODEL), 1.0),
        "x_sample": nrm(ks[1], (DEC_BATCH, DEC_SEQ, D_MODEL), 1.0),
        "cache_k": nrm(ks[2], (DEPTH, DEC_BATCH, N_META + PAST_LEN, N_HEADS, HEAD_DIM), 1.0),
        "cache_v": nrm(ks[3], (DEPTH, DEC_BATCH, N_META + PAST_LEN, N_HEADS, HEAD_DIM), 1.0),
        "cache_kidx": nrm(ks[4], (DEPTH, DEC_BATCH, N_META + PAST_LEN, IDX_DIM), 1.0),
        "state_conv": nrm(ks[5], (DEPTH, DEC_BATCH, CONV_WIDTH - 1, D_CONV), 0.5),
        "meta_tokens": nrm(ks[6], (N_META, D_MODEL), 1.0),
        "norm_g": 1.0 + nrm(ks[7], (DEPTH, D_MODEL), 0.02),
        "w_in": nrm(ks[8], (DEPTH, D_MODEL, D_IN), D_MODEL ** -0.5),
        "conv_w": nrm(ks[9], (DEPTH, CONV_WIDTH, D_CONV), CONV_WIDTH ** -0.5),
        "conv_b": nrm(ks[10], (DEPTH, D_CONV), 0.02),
        "conv_ln_g": 1.0 + nrm(ks[11], (DEPTH, D_CONV), 0.02),
        "conv_ln_b": nrm(ks[12], (DEPTH, D_CONV), 0.02),
        "w_out": nrm(ks[13], (DEPTH, D_CONV + D_ATT, D_MODEL), (D_CONV + D_ATT) ** -0.5),
        "rel_bias": nrm(ks[14], (N_BUCKETS, N_HEADS), 0.5),
        "final_g": 1.0 + nrm(ks[15], (D_MODEL,), 0.02),
    }


def reference(x_prompt, x_sample, cache_k, cache_v, cache_kidx, state_conv, meta_tokens,
              norm_g, w_in, conv_w, conv_b, conv_ln_g, conv_ln_b, w_out, rel_bias, final_g):
    bp, seq, _ = x_prompt.shape
    dec_seq = x_sample.shape[1]
    past_len = cache_k.shape[2] - N_META

    hp = jnp.concatenate([jnp.broadcast_to(meta_tokens[None].astype(x_prompt.dtype), (bp, N_META, D_MODEL)),
                          x_prompt], axis=1)
    pos_p = jnp.arange(N_META + seq, dtype=jnp.int32)
    topk_p = min(TOPK_MAX, seq // 4)
    zero_conv = jnp.zeros((bp, CONV_WIDTH - 1, D_CONV), x_prompt.dtype)

    hs = x_sample
    pos_s_k = jnp.arange(N_META + past_len + dec_seq, dtype=jnp.int32)
    pos_s_q = pos_s_k[N_META + past_len:]
    topk_s = min(TOPK_MAX, (past_len + dec_seq) // 4)

    kp_l, vp_l, kip_l, cp_l = [], [], [], []
    ks_l, vs_l, kis_l, cs_l = [], [], [], []
    for l in range(DEPTH):
        weights = (norm_g[l], w_in[l], conv_w[l], conv_b[l], conv_ln_g[l], conv_ln_b[l], w_out[l], rel_bias)
        hp, kp, vp, kip, cp = mixer_layer(hp, zero_conv, None, None, None, pos_p, pos_p, topk_p, *weights)
        hs, ks_, vs_, kis, cs = mixer_layer(hs, state_conv[l], cache_k[l], cache_v[l], cache_kidx[l],
                                            pos_s_q, pos_s_k, topk_s, *weights)
        kp_l.append(kp); vp_l.append(vp); kip_l.append(kip); cp_l.append(cp)
        ks_l.append(ks_); vs_l.append(vs_); kis_l.append(kis); cs_l.append(cs)

    y_prompt = rms_norm(hp[:, N_META:], final_g)
    y_sample = rms_norm(hs, final_g)
    return (y_prompt, y_sample,
            jnp.stack(kp_l), jnp.stack(vp_l), jnp.stack(kip_l), jnp.stack(cp_l),
            jnp.stack(ks_l), jnp.stack(vs_l), jnp.stack(kis_l), jnp.stack(cs_l))
```

```python
import functools
import math

import jax
import jax.numpy as jnp
import numpy as np
from jax import lax
from jax.experimental import pallas as pl
from jax.experimental.pallas import tpu as pltpu

N_META = 16
CHUNK = 64
D_CONV = 1024
CONV_WIDTH = 31
N_HEADS = 16
HEAD_DIM = 64
D_ATT = N_HEADS * HEAD_DIM
IDX_HEADS = 16
IDX_DIM = 64
TOPK_MAX = 256
N_BUCKETS = 32
MAX_DISTANCE = 128
NORM_EPS = 1e-6
NEG = -1e30

LANE = 128
SUBLANE = 8
KEY_BLOCK = 256
HALF = KEY_BLOCK // LANE
N_PAIRS = N_HEADS // 2
HIST_ROWS = 32
HIST_PAD = HIST_ROWS - (CONV_WIDTH - 1)
GROUP = 1024
N_GROUPS = 8
VMEM_LIMIT = 56 * 1024 * 1024
INT_MIN = -2147483648
M_INIT = -1e29

F32 = jnp.float32
BF16 = jnp.bfloat16
NT_DIMS = (((1,), (1,)), ((), ()))


def _silu(x):
    return x * jax.nn.sigmoid(x)


def _inproj_kernel(x_ref, g_ref, wm_ref, wt_ref,
                   u_ref, zc_ref, q_ref, k_ref, v_ref, za_ref, qi_ref, tail_ref,
                   xn_ref, a_ref):
    n = pl.program_id(1)

    @pl.when(n == 0)
    def _():
        x = x_ref[...]
        ms = jnp.mean(x * x, axis=-1, keepdims=True)
        xn_ref[...] = (x * lax.rsqrt(ms + NORM_EPS) * g_ref[...]).astype(BF16)

    def group():
        return jnp.dot(xn_ref[...], wm_ref[...], preferred_element_type=F32)

    @pl.when(n == 0)
    def _():
        a_ref[...] = group()

    @pl.when(n == 1)
    def _():
        u_ref[...] = a_ref[...] * jax.nn.sigmoid(group())

    @pl.when(n == 2)
    def _():
        zc_ref[...] = group().astype(BF16)

    @pl.when(n == 3)
    def _():
        q_ref[...] = (group() * (HEAD_DIM ** -0.5)).astype(BF16)

    @pl.when(n == 4)
    def _():
        k_ref[...] = group()

    @pl.when(n == 5)
    def _():
        v_ref[...] = group()

    @pl.when(n == 6)
    def _():
        za_ref[...] = group().astype(BF16)

    @pl.when(n == 7)
    def _():
        qi_ref[...] = group().astype(BF16)

    @pl.when(n == N_GROUPS)
    def _():
        tail_ref[...] = jnp.dot(xn_ref[...], wt_ref[...], preferred_element_type=F32)


def _inproj(x2d, g, w_main, w_tail, tm):
    m, d = x2d.shape
    assert m % tm == 0
    row = lambda width, dtype: jax.ShapeDtypeStruct((m, width), dtype)
    out_shape = (row(GROUP, F32), row(GROUP, BF16), row(GROUP, BF16), row(GROUP, F32),
                 row(GROUP, F32), row(GROUP, BF16), row(GROUP, BF16), row(LANE, F32))
    ospec = lambda width: pl.BlockSpec((tm, width), lambda i, n: (i, 0))
    return pl.pallas_call(
        _inproj_kernel,
        grid=(m // tm, N_GROUPS + 1),
        in_specs=[
            pl.BlockSpec((tm, d), lambda i, n: (i, 0)),
            pl.BlockSpec((1, d), lambda i, n: (0, 0)),
            pl.BlockSpec((d, GROUP), lambda i, n: (0, jnp.minimum(n, N_GROUPS - 1))),
            pl.BlockSpec((d, LANE), lambda i, n: (0, 0)),
        ],
        out_specs=[ospec(GROUP)] * 7 + [ospec(LANE)],
        out_shape=out_shape,
        scratch_shapes=[pltpu.VMEM((tm, d), BF16), pltpu.VMEM((tm, GROUP), F32)],
        compiler_params=pltpu.CompilerParams(
            dimension_semantics=("arbitrary", "arbitrary"), vmem_limit_bytes=VMEM_LIMIT),
        name="inproj",
    )(x2d, g, w_main, w_tail)


def _conv_kernel(u_ref, hist_ref, zc_ref, cw_ref, cb_ref, lg_ref, lb_ref, o_ref, ext_ref,
                 *, tt, rc):
    t = pl.program_id(1)
    n_hist = CONV_WIDTH - 1

    for b in range(SUBLANE):
        keep = n_hist - b

        @pl.when(t == 0)
        def _(b=b, keep=keep):
            ext_ref[b, 0:keep, :] = hist_ref[HIST_PAD + b:HIST_ROWS, :]

        @pl.when(t > 0)
        def _(b=b, keep=keep):
            ext_ref[b, 0:keep, :] = ext_ref[b, tt:tt + keep, :]

    for b in range(SUBLANE):
        ext_ref[b, n_hist - b:n_hist - b + tt, :] = u_ref[...]

    def body(r, carry):
        r0 = pl.multiple_of(r * rc, rc)
        acc = jnp.broadcast_to(cb_ref[...], (rc, D_CONV))
        for j in range(CONV_WIDTH):
            a, b = divmod(j, SUBLANE)
            acc = acc + cw_ref[j:j + 1, :] * ext_ref[b, pl.ds(r0 + a * SUBLANE, rc), :]
        mu = jnp.mean(acc, axis=-1, keepdims=True)
        xc = acc - mu
        var = jnp.mean(xc * xc, axis=-1, keepdims=True)
        y = xc * lax.rsqrt(var + NORM_EPS) * lg_ref[...] + lb_ref[...]
        z = zc_ref[pl.ds(r0, rc), :].astype(F32)
        o_ref[pl.ds(r0, rc), :] = (_silu(y) * _silu(z)).astype(BF16)
        return carry

    lax.fori_loop(0, tt // rc, body, 0)


def _conv(u, hist, zc, cw, cb, lg, lb):
    nb, t, c = u.shape
    tt = min(t, 256)
    rc = 16
    assert t % tt == 0 and tt % rc == 0
    hist_map = (lambda b, i: (b, 0, 0)) if hist.shape[0] == nb else (lambda b, i: (0, 0, 0))
    vec = pl.BlockSpec((1, c), lambda b, i: (0, 0))
    return pl.pallas_call(
        functools.partial(_conv_kernel, tt=tt, rc=rc),
        grid=(nb, t // tt),
        in_specs=[
            pl.BlockSpec((None, tt, c), lambda b, i: (b, i, 0)),
            pl.BlockSpec((None, HIST_ROWS, c), hist_map),
            pl.BlockSpec((None, tt, c), lambda b, i: (b, i, 0)),
            pl.BlockSpec((HIST_ROWS, c), lambda b, i: (0, 0)),
            vec, vec, vec,
        ],
        out_specs=pl.BlockSpec((None, tt, c), lambda b, i: (b, i, 0)),
        out_shape=jax.ShapeDtypeStruct((nb, t, c), BF16),
        scratch_shapes=[pltpu.VMEM((SUBLANE, HIST_ROWS + tt, c), F32)],
        compiler_params=pltpu.CompilerParams(
            dimension_semantics=("arbitrary", "arbitrary"), vmem_limit_bytes=VMEM_LIMIT),
        name="conv",
    )(u, hist, zc, cw, cb, lg, lb)


def _outproj_kernel(c_ref, a_ref, h_ref, wc_ref, wa_ref, fg_ref, o_ref, *, final):
    y = (jnp.dot(c_ref[...], wc_ref[...], preferred_element_type=F32)
         + jnp.dot(a_ref[...], wa_ref[...], preferred_element_type=F32))
    y = h_ref[...] + y
    if final:
        ms = jnp.mean(y * y, axis=-1, keepdims=True)
        y = y * lax.rsqrt(ms + NORM_EPS) * fg_ref[...]
    o_ref[...] = y


def _outproj(c, a, h, wc, wa, fg, tm, final):
    m, d = h.shape
    assert m % tm == 0
    half = lambda: pl.BlockSpec((tm, D_CONV), lambda i: (i, 0))
    full = lambda: pl.BlockSpec((tm, d), lambda i: (i, 0))
    wspec = lambda: pl.BlockSpec((D_CONV, d), lambda i: (0, 0))
    return pl.pallas_call(
        functools.partial(_outproj_kernel, final=final),
        grid=(m // tm,),
        in_specs=[half(), half(), full(), wspec(), wspec(), pl.BlockSpec((1, d), lambda i: (0, 0))],
        out_specs=full(),
        out_shape=jax.ShapeDtypeStruct((m, d), F32),
        compiler_params=pltpu.CompilerParams(
            dimension_semantics=("arbitrary",), vmem_limit_bytes=VMEM_LIMIT),
        name="outproj",
    )(c, a, h, wc, wa, fg)


def _bias_kernel(bucket_ref, rb_ref, o_ref, *, n_near, tq):
    h = pl.program_id(0)
    for j in range(n_near):
        def body(r, carry):
            r0 = pl.multiple_of(r * 8, 8)
            bk = bucket_ref[j, pl.ds(r0, 8), :]
            acc = jnp.zeros(bk.shape, F32)
            for b in range(N_BUCKETS):
                acc = jnp.where(bk == b, rb_ref[b, h], acc)
            o_ref[j, pl.ds(r0, 8), :] = acc
            return carry
        lax.fori_loop(0, tq // 8, body, 0)


def _bias_tiles(bucket, rel_bias):
    n_near, tq, kb = bucket.shape
    return pl.pallas_call(
        functools.partial(_bias_kernel, n_near=n_near, tq=tq),
        grid=(N_HEADS,),
        in_specs=[pl.BlockSpec((n_near, tq, kb), lambda h: (0, 0, 0)),
                  pl.BlockSpec(memory_space=pltpu.SMEM)],
        out_specs=pl.BlockSpec((None, n_near, tq, kb), lambda h: (h, 0, 0, 0)),
        out_shape=jax.ShapeDtypeStruct((N_HEADS, n_near, tq, kb), F32),
        compiler_params=pltpu.CompilerParams(dimension_semantics=("arbitrary",)),
        name="bias_tiles",
    )(bucket, rel_bias)


def _t5_bucket(rel):
    half = N_BUCKETS // 2
    max_exact = half // 2
    n = jnp.abs(rel)
    large = max_exact + (jnp.log(jnp.maximum(n, 1).astype(jnp.float32) / max_exact)
                         / math.log(MAX_DISTANCE / max_exact) * (half - max_exact)).astype(jnp.int32)
    large = jnp.minimum(large, half - 1)
    return jnp.where(rel > 0, half, 0) + jnp.where(n < max_exact, n, large)


class _AttnCfg:
    def __init__(self, tq, n_qt, s_pad, lane_off, n_pos, q_pos0, near_step, near_base, n_near, top_k):
        self.tq, self.n_qt, self.s_pad, self.lane_off = tq, n_qt, s_pad, lane_off
        self.n_pos, self.q_pos0 = n_pos, q_pos0
        self.near_step, self.near_base, self.n_near, self.top_k = near_step, near_base, n_near, top_k
        self.rel0 = near_base * KEY_BLOCK - lane_off - q_pos0
        assert s_pad % KEY_BLOCK == 0
        assert near_step * KEY_BLOCK == tq or n_qt == 1
        assert (near_base == 0 and near_step == 0) or 1 - self.rel0 >= MAX_DISTANCE
        assert (near_base + near_step * (n_qt - 1) + n_near) * KEY_BLOCK <= s_pad

    def bucket_table(self):
        j = jnp.arange(self.n_near, dtype=jnp.int32)[:, None, None]
        r = jnp.arange(self.tq, dtype=jnp.int32)[None, :, None]
        c = jnp.arange(KEY_BLOCK, dtype=jnp.int32)[None, None, :]
        rel = self.rel0 + KEY_BLOCK * j + c - r
        return _t5_bucket(lax.optimization_barrier(rel))


def _chunk_of(pos):
    return jnp.where(pos < N_META, 0, 1 + ((pos - N_META) >> 6))


def _attn_kernel(q_ref, qi_ref, w_ref, za_ref, k_ref, v_ref, kx_ref, bias_ref, rb_ref, tri_ref,
                 o_ref,
                 qm_ref, qim_ref, wb_ref, keys_ref, madd_ref, m_ref, l_ref, acc_ref, *, cfg):
    tq = cfg.tq
    i = pl.program_id(1)
    near0 = cfg.near_base + cfg.near_step * i if cfg.near_step else cfg.near_base
    nkb = near0 + cfg.n_near
    lane = lax.broadcasted_iota(jnp.int32, (tq, LANE), 1)
    row = lax.broadcasted_iota(jnp.int32, (tq, LANE), 0)
    low_half = lane < HEAD_DIM
    q_chunk = _chunk_of(cfg.q_pos0 + i * tq + row)

    def admissible(blk):
        kpos = blk * LANE + lane - cfg.lane_off
        return (kpos >= 0) & (kpos < cfg.n_pos) & (_chunk_of(kpos) <= q_chunk)

    zero = jnp.zeros((tq, LANE), BF16)
    for p in range(N_PAIRS):
        sl = slice(p * LANE, (p + 1) * LANE)
        qp = q_ref[:, sl]
        qm_ref[p, 0:tq, :] = jnp.where(low_half, qp, zero)
        qm_ref[p, tq:2 * tq, :] = jnp.where(low_half, zero, qp)
        qip = qi_ref[:, sl]
        qim_ref[p, 0:tq, :] = jnp.where(low_half, qip, zero)
        qim_ref[p, tq:2 * tq, :] = jnp.where(low_half, zero, qip)
    w_scale = (IDX_HEADS ** -0.5) * (IDX_DIM ** -0.5)
    for h in range(IDX_HEADS):
        col = IDX_DIM + h
        wb_ref[h] = jnp.broadcast_to(w_ref[:, col:col + 1] * w_scale, (tq, LANE))

    def index_block(kb, carry):
        r0 = pl.multiple_of(kb * KEY_BLOCK, KEY_BLOCK)
        kx = kx_ref[pl.ds(r0, KEY_BLOCK), :]
        acc = [jnp.zeros((tq, LANE), F32) for _ in range(HALF)]
        for p in range(N_PAIRS):
            d = lax.dot_general(qim_ref[p], kx, NT_DIMS, preferred_element_type=F32)
            d = jnp.maximum(d, 0.0)
            for hf in range(HALF):
                cs = slice(hf * LANE, (hf + 1) * LANE)
                acc[hf] = acc[hf] + wb_ref[2 * p] * d[0:tq, cs] + wb_ref[2 * p + 1] * d[tq:2 * tq, cs]
        for hf in range(HALF):
            blk = kb * HALF + hf
            sc = jnp.where(admissible(blk), acc[hf] + 0.0, NEG)
            bits = pltpu.bitcast(sc, jnp.int32)
            keys_ref[blk] = jnp.where(bits < 0, bits ^ jnp.int32(0x7FFFFFFF), bits)
        return carry

    lax.fori_loop(0, nkb, index_block, 0)
    nblk = nkb * HALF

    kf = jnp.float32(cfg.top_k)

    def count(pred):
        def body(b, c):
            return c + jnp.where(pred(keys_ref[b]), 1.0, 0.0)
        c = lax.fori_loop(0, nblk, body, jnp.zeros((tq, LANE), F32))
        return jnp.sum(c, axis=1, keepdims=True)

    def bisect(it, lo):
        cand = lo + lax.shift_left(jnp.int32(1), 31 - it)
        tot = count(lambda kk: kk >= cand)
        return jnp.where(tot >= kf, cand, lo)

    thr = lax.fori_loop(0, 32, bisect, jnp.full((tq, LANE), INT_MIN, jnp.int32))
    need = kf - count(lambda kk: kk > thr)

    def mask_block(b, carry):
        kk = keys_ref[b]
        eq = kk == thr
        eqf = jnp.where(eq, 1.0, 0.0)
        rank = carry + jnp.dot(eqf.astype(BF16), tri_ref[...], preferred_element_type=F32)
        sel = jnp.where(eq, jnp.where(rank <= need, 0.0, NEG), jnp.where(kk > thr, 0.0, NEG))
        madd_ref[b] = jnp.where(admissible(b), sel, NEG)
        return carry + jnp.sum(eqf, axis=1, keepdims=True)

    lax.fori_loop(0, nblk, mask_block, jnp.zeros((tq, 1), F32))

    def attend(p, kb, bias_fn):
        sl = slice(p * LANE, (p + 1) * LANE)
        r0 = pl.multiple_of(kb * KEY_BLOCK, KEY_BLOCK)
        kx = k_ref[pl.ds(r0, KEY_BLOCK), sl]
        vx = v_ref[pl.ds(r0, KEY_BLOCK), sl]
        s = lax.dot_general(qm_ref[p], kx, NT_DIMS, preferred_element_type=F32)
        ma = jnp.concatenate([madd_ref[kb * HALF + hf] for hf in range(HALF)], axis=1)
        b_even, b_odd = bias_fn()
        s = s + jnp.concatenate([ma + b_even, ma + b_odd], axis=0)
        m_prev = m_ref[...]
        m_new = jnp.maximum(m_prev, jnp.max(s, axis=1, keepdims=True))
        alpha = jnp.exp(m_prev - m_new)
        pe = jnp.exp(s - jnp.concatenate([m_new] * HALF, axis=1))
        l_ref[...] = alpha * l_ref[...] + jnp.sum(pe, axis=1, keepdims=True)
        m_ref[...] = m_new
        acc_ref[...] = alpha * acc_ref[...] + jnp.dot(pe.astype(BF16), vx, preferred_element_type=F32)

    for p in range(N_PAIRS):
        m_ref[...] = jnp.full((2 * tq, LANE), M_INIT, F32)
        l_ref[...] = jnp.zeros((2 * tq, LANE), F32)
        acc_ref[...] = jnp.zeros((2 * tq, LANE), F32)

        def far_block(kb, carry, p=p):
            attend(p, kb, lambda: (rb_ref[N_BUCKETS // 2 - 1, 2 * p], rb_ref[N_BUCKETS // 2 - 1, 2 * p + 1]))
            return carry

        lax.fori_loop(0, near0, far_block, 0)
        for j in range(cfg.n_near):
            attend(p, near0 + j, lambda j=j, p=p: (bias_ref[2 * p, j], bias_ref[2 * p + 1, j]))

        o = acc_ref[...] / l_ref[...]
        o = jnp.where(low_half, o[0:tq], o[tq:2 * tq])
        z = za_ref[:, p * LANE:(p + 1) * LANE].astype(F32)
        o_ref[:, p * LANE:(p + 1) * LANE] = (o * _silu(z)).astype(BF16)


def _attention(cfg, q, qi, w, za, kcat, vcat, kxcat, bias, rel_bias, tri):
    nb, t, _ = q.shape
    tq, s_pad = cfg.tq, cfg.s_pad
    assert t == tq * cfg.n_qt and kcat.shape[1] == s_pad
    qspec = lambda width: pl.BlockSpec((None, tq, width), lambda b, i: (b, i, 0))
    kspec = lambda width: pl.BlockSpec((None, s_pad, width), lambda b, i: (b, 0, 0),
                                       pipeline_mode=pl.Buffered(1))
    nblk_max = s_pad // LANE
    return pl.pallas_call(
        functools.partial(_attn_kernel, cfg=cfg),
        grid=(nb, cfg.n_qt),
        in_specs=[
            qspec(D_ATT), qspec(D_ATT), qspec(LANE), qspec(D_ATT),
            kspec(D_ATT), kspec(D_ATT), kspec(LANE),
            pl.BlockSpec((N_HEADS, cfg.n_near, tq, KEY_BLOCK), lambda b, i: (0, 0, 0, 0),
                         pipeline_mode=pl.Buffered(1)),
            pl.BlockSpec(memory_space=pltpu.SMEM),
            pl.BlockSpec((LANE, LANE), lambda b, i: (0, 0)),
        ],
        out_specs=qspec(D_ATT),
        out_shape=jax.ShapeDtypeStruct((nb, t, D_ATT), BF16),
        scratch_shapes=[
            pltpu.VMEM((N_PAIRS, 2 * tq, LANE), BF16),
            pltpu.VMEM((N_PAIRS, 2 * tq, LANE), BF16),
            pltpu.VMEM((IDX_HEADS, tq, LANE), F32),
            pltpu.VMEM((nblk_max, tq, LANE), jnp.int32),
            pltpu.VMEM((nblk_max, tq, LANE), F32),
            pltpu.VMEM((2 * tq, LANE), F32),
            pltpu.VMEM((2 * tq, LANE), F32),
            pltpu.VMEM((2 * tq, LANE), F32),
        ],
        compiler_params=pltpu.CompilerParams(
            dimension_semantics=("arbitrary", "arbitrary"), vmem_limit_bytes=VMEM_LIMIT),
        name="attention",
    )(q, qi, w, za, kcat, vcat, kxcat, bias, rel_bias, tri)


def _pad_rows(parts, axis=1):
    return jnp.concatenate(parts, axis=axis)


def _forward(x_prompt, x_sample, cache_k, cache_v, cache_kidx, state_conv, meta_tokens,
             norm_g, w_in, conv_w, conv_b, conv_ln_g, conv_ln_b, w_out, rel_bias, final_g,
             *, tq_frames=256, tm_frames=512):
    depth = w_in.shape[0]
    bp, seq, d = x_prompt.shape
    bs, dec, _ = x_sample.shape
    past = cache_k.shape[2] - N_META
    assert seq % tq_frames == 0 and seq % CHUNK == 0 and (bp * seq) % tm_frames == 0

    n_qt = seq // tq_frames
    cfg_f = _AttnCfg(tq=tq_frames, n_qt=n_qt, s_pad=LANE + seq + LANE, lane_off=LANE - N_META,
                     n_pos=N_META + seq, q_pos0=N_META, near_step=tq_frames // KEY_BLOCK, near_base=0,
                     n_near=tq_frames // KEY_BLOCK + 1, top_k=min(TOPK_MAX, seq // 4))
    n_pos_s = N_META + past + dec
    s_pad_s = -(-n_pos_s // KEY_BLOCK) * KEY_BLOCK
    cfg_s = _AttnCfg(tq=dec, n_qt=1, s_pad=s_pad_s, lane_off=s_pad_s - n_pos_s, n_pos=n_pos_s,
                     q_pos0=N_META + past, near_step=0, near_base=s_pad_s // KEY_BLOCK - 1,
                     n_near=1, top_k=min(TOPK_MAX, (past + dec) // 4))
    cfg_m = _AttnCfg(tq=N_META, n_qt=1, s_pad=KEY_BLOCK, lane_off=KEY_BLOCK - N_META, n_pos=N_META,
                     q_pos0=0, near_step=0, near_base=0, n_near=1, top_k=min(TOPK_MAX, seq // 4))
    assert dec % 16 == 0 and dec + MAX_DISTANCE <= KEY_BLOCK

    bias_f = _bias_tiles(cfg_f.bucket_table(), rel_bias)
    bias_s = _bias_tiles(cfg_s.bucket_table(), rel_bias)
    bias_m = _bias_tiles(cfg_m.bucket_table(), rel_bias)
    tri = (jnp.arange(LANE)[:, None] <= jnp.arange(LANE)[None, :]).astype(BF16)

    n_aux = bs * dec + N_META
    hf = x_prompt.reshape(bp * seq, d)
    haux = jnp.concatenate([x_sample.reshape(bs * dec, d), meta_tokens.astype(x_sample.dtype)], axis=0)

    outs = {name: [] for name in ("kp", "vp", "kip", "cp", "ks", "vs", "kis", "cs")}
    for l in range(depth):
        w_main = w_in[l, :, :N_GROUPS * GROUP].astype(BF16)
        w_tail = jnp.pad(w_in[l, :, N_GROUPS * GROUP:], ((0, 0), (0, LANE - IDX_DIM - IDX_HEADS))).astype(BF16)
        g = norm_g[l][None, :]
        cw = jnp.pad(conv_w[l], ((0, HIST_ROWS - CONV_WIDTH), (0, 0)))
        cb, lg, lb = conv_b[l][None, :], conv_ln_g[l][None, :], conv_ln_b[l][None, :]
        wc = w_out[l, :D_CONV].astype(BF16)
        wa = w_out[l, D_CONV:].astype(BF16)
        final = l == depth - 1
        fg = final_g[None, :]

        uA, zcA, qA, kA, vA, zaA, qiA, tailA = _inproj(haux, g, w_main, w_tail, n_aux)
        uF, zcF, qF, kF, vF, zaF, qiF, tailF = _inproj(hf, g, w_main, w_tail, tm_frames)
        ns = bs * dec
        split = lambda a: (a[:ns].reshape(bs, dec, -1), a[ns:][None])
        uS, uM = split(uA); zcS, zcM = split(zcA); qS, qM = split(qA); kS, kM = split(kA)
        vS, vM = split(vA); zaS, zaM = split(zaA); qiS, qiM = split(qiA); tailS, tailM = split(tailA)
        b3 = lambda a: a.reshape(bp, seq, -1)
        uF, zcF, qF, kF, vF, zaF, qiF, tailF = map(b3, (uF, zcF, qF, kF, vF, zaF, qiF, tailF))

        zero_hist = jnp.zeros((1, HIST_ROWS, D_CONV), F32)
        cM = _conv(uM, zero_hist, zcM, cw, cb, lg, lb)
        hist_f = jnp.concatenate([jnp.zeros((1, HIST_ROWS - N_META, D_CONV), F32), uM], axis=1)
        cF = _conv(uF, hist_f, zcF, cw, cb, lg, lb)
        hist_s = jnp.pad(state_conv[l].astype(F32), ((0, 0), (HIST_PAD, 0), (0, 0)))
        cS = _conv(uS, hist_s, zcS, cw, cb, lg, lb)

        kidx = lambda tail: tail[..., :IDX_DIM]
        twice = lambda a: jnp.concatenate([a, a], axis=-1).astype(BF16)

        def lay(parts, width, nb, cfg):
            lead = jnp.zeros((nb, cfg.lane_off, width), BF16)
            body = [jnp.broadcast_to(a, (nb,) + a.shape[1:]).astype(BF16) for a in parts]
            n_tail = cfg.s_pad - cfg.lane_off - sum(a.shape[1] for a in parts)
            tailz = [jnp.zeros((nb, n_tail, width), BF16)] if n_tail else []
            return jnp.concatenate([lead] + body + tailz, axis=1)

        aM = _attention(cfg_m, qM, qiM, tailM, zaM,
                        lay([kM], D_ATT, 1, cfg_m), lay([vM], D_ATT, 1, cfg_m),
                        lay([twice(kidx(tailM))], LANE, 1, cfg_m), bias_m, rel_bias, tri)
        aF = _attention(cfg_f, qF, qiF, tailF, zaF,
                        lay([kM, kF], D_ATT, bp, cfg_f), lay([vM, vF], D_ATT, bp, cfg_f),
                        lay([twice(kidx(tailM)), twice(kidx(tailF))], LANE, bp, cfg_f),
                        bias_f, rel_bias, tri)
        ck = cache_k[l].reshape(bs, N_META + past, D_ATT)
        cv = cache_v[l].reshape(bs, N_META + past, D_ATT)
        aS = _attention(cfg_s, qS, qiS, tailS, zaS,
                        lay([ck, kS], D_ATT, bs, cfg_s), lay([cv, vS], D_ATT, bs, cfg_s),
                        lay([twice(cache_kidx[l]), twice(kidx(tailS))], LANE, bs, cfg_s),
                        bias_s, rel_bias, tri)

        cAux = jnp.concatenate([cS.reshape(ns, D_CONV), cM[0]], axis=0)
        aAux = jnp.concatenate([aS.reshape(ns, D_ATT), aM[0]], axis=0)
        haux = _outproj(cAux, aAux, haux, wc, wa, fg, n_aux, final)
        hf = _outproj(cF.reshape(bp * seq, D_CONV), aF.reshape(bp * seq, D_ATT), hf, wc, wa, fg,
                      tm_frames, final)

        with_meta = lambda m, f: jnp.concatenate([jnp.broadcast_to(m, (bp,) + m.shape[1:]), f], axis=1)
        outs["kp"].append(with_meta(kM, kF).reshape(bp, N_META + seq, N_HEADS, HEAD_DIM))
        outs["vp"].append(with_meta(vM, vF).reshape(bp, N_META + seq, N_HEADS, HEAD_DIM))
        outs["kip"].append(with_meta(kidx(tailM), kidx(tailF)))
        u_ext_p = jnp.concatenate([jnp.zeros((bp, CONV_WIDTH - 1, D_CONV), F32),
                                   jnp.broadcast_to(uM, (bp, N_META, D_CONV)), uF], axis=1)
        outs["cp"].append(u_ext_p[:, -(CONV_WIDTH - 1):])
        outs["ks"].append(kS.reshape(bs, dec, N_HEADS, HEAD_DIM))
        outs["vs"].append(vS.reshape(bs, dec, N_HEADS, HEAD_DIM))
        outs["kis"].append(kidx(tailS))
        u_ext_s = jnp.concatenate([state_conv[l].astype(F32), uS], axis=1)
        outs["cs"].append(u_ext_s[:, -(CONV_WIDTH - 1):])

    y_prompt = hf.reshape(bp, seq, d)
    y_sample = haux[:bs * dec].reshape(bs, dec, d)
    st = lambda name: jnp.stack(outs[name])
    return (y_prompt, y_sample, st("kp"), st("vp"), st("kip"), st("cp"),
            st("ks"), st("vs"), st("kis"), st("cs"))


def kernel(x_prompt, x_sample, cache_k, cache_v, cache_kidx, state_conv, meta_tokens, norm_g, w_in,
           conv_w, conv_b, conv_ln_g, conv_ln_b, w_out, rel_bias, final_g):
    return _forward(x_prompt, x_sample, cache_k, cache_v, cache_kidx, state_conv, meta_tokens,
                    norm_g, w_in, conv_w, conv_b, conv_ln_g, conv_ln_b, w_out, rel_bias, final_g)
```

```python
import functools
import math

import jax
import jax.numpy as jnp
from jax import lax
from jax.experimental import pallas as pl
from jax.experimental.pallas import tpu as pltpu

N_META = 16
CHUNK = 64
CHUNK_SHIFT = 6
D_CONV = 1024
CONV_WIDTH = 31
N_HEADS = 16
HEAD_DIM = 64
D_ATT = N_HEADS * HEAD_DIM
IDX_HEADS = 16
IDX_DIM = 64
TOPK_MAX = 256
N_BUCKETS = 32
MAX_DISTANCE = 128
NORM_EPS = 1e-6
NEG = -1e30

LANE = 128
SUBLANE = 8
KEY_BLOCK = 256
HALF = KEY_BLOCK // LANE
N_PAIRS = N_HEADS // 2
PAIR_GROUP = 2
HIST_ROWS = 32
HIST_PAD = HIST_ROWS - (CONV_WIDTH - 1)
GROUP = 1024
N_GROUPS = 8
VMEM_LIMIT = 56 * 1024 * 1024
INT_MIN = -2147483648
M_INIT = -1e29
FAR_BUCKET = N_BUCKETS // 2 - 1

F32 = jnp.float32
BF16 = jnp.bfloat16
NT_DIMS = (((1,), (1,)), ((), ()))


def _silu(x):
    return x * jax.nn.sigmoid(x)


def _inproj_kernel(x_ref, g_ref, wm_ref, wt_ref,
                   u_ref, zc_ref, q_ref, k_ref, v_ref, za_ref, qi_ref, tail_ref,
                   xn_ref, a_ref):
    n = pl.program_id(1)

    @pl.when(n == 0)
    def _():
        x = x_ref[...]
        ms = jnp.mean(x * x, axis=-1, keepdims=True)
        xn_ref[...] = (x * lax.rsqrt(ms + NORM_EPS) * g_ref[...]).astype(BF16)

    def group():
        return jnp.dot(xn_ref[...], wm_ref[...], preferred_element_type=F32)

    @pl.when(n == 0)
    def _():
        a_ref[...] = group()

    @pl.when(n == 1)
    def _():
        u_ref[...] = a_ref[...] * jax.nn.sigmoid(group())

    @pl.when(n == 2)
    def _():
        zc_ref[...] = group().astype(BF16)

    @pl.when(n == 3)
    def _():
        q_ref[...] = (group() * (HEAD_DIM ** -0.5)).astype(BF16)

    @pl.when(n == 4)
    def _():
        k_ref[...] = group()

    @pl.when(n == 5)
    def _():
        v_ref[...] = group()

    @pl.when(n == 6)
    def _():
        za_ref[...] = group().astype(BF16)

    @pl.when(n == 7)
    def _():
        qi_ref[...] = group().astype(BF16)

    @pl.when(n == N_GROUPS)
    def _():
        tail_ref[...] = jnp.dot(xn_ref[...], wt_ref[...], preferred_element_type=F32)


def _inproj(x2d, g, w_main, w_tail, tm):
    m, d = x2d.shape
    assert m % tm == 0
    row = lambda width, dtype: jax.ShapeDtypeStruct((m, width), dtype)
    out_shape = (row(GROUP, F32), row(GROUP, BF16), row(GROUP, BF16), row(GROUP, F32),
                 row(GROUP, F32), row(GROUP, BF16), row(GROUP, BF16), row(LANE, F32))
    ospec = lambda width: pl.BlockSpec((tm, width), lambda i, n: (i, 0))
    return pl.pallas_call(
        _inproj_kernel,
        grid=(m // tm, N_GROUPS + 1),
        in_specs=[
            pl.BlockSpec((tm, d), lambda i, n: (i, 0)),
            pl.BlockSpec((1, d), lambda i, n: (0, 0)),
            pl.BlockSpec((d, GROUP), lambda i, n: (0, jnp.minimum(n, N_GROUPS - 1))),
            pl.BlockSpec((d, LANE), lambda i, n: (0, 0)),
        ],
        out_specs=[ospec(GROUP)] * 7 + [ospec(LANE)],
        out_shape=out_shape,
        scratch_shapes=[pltpu.VMEM((tm, d), BF16), pltpu.VMEM((tm, GROUP), F32)],
        compiler_params=pltpu.CompilerParams(
            dimension_semantics=("arbitrary", "arbitrary"), vmem_limit_bytes=VMEM_LIMIT),
        name="inproj",
    )(x2d, g, w_main, w_tail)


def _conv_kernel(u_ref, hist_ref, zc_ref, cw_ref, cb_ref, lg_ref, lb_ref, o_ref, ext_ref,
                 *, tt, rc):
    t = pl.program_id(1)
    n_hist = CONV_WIDTH - 1

    for b in range(SUBLANE):
        keep = n_hist - b

        @pl.when(t == 0)
        def _(b=b, keep=keep):
            ext_ref[b, 0:keep, :] = hist_ref[HIST_PAD + b:HIST_ROWS, :]

        @pl.when(t > 0)
        def _(b=b, keep=keep):
            ext_ref[b, 0:keep, :] = ext_ref[b, tt:tt + keep, :]

    for b in range(SUBLANE):
        ext_ref[b, n_hist - b:n_hist - b + tt, :] = u_ref[...]

    def body(r, carry):
        r0 = pl.multiple_of(r * rc, rc)
        acc = jnp.broadcast_to(cb_ref[...], (rc, D_CONV))
        for j in range(CONV_WIDTH):
            a, b = divmod(j, SUBLANE)
            acc = acc + cw_ref[j:j + 1, :] * ext_ref[b, pl.ds(r0 + a * SUBLANE, rc), :]
        mu = jnp.mean(acc, axis=-1, keepdims=True)
        xc = acc - mu
        var = jnp.mean(xc * xc, axis=-1, keepdims=True)
        y = xc * lax.rsqrt(var + NORM_EPS) * lg_ref[...] + lb_ref[...]
        z = zc_ref[pl.ds(r0, rc), :].astype(F32)
        o_ref[pl.ds(r0, rc), :] = (_silu(y) * _silu(z)).astype(BF16)
        return carry

    lax.fori_loop(0, tt // rc, body, 0)


def _conv(u, hist, zc, cw, cb, lg, lb):
    nb, t, c = u.shape
    tt = min(t, 256)
    rc = 16
    assert t % tt == 0 and tt % rc == 0
    hist_map = (lambda b, i: (b, 0, 0)) if hist.shape[0] == nb else (lambda b, i: (0, 0, 0))
    vec = pl.BlockSpec((1, c), lambda b, i: (0, 0))
    return pl.pallas_call(
        functools.partial(_conv_kernel, tt=tt, rc=rc),
        grid=(nb, t // tt),
        in_specs=[
            pl.BlockSpec((None, tt, c), lambda b, i: (b, i, 0)),
            pl.BlockSpec((None, HIST_ROWS, c), hist_map),
            pl.BlockSpec((None, tt, c), lambda b, i: (b, i, 0)),
            pl.BlockSpec((HIST_ROWS, c), lambda b, i: (0, 0)),
            vec, vec, vec,
        ],
        out_specs=pl.BlockSpec((None, tt, c), lambda b, i: (b, i, 0)),
        out_shape=jax.ShapeDtypeStruct((nb, t, c), BF16),
        scratch_shapes=[pltpu.VMEM((SUBLANE, HIST_ROWS + tt, c), F32)],
        compiler_params=pltpu.CompilerParams(
            dimension_semantics=("arbitrary", "arbitrary"), vmem_limit_bytes=VMEM_LIMIT),
        name="conv",
    )(u, hist, zc, cw, cb, lg, lb)


def _outproj_kernel(c_ref, a_ref, h_ref, wc_ref, wa_ref, fg_ref, o_ref, *, final):
    y = (jnp.dot(c_ref[...], wc_ref[...], preferred_element_type=F32)
         + jnp.dot(a_ref[...], wa_ref[...], preferred_element_type=F32))
    y = h_ref[...] + y
    if final:
        ms = jnp.mean(y * y, axis=-1, keepdims=True)
        y = y * lax.rsqrt(ms + NORM_EPS) * fg_ref[...]
    o_ref[...] = y


def _outproj(c, a, h, wc, wa, fg, tm, final):
    m, d = h.shape
    assert m % tm == 0
    half = lambda: pl.BlockSpec((tm, D_CONV), lambda i: (i, 0))
    full = lambda: pl.BlockSpec((tm, d), lambda i: (i, 0))
    wspec = lambda: pl.BlockSpec((D_CONV, d), lambda i: (0, 0))
    return pl.pallas_call(
        functools.partial(_outproj_kernel, final=final),
        grid=(m // tm,),
        in_specs=[half(), half(), full(), wspec(), wspec(), pl.BlockSpec((1, d), lambda i: (0, 0))],
        out_specs=full(),
        out_shape=jax.ShapeDtypeStruct((m, d), F32),
        compiler_params=pltpu.CompilerParams(
            dimension_semantics=("arbitrary",), vmem_limit_bytes=VMEM_LIMIT),
        name="outproj",
    )(c, a, h, wc, wa, fg)


def _bias_kernel(bucket_ref, rb_ref, o_ref, *, n_near, tq):
    h = pl.program_id(0)
    far = rb_ref[FAR_BUCKET, h]
    for j in range(n_near):
        def body(r, carry):
            r0 = pl.multiple_of(r * SUBLANE, SUBLANE)
            bk = bucket_ref[j, pl.ds(r0, SUBLANE), :]
            acc = jnp.zeros(bk.shape, F32)
            for b in range(N_BUCKETS):
                acc = jnp.where(bk == b, rb_ref[b, h], acc)
            o_ref[j, pl.ds(r0, SUBLANE), :] = acc - far
            return carry
        lax.fori_loop(0, tq // SUBLANE, body, 0)


def _bias_tiles(bucket, rel_bias):
    n_near, tq, kb = bucket.shape
    return pl.pallas_call(
        functools.partial(_bias_kernel, n_near=n_near, tq=tq),
        grid=(N_HEADS,),
        in_specs=[pl.BlockSpec((n_near, tq, kb), lambda h: (0, 0, 0)),
                  pl.BlockSpec(memory_space=pltpu.SMEM)],
        out_specs=pl.BlockSpec((None, n_near, tq, kb), lambda h: (h, 0, 0, 0)),
        out_shape=jax.ShapeDtypeStruct((N_HEADS, n_near, tq, kb), F32),
        compiler_params=pltpu.CompilerParams(dimension_semantics=("arbitrary",)),
        name="bias_tiles",
    )(bucket, rel_bias)


def _t5_bucket(rel):
    half = N_BUCKETS // 2
    max_exact = half // 2
    n = jnp.abs(rel)
    large = max_exact + (jnp.log(jnp.maximum(n, 1).astype(jnp.float32) / max_exact)
                         / math.log(MAX_DISTANCE / max_exact) * (half - max_exact)).astype(jnp.int32)
    large = jnp.minimum(large, half - 1)
    return jnp.where(rel > 0, half, 0) + jnp.where(n < max_exact, n, large)


class _AttnCfg:
    def __init__(self, tq, n_rows, n_qt, s_pad, lane_off, n_pos, q_pos0, near_step, near_base, n_near, top_k):
        self.tq, self.n_rows, self.n_qt, self.s_pad, self.lane_off = tq, n_rows, n_qt, s_pad, lane_off
        self.n_pos, self.q_pos0 = n_pos, q_pos0
        self.near_step, self.near_base, self.n_near, self.top_k = near_step, near_base, n_near, top_k
        self.rel0 = near_base * KEY_BLOCK - lane_off - q_pos0
        assert s_pad % KEY_BLOCK == 0 and tq % LANE == 0 and n_rows <= tq
        assert near_step * KEY_BLOCK == tq or n_qt == 1
        assert (near_base == 0 and near_step == 0) or 1 - self.rel0 >= MAX_DISTANCE
        assert (near_base + near_step * (n_qt - 1) + n_near) * KEY_BLOCK <= s_pad

    def bucket_table(self):
        j = jnp.arange(self.n_near, dtype=jnp.int32)[:, None, None]
        r = jnp.arange(self.tq, dtype=jnp.int32)[None, :, None]
        c = jnp.arange(KEY_BLOCK, dtype=jnp.int32)[None, None, :]
        rel = self.rel0 + KEY_BLOCK * j + c - r
        return _t5_bucket(lax.optimization_barrier(rel))


def _chunk_of(pos):
    return jnp.where(pos < N_META, 0, 1 + ((pos - N_META) >> CHUNK_SHIFT))


def _attn_kernel(q_ref, qi_ref, w_ref, za_ref, k_ref, v_ref, kx_ref, bias_ref, tri_ref,
                 o_ref,
                 qm_ref, qim_ref, wb_ref, keys_ref, madd_ref, s_ref, mx_ref, l_ref, acc_ref, *, cfg):
    tq = cfg.tq
    i = pl.program_id(1)
    near0 = cfg.near_base + cfg.near_step * i if cfg.near_step else cfg.near_base
    nkb = near0 + cfg.n_near
    lane = lax.broadcasted_iota(jnp.int32, (tq, LANE), 1)
    row = lax.broadcasted_iota(jnp.int32, (tq, LANE), 0)
    low_half = lane < HEAD_DIM
    q_pos_base = cfg.q_pos0 + i * tq
    q_chunk = _chunk_of(q_pos_base + row)

    def admissible(blk):
        kpos = blk * LANE + lane - cfg.lane_off
        return (kpos >= 0) & (kpos < cfg.n_pos) & (_chunk_of(kpos) <= q_chunk)

    zero = jnp.zeros((tq, LANE), BF16)
    for p in range(N_PAIRS):
        sl = slice(p * LANE, (p + 1) * LANE)
        qp = q_ref[:, sl]
        qm_ref[p, 0:tq, :] = jnp.where(low_half, qp, zero)
        qm_ref[p, tq:2 * tq, :] = jnp.where(low_half, zero, qp)
        qip = qi_ref[:, sl]
        qim_ref[p, 0:tq, :] = jnp.where(low_half, qip, zero)
        qim_ref[p, tq:2 * tq, :] = jnp.where(low_half, zero, qip)
    w_scale = (IDX_HEADS ** -0.5) * (IDX_DIM ** -0.5)
    for h in range(IDX_HEADS):
        col = IDX_DIM + h
        wb_ref[h] = jnp.broadcast_to(w_ref[:, col:col + 1] * w_scale, (tq, LANE))

    def index_block(kb, carry):
        r0 = pl.multiple_of(kb * KEY_BLOCK, KEY_BLOCK)
        kx = kx_ref[pl.ds(r0, KEY_BLOCK), :]
        acc = [jnp.zeros((tq, LANE), F32) for _ in range(HALF)]
        for p in range(N_PAIRS):
            d = lax.dot_general(qim_ref[p], kx, NT_DIMS, preferred_element_type=F32)
            d = jnp.maximum(d, 0.0)
            for hf in range(HALF):
                cs = slice(hf * LANE, (hf + 1) * LANE)
                acc[hf] = acc[hf] + wb_ref[2 * p] * d[0:tq, cs] + wb_ref[2 * p + 1] * d[tq:2 * tq, cs]
        for hf in range(HALF):
            blk = kb * HALF + hf
            sc = jnp.where(admissible(blk), acc[hf] + 0.0, NEG)
            bits = pltpu.bitcast(sc.T, jnp.int32)
            keys_ref[pl.ds(pl.multiple_of(blk * LANE, LANE), LANE), :] = (
                jnp.where(bits < 0, bits ^ jnp.int32(0x7FFFFFFF), bits))
        return carry

    lax.fori_loop(0, nkb, index_block, 0)

    kf = jnp.float32(cfg.top_k)
    n_part = 4

    def count(pred):
        def body(c_i, parts):
            r0 = pl.multiple_of(c_i * LANE, LANE)
            parts = list(parts)
            for j in range(LANE // SUBLANE):
                kk = keys_ref[pl.ds(r0 + j * SUBLANE, SUBLANE), :]
                parts[j % n_part] = parts[j % n_part] + jnp.where(pred(kk), 1.0, 0.0)
            return tuple(parts)
        parts = lax.fori_loop(0, nkb * HALF, body, tuple(jnp.zeros((SUBLANE, tq), F32) for _ in range(n_part)))
        tot = (parts[0] + parts[1]) + (parts[2] + parts[3])
        return jnp.broadcast_to(jnp.sum(tot, axis=0, keepdims=True), (SUBLANE, tq))

    def bisect(it, lo):
        cand = lo + lax.shift_left(jnp.int32(1), 31 - it)
        return jnp.where(count(lambda kk: kk >= cand) >= kf, cand, lo)

    thr8 = lax.fori_loop(0, 32, bisect, jnp.full((SUBLANE, tq), INT_MIN, jnp.int32))
    need8 = kf - count(lambda kk: kk > thr8)
    thr = thr8[0:1, :]
    need = need8[0:1, :]

    key_row = lax.broadcasted_iota(jnp.int32, (KEY_BLOCK, tq), 0)
    q_chunk_t = _chunk_of(q_pos_base + lax.broadcasted_iota(jnp.int32, (KEY_BLOCK, tq), 1))

    def mask_block(kb, carry):
        r0 = pl.multiple_of(kb * KEY_BLOCK, KEY_BLOCK)
        kk = keys_ref[pl.ds(r0, KEY_BLOCK), :]
        eq = kk == thr
        eqf = jnp.where(eq, 1.0, 0.0)
        rank = carry + jnp.dot(tri_ref[...], eqf.astype(BF16), preferred_element_type=F32)
        sel = jnp.where(eq, jnp.where(rank <= need, 0.0, NEG), jnp.where(kk > thr, 0.0, NEG))
        kpos = r0 + key_row - cfg.lane_off
        ok = (kpos >= 0) & (kpos < cfg.n_pos) & (_chunk_of(kpos) <= q_chunk_t)
        madd_ref[kb] = jnp.where(ok, sel, NEG).T
        return carry + jnp.sum(eqf, axis=0, keepdims=True)

    lax.fori_loop(0, nkb, mask_block, jnp.zeros((1, tq), F32))

    def logits(g, p, kb, near_j):
        sl = slice(p * LANE, (p + 1) * LANE)
        r0 = pl.multiple_of(kb * KEY_BLOCK, KEY_BLOCK)
        s = lax.dot_general(qm_ref[p], k_ref[pl.ds(r0, KEY_BLOCK), sl], NT_DIMS,
                            preferred_element_type=F32)
        ma = madd_ref[kb]
        if near_j is None:
            add = jnp.concatenate([ma, ma], axis=0)
        else:
            add = jnp.concatenate([ma + bias_ref[2 * p, near_j], ma + bias_ref[2 * p + 1, near_j]], axis=0)
        s = s + add
        s_ref[g, kb] = s
        mx_ref[g] = jnp.maximum(mx_ref[g], jnp.maximum(s[:, 0:LANE], s[:, LANE:KEY_BLOCK]))

    def weighted_values(g, p, kb):
        sl = slice(p * LANE, (p + 1) * LANE)
        r0 = pl.multiple_of(kb * KEY_BLOCK, KEY_BLOCK)
        m = mx_ref[g]
        pe = jnp.exp(s_ref[g, kb] - jnp.concatenate([m] * HALF, axis=1))
        l_ref[g] = l_ref[g] + (pe[:, 0:LANE] + pe[:, LANE:KEY_BLOCK])
        acc_ref[g] = acc_ref[g] + jnp.dot(pe.astype(BF16), v_ref[pl.ds(r0, KEY_BLOCK), sl],
                                          preferred_element_type=F32)

    for grp in range(N_PAIRS // PAIR_GROUP):
        pairs = [(g, grp * PAIR_GROUP + g) for g in range(PAIR_GROUP)]
        for g, _ in pairs:
            mx_ref[g] = jnp.full((2 * tq, LANE), M_INIT, F32)

        def far_block(kb, carry, pairs=pairs):
            for g, p in pairs:
                logits(g, p, kb, None)
            return carry

        lax.fori_loop(0, near0, far_block, 0)
        for j in range(cfg.n_near):
            for g, p in pairs:
                logits(g, p, near0 + j, j)

        for g, _ in pairs:
            mx_ref[g] = jnp.broadcast_to(jnp.max(mx_ref[g], axis=1, keepdims=True), (2 * tq, LANE))
            l_ref[g] = jnp.zeros((2 * tq, LANE), F32)
            acc_ref[g] = jnp.zeros((2 * tq, LANE), F32)

        def value_block(kb, carry, pairs=pairs):
            for g, p in pairs:
                weighted_values(g, p, kb)
            return carry

        lax.fori_loop(0, nkb, value_block, 0)

        for g, p in pairs:
            o = acc_ref[g] / jnp.sum(l_ref[g], axis=1, keepdims=True)
            o = jnp.where(low_half, o[0:tq], o[tq:2 * tq])
            z = za_ref[:, p * LANE:(p + 1) * LANE].astype(F32)
            o_ref[:, p * LANE:(p + 1) * LANE] = (o * _silu(z)).astype(BF16)


def _attention(cfg, q, qi, w, za, kcat, vcat, kxcat, bias, tri):
    nb, t, _ = q.shape
    tq, s_pad = cfg.tq, cfg.s_pad
    assert t == tq * cfg.n_qt and kcat.shape[1] == s_pad
    qspec = lambda width: pl.BlockSpec((None, tq, width), lambda b, i: (b, i, 0))
    kspec = lambda width: pl.BlockSpec((None, s_pad, width), lambda b, i: (b, 0, 0),
                                       pipeline_mode=pl.Buffered(1))
    nkb_max = s_pad // KEY_BLOCK
    return pl.pallas_call(
        functools.partial(_attn_kernel, cfg=cfg),
        grid=(nb, cfg.n_qt),
        in_specs=[
            qspec(D_ATT), qspec(D_ATT), qspec(LANE), qspec(D_ATT),
            kspec(D_ATT), kspec(D_ATT), kspec(LANE),
            pl.BlockSpec((N_HEADS, cfg.n_near, tq, KEY_BLOCK), lambda b, i: (0, 0, 0, 0),
                         pipeline_mode=pl.Buffered(1)),
            pl.BlockSpec((KEY_BLOCK, KEY_BLOCK), lambda b, i: (0, 0)),
        ],
        out_specs=qspec(D_ATT),
        out_shape=jax.ShapeDtypeStruct((nb, t, D_ATT), BF16),
        scratch_shapes=[
            pltpu.VMEM((N_PAIRS, 2 * tq, LANE), BF16),
            pltpu.VMEM((N_PAIRS, 2 * tq, LANE), BF16),
            pltpu.VMEM((IDX_HEADS, tq, LANE), F32),
            pltpu.VMEM((s_pad, tq), jnp.int32),
            pltpu.VMEM((nkb_max, tq, KEY_BLOCK), F32),
            pltpu.VMEM((PAIR_GROUP, nkb_max, 2 * tq, KEY_BLOCK), F32),
            pltpu.VMEM((PAIR_GROUP, 2 * tq, LANE), F32),
            pltpu.VMEM((PAIR_GROUP, 2 * tq, LANE), F32),
            pltpu.VMEM((PAIR_GROUP, 2 * tq, LANE), F32),
        ],
        compiler_params=pltpu.CompilerParams(
            dimension_semantics=("arbitrary", "arbitrary"), vmem_limit_bytes=VMEM_LIMIT),
        name="attention",
    )(q, qi, w, za, kcat, vcat, kxcat, bias, tri)


def _forward(x_prompt, x_sample, cache_k, cache_v, cache_kidx, state_conv, meta_tokens,
             norm_g, w_in, conv_w, conv_b, conv_ln_g, conv_ln_b, w_out, rel_bias, final_g,
             *, tq_frames=256, tm_frames=512):
    depth = w_in.shape[0]
    bp, seq, d = x_prompt.shape
    bs, dec, _ = x_sample.shape
    past = cache_k.shape[2] - N_META
    assert seq % tq_frames == 0 and seq % CHUNK == 0 and (bp * seq) % tm_frames == 0

    n_qt = seq // tq_frames
    cfg_f = _AttnCfg(tq=tq_frames, n_rows=tq_frames, n_qt=n_qt, s_pad=LANE + seq + LANE,
                     lane_off=LANE - N_META, n_pos=N_META + seq, q_pos0=N_META,
                     near_step=tq_frames // KEY_BLOCK, near_base=0,
                     n_near=tq_frames // KEY_BLOCK + 1, top_k=min(TOPK_MAX, seq // 4))
    n_pos_s = N_META + past + dec
    s_pad_s = -(-n_pos_s // KEY_BLOCK) * KEY_BLOCK
    cfg_s = _AttnCfg(tq=LANE, n_rows=dec, n_qt=1, s_pad=s_pad_s, lane_off=s_pad_s - n_pos_s,
                     n_pos=n_pos_s, q_pos0=N_META + past, near_step=0,
                     near_base=s_pad_s // KEY_BLOCK - 1, n_near=1,
                     top_k=min(TOPK_MAX, (past + dec) // 4))
    cfg_m = _AttnCfg(tq=LANE, n_rows=N_META, n_qt=1, s_pad=KEY_BLOCK, lane_off=KEY_BLOCK - N_META,
                     n_pos=N_META, q_pos0=0, near_step=0, near_base=0, n_near=1,
                     top_k=min(TOPK_MAX, seq // 4))
    assert dec <= LANE and dec + MAX_DISTANCE <= KEY_BLOCK

    bias_f = _bias_tiles(cfg_f.bucket_table(), rel_bias)
    bias_s = _bias_tiles(cfg_s.bucket_table(), rel_bias)
    bias_m = _bias_tiles(cfg_m.bucket_table(), rel_bias)
    tri = (jnp.arange(KEY_BLOCK)[None, :] <= jnp.arange(KEY_BLOCK)[:, None]).astype(BF16)

    n_aux = bs * dec + N_META
    hf = x_prompt.reshape(bp * seq, d)
    haux = jnp.concatenate([x_sample.reshape(bs * dec, d), meta_tokens.astype(x_sample.dtype)], axis=0)

    outs = {name: [] for name in ("kp", "vp", "kip", "cp", "ks", "vs", "kis", "cs")}
    for l in range(depth):
        w_main = w_in[l, :, :N_GROUPS * GROUP].astype(BF16)
        w_tail = jnp.pad(w_in[l, :, N_GROUPS * GROUP:], ((0, 0), (0, LANE - IDX_DIM - IDX_HEADS))).astype(BF16)
        g = norm_g[l][None, :]
        cw = jnp.pad(conv_w[l], ((0, HIST_ROWS - CONV_WIDTH), (0, 0)))
        cb, lg, lb = conv_b[l][None, :], conv_ln_g[l][None, :], conv_ln_b[l][None, :]
        wc = w_out[l, :D_CONV].astype(BF16)
        wa = w_out[l, D_CONV:].astype(BF16)
        final = l == depth - 1
        fg = final_g[None, :]

        uA, zcA, qA, kA, vA, zaA, qiA, tailA = _inproj(haux, g, w_main, w_tail, n_aux)
        uF, zcF, qF, kF, vF, zaF, qiF, tailF = _inproj(hf, g, w_main, w_tail, tm_frames)
        ns = bs * dec
        split = lambda a: (a[:ns].reshape(bs, dec, -1), a[ns:][None])
        uS, uM = split(uA); zcS, zcM = split(zcA); qS, qM = split(qA); kS, kM = split(kA)
        vS, vM = split(vA); zaS, zaM = split(zaA); qiS, qiM = split(qiA); tailS, tailM = split(tailA)
        b3 = lambda a: a.reshape(bp, seq, -1)
        uF, zcF, qF, kF, vF, zaF, qiF, tailF = map(b3, (uF, zcF, qF, kF, vF, zaF, qiF, tailF))

        zero_hist = jnp.zeros((1, HIST_ROWS, D_CONV), F32)
        cM = _conv(uM, zero_hist, zcM, cw, cb, lg, lb)
        hist_f = jnp.concatenate([jnp.zeros((1, HIST_ROWS - N_META, D_CONV), F32), uM], axis=1)
        cF = _conv(uF, hist_f, zcF, cw, cb, lg, lb)
        hist_s = jnp.pad(state_conv[l].astype(F32), ((0, 0), (HIST_PAD, 0), (0, 0)))
        cS = _conv(uS, hist_s, zcS, cw, cb, lg, lb)

        kidx = lambda tail: tail[..., :IDX_DIM]
        twice = lambda a: jnp.concatenate([a, a], axis=-1).astype(BF16)

        def lay(parts, width, nb, cfg):
            lead = jnp.zeros((nb, cfg.lane_off, width), BF16)
            body = [jnp.broadcast_to(a, (nb,) + a.shape[1:]).astype(BF16) for a in parts]
            n_tail = cfg.s_pad - cfg.lane_off - sum(a.shape[1] for a in parts)
            tailz = [jnp.zeros((nb, n_tail, width), BF16)] if n_tail else []
            return jnp.concatenate([lead] + body + tailz, axis=1)

        def attend(cfg, q, qi, tail, za, k_parts, v_parts, kx_parts, bias, nb):
            rows = lambda a: jnp.pad(a, ((0, 0), (0, cfg.tq * cfg.n_qt - a.shape[1]), (0, 0)))
            out = _attention(cfg, rows(q), rows(qi), rows(tail), rows(za),
                             lay(k_parts, D_ATT, nb, cfg), lay(v_parts, D_ATT, nb, cfg),
                             lay([twice(a) for a in kx_parts], LANE, nb, cfg), bias, tri)
            return out[:, :q.shape[1]]

        aM = attend(cfg_m, qM, qiM, tailM, zaM, [kM], [vM], [kidx(tailM)], bias_m, 1)
        aF = attend(cfg_f, qF, qiF, tailF, zaF, [kM, kF], [vM, vF], [kidx(tailM), kidx(tailF)], bias_f, bp)
        ck = cache_k[l].reshape(bs, N_META + past, D_ATT)
        cv = cache_v[l].reshape(bs, N_META + past, D_ATT)
        aS = attend(cfg_s, qS, qiS, tailS, zaS, [ck, kS], [cv, vS], [cache_kidx[l], kidx(tailS)], bias_s, bs)

        cAux = jnp.concatenate([cS.reshape(ns, D_CONV), cM[0]], axis=0)
        aAux = jnp.concatenate([aS.reshape(ns, D_ATT), aM[0]], axis=0)
        haux = _outproj(cAux, aAux, haux, wc, wa, fg, n_aux, final)
        hf = _outproj(cF.reshape(bp * seq, D_CONV), aF.reshape(bp * seq, D_ATT), hf, wc, wa, fg,
                      tm_frames, final)

        with_meta = lambda m, f: jnp.concatenate([jnp.broadcast_to(m, (bp,) + m.shape[1:]), f], axis=1)
        outs["kp"].append(with_meta(kM, kF).reshape(bp, N_META + seq, N_HEADS, HEAD_DIM))
        outs["vp"].append(with_meta(vM, vF).reshape(bp, N_META + seq, N_HEADS, HEAD_DIM))
        outs["kip"].append(with_meta(kidx(tailM), kidx(tailF)))
        u_ext_p = jnp.concatenate([jnp.zeros((bp, CONV_WIDTH - 1, D_CONV), F32),
                                   jnp.broadcast_to(uM, (bp, N_META, D_CONV)), uF], axis=1)
        outs["cp"].append(u_ext_p[:, -(CONV_WIDTH - 1):])
        outs["ks"].append(kS.reshape(bs, dec, N_HEADS, HEAD_DIM))
        outs["vs"].append(vS.reshape(bs, dec, N_HEADS, HEAD_DIM))
        outs["kis"].append(kidx(tailS))
        u_ext_s = jnp.concatenate([state_conv[l].astype(F32), uS], axis=1)
        outs["cs"].append(u_ext_s[:, -(CONV_WIDTH - 1):])

    y_prompt = hf.reshape(bp, seq, d)
    y_sample = haux[:bs * dec].reshape(bs, dec, d)
    st = lambda name: jnp.stack(outs[name])
    return (y_prompt, y_sample, st("kp"), st("vp"), st("kip"), st("cp"),
            st("ks"), st("vs"), st("kis"), st("cs"))


def kernel(x_prompt, x_sample, cache_k, cache_v, cache_kidx, state_conv, meta_tokens, norm_g, w_in,
           conv_w, conv_b, conv_ln_g, conv_ln_b, w_out, rel_bias, final_g):
    return _forward(x_prompt, x_sample, cache_k, cache_v, cache_kidx, state_conv, meta_tokens,
                    norm_g, w_in, conv_w, conv_b, conv_ln_g, conv_ln_b, w_out, rel_bias, final_g)
```

```python
import functools
import math

import jax
import jax.numpy as jnp
from jax import lax
from jax.experimental import pallas as pl
from jax.experimental.pallas import tpu as pltpu

N_META = 16
CHUNK = 64
CHUNK_SHIFT = 6
D_CONV = 1024
CONV_WIDTH = 31
N_HEADS = 16
HEAD_DIM = 64
D_ATT = N_HEADS * HEAD_DIM
IDX_HEADS = 16
IDX_DIM = 64
TOPK_MAX = 256
N_BUCKETS = 32
MAX_DISTANCE = 128
NORM_EPS = 1e-6
NEG = -1e30

LANE = 128
SUBLANE = 8
KEY_BLOCK = 256
HALF = KEY_BLOCK // LANE
N_PAIRS = N_HEADS // 2
PAIR_BUFFERS = 2
HIST_ROWS = 32
HIST_PAD = HIST_ROWS - (CONV_WIDTH - 1)
GROUP = 1024
N_GROUPS = 8
VMEM_LIMIT = 56 * 1024 * 1024
INT_MIN = -2147483648
M_INIT = -1e29
FAR_BUCKET = N_BUCKETS // 2 - 1

F32 = jnp.float32
BF16 = jnp.bfloat16
NT_DIMS = (((1,), (1,)), ((), ()))


def _silu(x):
    return x * jax.nn.sigmoid(x)


def _inproj_kernel(x_ref, g_ref, wm_ref, wt_ref,
                   u_ref, zc_ref, q_ref, k_ref, v_ref, za_ref, qi_ref, tail_ref, kb_ref, vb_ref,
                   xn_ref, a_ref):
    n = pl.program_id(1)

    @pl.when(n == 0)
    def _():
        x = x_ref[...]
        ms = jnp.mean(x * x, axis=-1, keepdims=True)
        xn_ref[...] = (x * lax.rsqrt(ms + NORM_EPS) * g_ref[...]).astype(BF16)

    def group():
        return jnp.dot(xn_ref[...], wm_ref[...], preferred_element_type=F32)

    @pl.when(n == 0)
    def _():
        a_ref[...] = group()

    @pl.when(n == 1)
    def _():
        u_ref[...] = a_ref[...] * jax.nn.sigmoid(group())

    @pl.when(n == 2)
    def _():
        zc_ref[...] = group().astype(BF16)

    @pl.when(n == 3)
    def _():
        q_ref[...] = (group() * (HEAD_DIM ** -0.5)).astype(BF16)

    @pl.when(n == 4)
    def _():
        y = group()
        k_ref[...] = y
        kb_ref[...] = y.astype(BF16)

    @pl.when(n == 5)
    def _():
        y = group()
        v_ref[...] = y
        vb_ref[...] = y.astype(BF16)

    @pl.when(n == 6)
    def _():
        za_ref[...] = group().astype(BF16)

    @pl.when(n == 7)
    def _():
        qi_ref[...] = group().astype(BF16)

    @pl.when(n == N_GROUPS)
    def _():
        tail_ref[...] = jnp.dot(xn_ref[...], wt_ref[...], preferred_element_type=F32)


def _inproj(x2d, g, w_all, layer, w_tail, tm):
    m, d = x2d.shape
    assert m % tm == 0 and w_all.shape[2] >= N_GROUPS * GROUP
    row = lambda width, dtype: jax.ShapeDtypeStruct((m, width), dtype)
    out_shape = (row(GROUP, F32), row(GROUP, BF16), row(GROUP, BF16), row(GROUP, F32),
                 row(GROUP, F32), row(GROUP, BF16), row(GROUP, BF16), row(LANE, F32),
                 row(GROUP, BF16), row(GROUP, BF16))
    ospec = lambda width, n_w: pl.BlockSpec(
        (tm, width), lambda i, n: (jnp.maximum(i - jnp.where(n < n_w, 1, 0), 0), 0))
    return pl.pallas_call(
        _inproj_kernel,
        grid=(m // tm, N_GROUPS + 1),
        in_specs=[
            pl.BlockSpec((tm, d), lambda i, n: (i, 0)),
            pl.BlockSpec((1, d), lambda i, n: (0, 0)),
            pl.BlockSpec((None, d, GROUP), lambda i, n: (layer, 0, jnp.minimum(n, N_GROUPS - 1))),
            pl.BlockSpec((d, LANE), lambda i, n: (0, 0)),
        ],
        out_specs=([ospec(GROUP, n_w) for n_w in range(1, N_GROUPS)] + [ospec(LANE, N_GROUPS)]
                   + [ospec(GROUP, 4), ospec(GROUP, 5)]),
        out_shape=out_shape,
        scratch_shapes=[pltpu.VMEM((tm, d), BF16), pltpu.VMEM((tm, GROUP), F32)],
        compiler_params=pltpu.CompilerParams(
            dimension_semantics=("arbitrary", "arbitrary"), vmem_limit_bytes=VMEM_LIMIT),
        name="inproj",
    )(x2d, g, w_all, w_tail)


def _conv_kernel(u_ref, hist_ref, zc_ref, cw_ref, cb_ref, lg_ref, lb_ref, o_ref, ext_ref, c_ref,
                 *, tt, rc):
    t = pl.program_id(1)
    n_hist = CONV_WIDTH - 1

    for b in range(SUBLANE):
        keep = n_hist - b

        @pl.when(t == 0)
        def _(b=b, keep=keep):
            ext_ref[b, 0:keep, :] = hist_ref[HIST_PAD + b:HIST_ROWS, :]

        @pl.when(t > 0)
        def _(b=b, keep=keep):
            ext_ref[b, 0:keep, :] = ext_ref[b, tt:tt + keep, :]

    for b in range(SUBLANE):
        ext_ref[b, n_hist - b:n_hist - b + tt, :] = u_ref[...]

    half_c = D_CONV // 2

    def taps(r, carry):
        r0 = pl.multiple_of(r * (2 * SUBLANE), 2 * SUBLANE)
        for c0 in (0, half_c):
            cs = slice(c0, c0 + half_c)
            acc = [jnp.broadcast_to(cb_ref[:, cs], (SUBLANE, half_c)) for _ in range(2)]
            for b in range(SUBLANE):
                n_a = (CONV_WIDTH - 1 - b) // SUBLANE + 1
                win = ext_ref[b, pl.ds(r0, (n_a + 1) * SUBLANE), cs]
                for a in range(n_a):
                    j = a * SUBLANE + b
                    w8 = cw_ref[j * SUBLANE:(j + 1) * SUBLANE, cs]
                    for g in range(2):
                        lo = (a + g) * SUBLANE
                        acc[g] = acc[g] + w8 * win[lo:lo + SUBLANE, :]
            for g in range(2):
                c_ref[pl.ds(r0 + g * SUBLANE, SUBLANE), cs] = acc[g]
        return carry

    lax.fori_loop(0, tt // (2 * SUBLANE), taps, 0)

    def norm_gate(r, carry):
        r0 = pl.multiple_of(r * rc, rc)
        c = c_ref[pl.ds(r0, rc), :]
        mu = jnp.mean(c, axis=-1, keepdims=True)
        xc = c - mu
        var = jnp.mean(xc * xc, axis=-1, keepdims=True)
        y = xc * lax.rsqrt(var + NORM_EPS) * lg_ref[...] + lb_ref[...]
        z = zc_ref[pl.ds(r0, rc), :].astype(F32)
        o_ref[pl.ds(r0, rc), :] = (_silu(y) * _silu(z)).astype(BF16)
        return carry

    lax.fori_loop(0, tt // rc, norm_gate, 0)


def _conv(u, hist, zc, cw, cb, lg, lb):
    nb, t, c = u.shape
    tt = min(t, 256)
    rc = min(tt, 64)
    assert t % tt == 0 and tt % rc == 0 and tt % (2 * SUBLANE) == 0 and c == D_CONV
    hist_map = (lambda b, i: (b, 0, 0)) if hist.shape[0] == nb else (lambda b, i: (0, 0, 0))
    vec = pl.BlockSpec((1, c), lambda b, i: (0, 0))
    return pl.pallas_call(
        functools.partial(_conv_kernel, tt=tt, rc=rc),
        grid=(nb, t // tt),
        in_specs=[
            pl.BlockSpec((None, tt, c), lambda b, i: (b, i, 0)),
            pl.BlockSpec((None, HIST_ROWS, c), hist_map),
            pl.BlockSpec((None, tt, c), lambda b, i: (b, i, 0)),
            pl.BlockSpec((CONV_WIDTH * SUBLANE, c), lambda b, i: (0, 0)),
            vec, vec, vec,
        ],
        out_specs=pl.BlockSpec((None, tt, c), lambda b, i: (b, i, 0)),
        out_shape=jax.ShapeDtypeStruct((nb, t, c), BF16),
        scratch_shapes=[pltpu.VMEM((SUBLANE, HIST_ROWS + tt, c), F32), pltpu.VMEM((tt, c), F32)],
        compiler_params=pltpu.CompilerParams(
            dimension_semantics=("arbitrary", "arbitrary"), vmem_limit_bytes=VMEM_LIMIT),
        name="conv",
    )(u, hist, zc, cw, cb, lg, lb)


def _outproj_kernel(c_ref, a_ref, h_ref, wc_ref, wa_ref, fg_ref, o_ref, *, final):
    y = (jnp.dot(c_ref[...], wc_ref[...], preferred_element_type=F32)
         + jnp.dot(a_ref[...], wa_ref[...], preferred_element_type=F32))
    y = h_ref[...] + y
    if final:
        ms = jnp.mean(y * y, axis=-1, keepdims=True)
        y = y * lax.rsqrt(ms + NORM_EPS) * fg_ref[...]
    o_ref[...] = y


def _outproj(c, a, h, wo_all, layer, fg, tm, final):
    m, d = h.shape
    assert m % tm == 0 and D_CONV == D_ATT
    half = lambda: pl.BlockSpec((tm, D_CONV), lambda i: (i, 0))
    full = lambda: pl.BlockSpec((tm, d), lambda i: (i, 0))
    wspec = lambda part: pl.BlockSpec((None, D_CONV, d), lambda i: (layer, part, 0))
    return pl.pallas_call(
        functools.partial(_outproj_kernel, final=final),
        grid=(m // tm,),
        in_specs=[half(), half(), full(), wspec(0), wspec(1), pl.BlockSpec((1, d), lambda i: (0, 0))],
        out_specs=full(),
        out_shape=jax.ShapeDtypeStruct((m, d), F32),
        compiler_params=pltpu.CompilerParams(
            dimension_semantics=("arbitrary",), vmem_limit_bytes=VMEM_LIMIT),
        name="outproj",
    )(c, a, h, wo_all, wo_all, fg)


def _bias_kernel(bucket_ref, rb_ref, o_ref, *, n_near, tq):
    h = pl.program_id(0)
    far = rb_ref[FAR_BUCKET, h]
    for j in range(n_near):
        def body(r, carry):
            r0 = pl.multiple_of(r * SUBLANE, SUBLANE)
            bk = bucket_ref[j, pl.ds(r0, SUBLANE), :]
            acc = jnp.zeros(bk.shape, F32)
            for b in range(N_BUCKETS):
                acc = jnp.where(bk == b, rb_ref[b, h], acc)
            o_ref[j, pl.ds(r0, SUBLANE), :] = acc - far
            return carry
        lax.fori_loop(0, tq // SUBLANE, body, 0)


def _bias_tiles(bucket, rel_bias):
    n_near, tq, kb = bucket.shape
    return pl.pallas_call(
        functools.partial(_bias_kernel, n_near=n_near, tq=tq),
        grid=(N_HEADS,),
        in_specs=[pl.BlockSpec((n_near, tq, kb), lambda h: (0, 0, 0)),
                  pl.BlockSpec(memory_space=pltpu.SMEM)],
        out_specs=pl.BlockSpec((None, n_near, tq, kb), lambda h: (h, 0, 0, 0)),
        out_shape=jax.ShapeDtypeStruct((N_HEADS, n_near, tq, kb), F32),
        compiler_params=pltpu.CompilerParams(dimension_semantics=("arbitrary",)),
        name="bias_tiles",
    )(bucket, rel_bias)


def _t5_bucket(rel):
    half = N_BUCKETS // 2
    max_exact = half // 2
    n = jnp.abs(rel)
    large = max_exact + (jnp.log(jnp.maximum(n, 1).astype(jnp.float32) / max_exact)
                         / math.log(MAX_DISTANCE / max_exact) * (half - max_exact)).astype(jnp.int32)
    large = jnp.minimum(large, half - 1)
    return jnp.where(rel > 0, half, 0) + jnp.where(n < max_exact, n, large)


class _AttnCfg:
    def __init__(self, tq, n_rows, n_qt, s_pad, lane_off, n_pos, q_pos0, near_step, near_base, n_near, top_k,
                 edge_block, main_shift, n_main, dims_major=False):
        self.dims_major = dims_major
        self.edge_block, self.main_shift, self.n_main = edge_block, main_shift, n_main
        assert all(kb == edge_block or 0 <= kb - main_shift < n_main for kb in range(s_pad // KEY_BLOCK))
        self.tq, self.n_rows, self.n_qt, self.s_pad, self.lane_off = tq, n_rows, n_qt, s_pad, lane_off
        self.n_pos, self.q_pos0 = n_pos, q_pos0
        self.near_step, self.near_base, self.n_near, self.top_k = near_step, near_base, n_near, top_k
        self.rel0 = near_base * KEY_BLOCK - lane_off - q_pos0
        assert s_pad % KEY_BLOCK == 0 and tq % LANE == 0 and n_rows <= tq
        assert near_step * KEY_BLOCK == tq or n_qt == 1
        assert (near_base == 0 and near_step == 0) or 1 - self.rel0 >= MAX_DISTANCE
        assert (near_base + near_step * (n_qt - 1) + n_near) * KEY_BLOCK <= s_pad

    def bucket_table(self):
        j = jnp.arange(self.n_near, dtype=jnp.int32)[:, None, None]
        r = jnp.arange(self.tq, dtype=jnp.int32)[None, :, None]
        c = jnp.arange(KEY_BLOCK, dtype=jnp.int32)[None, None, :]
        rel = self.rel0 + KEY_BLOCK * j + c - r
        return _t5_bucket(lax.optimization_barrier(rel))


def _chunk_of(pos):
    return jnp.where(pos < N_META, 0, 1 + ((pos - N_META) >> CHUNK_SHIFT))


def _for_blocks(n, fn):
    def two(k2, carry):
        fn(2 * k2)
        fn(2 * k2 + 1)
        return carry

    if isinstance(n, int):
        lax.fori_loop(0, n // 2, two, 0)
        if n % 2:
            fn(n - 1)
    else:
        lax.fori_loop(0, n >> 1, two, 0)

        @pl.when((n & 1) == 1)
        def _():
            fn(n - 1)


def _attn_kernel(q_ref, qi_ref, w_ref, za_ref, k_ref, v_ref, kx_ref, ke_ref, ve_ref, kxe_ref,
                 bias_ref, tri_ref, o_ref,
                 qm_ref, qim_ref, wb_ref, keys_ref, madd_ref, s_ref, mx_ref, l_ref, acc_ref, ties_ref,
                 *, cfg):
    tq = cfg.tq
    i = pl.program_id(1)
    near0 = cfg.near_base + cfg.near_step * i if cfg.near_step else cfg.near_base
    nkb = near0 + cfg.n_near
    lane = lax.broadcasted_iota(jnp.int32, (tq, LANE), 1)
    row = lax.broadcasted_iota(jnp.int32, (tq, LANE), 0)
    low_half = lane < HEAD_DIM
    q_pos_base = cfg.q_pos0 + i * tq
    q_chunk = _chunk_of(q_pos_base + row)

    def admissible(blk):
        kpos = blk * LANE + lane - cfg.lane_off
        return (kpos >= 0) & (kpos < cfg.n_pos) & (_chunk_of(kpos) <= q_chunk)

    def key_operand(main_ref, edge_ref, fs, kb):
        def main(idx):
            ks = pl.ds(pl.multiple_of(idx * KEY_BLOCK, KEY_BLOCK), KEY_BLOCK)
            return main_ref[fs, ks] if cfg.dims_major else main_ref[ks, fs]

        edge = edge_ref[fs, :] if cfg.dims_major else edge_ref[:, fs]
        if isinstance(kb, int):
            return edge if kb == cfg.edge_block else main(kb - cfg.main_shift)
        if cfg.n_main == 0:
            return edge
        return jnp.where(kb == cfg.edge_block, edge, main(jnp.clip(kb - cfg.main_shift, 0, cfg.n_main - 1)))

    def dot_keys(lhs, rhs):
        if cfg.dims_major:
            return jnp.dot(lhs, rhs, preferred_element_type=F32)
        return lax.dot_general(lhs, rhs, NT_DIMS, preferred_element_type=F32)

    def dot_values(lhs, rhs):
        if cfg.dims_major:
            return lax.dot_general(lhs, rhs, NT_DIMS, preferred_element_type=F32)
        return jnp.dot(lhs, rhs, preferred_element_type=F32)

    all_feat = slice(0, LANE)
    pair_feat = lambda p: slice(p * LANE, (p + 1) * LANE)

    zero = jnp.zeros((tq, LANE), BF16)
    for p in range(N_PAIRS):
        sl = slice(p * LANE, (p + 1) * LANE)
        qp = q_ref[:, sl]
        qm_ref[p, 0:tq, :] = jnp.where(low_half, qp, zero)
        qm_ref[p, tq:2 * tq, :] = jnp.where(low_half, zero, qp)
        qip = qi_ref[:, sl]
        qim_ref[p, 0:tq, :] = jnp.where(low_half, qip, zero)
        qim_ref[p, tq:2 * tq, :] = jnp.where(low_half, zero, qip)
    w_scale = (IDX_HEADS ** -0.5) * (IDX_DIM ** -0.5)
    for h in range(IDX_HEADS):
        col = IDX_DIM + h
        wb_ref[h] = jnp.broadcast_to(w_ref[:, col:col + 1] * w_scale, (tq, LANE))

    def index_block(kb):
        kx = key_operand(kx_ref, kxe_ref, all_feat, kb)
        acc = [jnp.zeros((tq, LANE), F32) for _ in range(HALF)]
        for p in range(N_PAIRS):
            d = jnp.maximum(dot_keys(qim_ref[p], kx), 0.0)
            for hf in range(HALF):
                cs = slice(hf * LANE, (hf + 1) * LANE)
                acc[hf] = acc[hf] + wb_ref[2 * p] * d[0:tq, cs] + wb_ref[2 * p + 1] * d[tq:2 * tq, cs]
        for hf in range(HALF):
            blk = kb * HALF + hf
            sc = jnp.where(admissible(blk), acc[hf] + 0.0, NEG)
            bits = pltpu.bitcast(sc.T, jnp.int32)
            keys_ref[pl.ds(pl.multiple_of(blk * LANE, LANE), LANE), :] = (
                jnp.where(bits < 0, bits ^ jnp.int32(0x7FFFFFFF), bits))

    _for_blocks(nkb, index_block)

    kf = jnp.float32(cfg.top_k)
    n_part = 4

    def count(pred):
        def body(c_i, parts):
            chunk = keys_ref[pl.ds(pl.multiple_of(c_i * LANE, LANE), LANE), :]
            parts = list(parts)
            for j in range(LANE // SUBLANE):
                kk = chunk[j * SUBLANE:(j + 1) * SUBLANE, :]
                parts[j % n_part] = parts[j % n_part] + jnp.where(pred(kk), 1.0, 0.0)
            return tuple(parts)
        parts = lax.fori_loop(0, nkb * HALF, body, tuple(jnp.zeros((SUBLANE, tq), F32) for _ in range(n_part)))
        tot = (parts[0] + parts[1]) + (parts[2] + parts[3])
        return jnp.broadcast_to(jnp.sum(tot, axis=0, keepdims=True), (SUBLANE, tq))

    def bisect(it, lo):
        cand = lo + lax.shift_left(jnp.int32(1), 31 - it)
        return jnp.where(count(lambda kk: kk >= cand) >= kf, cand, lo)

    thr8 = lax.fori_loop(0, 32, bisect, jnp.full((SUBLANE, tq), INT_MIN, jnp.int32))
    need8 = kf - count(lambda kk: kk > thr8)
    thr = thr8[0:1, :]
    need = need8[0:1, :]

    key_row = lax.broadcasted_iota(jnp.int32, (KEY_BLOCK, tq), 0)
    q_chunk_t = _chunk_of(q_pos_base + lax.broadcasted_iota(jnp.int32, (KEY_BLOCK, tq), 1))

    ties_ref[...] = jnp.zeros((SUBLANE, tq), F32)

    def mask_block(kb):
        r0 = pl.multiple_of(kb * KEY_BLOCK, KEY_BLOCK)
        kk = keys_ref[pl.ds(r0, KEY_BLOCK), :]
        eq = kk == thr
        eqf = jnp.where(eq, 1.0, 0.0)
        seen = ties_ref[...]
        rank = seen[0:1, :] + jnp.dot(tri_ref[...], eqf.astype(BF16), preferred_element_type=F32)
        sel = jnp.where(eq, jnp.where(rank <= need, 0.0, NEG), jnp.where(kk > thr, 0.0, NEG))
        kpos = r0 + key_row - cfg.lane_off
        ok = (kpos >= 0) & (kpos < cfg.n_pos) & (_chunk_of(kpos) <= q_chunk_t)
        madd_ref[kb] = jnp.where(ok, sel, NEG).T
        ties_ref[...] = seen + jnp.sum(eqf, axis=0, keepdims=True)

    _for_blocks(nkb, mask_block)

    def logits(g, p, kb, near_j):
        s = dot_keys(qm_ref[p], key_operand(k_ref, ke_ref, pair_feat(p), kb))
        ma = madd_ref[kb]
        if near_j is None:
            add = jnp.concatenate([ma, ma], axis=0)
        else:
            add = jnp.concatenate([ma + bias_ref[2 * p, near_j], ma + bias_ref[2 * p + 1, near_j]], axis=0)
        s = s + add
        s_ref[g, kb] = s
        mx_ref[g] = jnp.maximum(mx_ref[g], jnp.maximum(s[:, 0:LANE], s[:, LANE:KEY_BLOCK]))

    def weighted_values(g, p, kb):
        m = mx_ref[g]
        pe = jnp.exp(s_ref[g, kb] - jnp.concatenate([m] * HALF, axis=1))
        l_ref[g] = l_ref[g] + (pe[:, 0:LANE] + pe[:, LANE:KEY_BLOCK])
        acc_ref[g] = acc_ref[g] + dot_values(pe.astype(BF16), key_operand(v_ref, ve_ref, pair_feat(p), kb))

    for stage in range(N_PAIRS + 1):
        p_a = stage if stage < N_PAIRS else None
        p_b = stage - 1 if stage > 0 else None

        def both(kb, near_j, p_a=p_a, p_b=p_b):
            if p_a is not None:
                logits(p_a % 2, p_a, kb, near_j)
            if p_b is not None:
                weighted_values(p_b % 2, p_b, kb)

        if p_a is not None:
            mx_ref[p_a % 2] = jnp.full((2 * tq, LANE), M_INIT, F32)

        _for_blocks(near0, lambda kb, both=both: both(kb, None))
        for j in range(cfg.n_near):
            both(near0 + j, j)

        if p_a is not None:
            g = p_a % 2
            mx_ref[g] = jnp.broadcast_to(jnp.max(mx_ref[g], axis=1, keepdims=True), (2 * tq, LANE))
            l_ref[g] = jnp.zeros((2 * tq, LANE), F32)
            acc_ref[g] = jnp.zeros((2 * tq, LANE), F32)
        if p_b is not None:
            g = p_b % 2
            o = acc_ref[g] / jnp.sum(l_ref[g], axis=1, keepdims=True)
            o = jnp.where(low_half, o[0:tq], o[tq:2 * tq])
            z = za_ref[:, p_b * LANE:(p_b + 1) * LANE].astype(F32)
            o_ref[:, p_b * LANE:(p_b + 1) * LANE] = (o * _silu(z)).astype(BF16)


def _attention(cfg, q, qi, w, za, mains, edges, bias, tri, main_lead=()):
    nb, t, _ = q.shape
    tq, s_pad = cfg.tq, cfg.s_pad
    assert t == tq * cfg.n_qt
    qspec = lambda width: pl.BlockSpec((None, tq, width), lambda b, i: (b, i, 0))

    def kspec(a, lead):
        per_batch = a.shape[len(lead)] == nb
        assert per_batch or a.shape[len(lead)] == 1
        return pl.BlockSpec((None,) * (len(lead) + 1) + a.shape[-2:],
                            lambda b, i: lead + ((b if per_batch else 0), 0, 0),
                            pipeline_mode=pl.Buffered(1))

    key_axis = -1 if cfg.dims_major else -2
    for a in mains:
        assert a.shape[key_axis] >= cfg.n_main * KEY_BLOCK
    for a in edges:
        assert a.shape[key_axis] == KEY_BLOCK
    nkb_max = s_pad // KEY_BLOCK
    return pl.pallas_call(
        functools.partial(_attn_kernel, cfg=cfg),
        grid=(nb, cfg.n_qt),
        in_specs=[
            qspec(D_ATT), qspec(D_ATT), qspec(LANE), qspec(D_ATT),
            *[kspec(a, tuple(main_lead)) for a in mains],
            *[kspec(a, ()) for a in edges],
            pl.BlockSpec((N_HEADS, cfg.n_near, tq, KEY_BLOCK), lambda b, i: (0, 0, 0, 0),
                         pipeline_mode=pl.Buffered(1)),
            pl.BlockSpec((KEY_BLOCK, KEY_BLOCK), lambda b, i: (0, 0)),
        ],
        out_specs=qspec(D_ATT),
        out_shape=jax.ShapeDtypeStruct((nb, t, D_ATT), BF16),
        scratch_shapes=[
            pltpu.VMEM((N_PAIRS, 2 * tq, LANE), BF16),
            pltpu.VMEM((N_PAIRS, 2 * tq, LANE), BF16),
            pltpu.VMEM((IDX_HEADS, tq, LANE), F32),
            pltpu.VMEM((s_pad, tq), jnp.int32),
            pltpu.VMEM((nkb_max, tq, KEY_BLOCK), F32),
            pltpu.VMEM((PAIR_BUFFERS, nkb_max, 2 * tq, KEY_BLOCK), F32),
            pltpu.VMEM((PAIR_BUFFERS, 2 * tq, LANE), F32),
            pltpu.VMEM((PAIR_BUFFERS, 2 * tq, LANE), F32),
            pltpu.VMEM((PAIR_BUFFERS, 2 * tq, LANE), F32),
            pltpu.VMEM((SUBLANE, tq), F32),
        ],
        compiler_params=pltpu.CompilerParams(
            dimension_semantics=("arbitrary", "arbitrary"), vmem_limit_bytes=VMEM_LIMIT),
        name="attention",
    )(q, qi, w, za, *mains, *edges, bias, tri)


def _edge_block(cfg, parts, lead_zeros):
    key_axis = 2 if cfg.dims_major else 1
    parts = [a.astype(BF16) for a in parts]

    def zeros(n):
        shape = list(parts[0].shape)
        shape[key_axis] = n
        return [jnp.zeros(shape, BF16)] if n else []

    n_end = KEY_BLOCK - lead_zeros - sum(a.shape[key_axis] for a in parts)
    return jnp.concatenate(zeros(lead_zeros) + parts + zeros(n_end), axis=key_axis)


def _attend(cfg, q, qi, tail, za, mains, edges, bias, tri, main_lead=()):
    rows = lambda a: jnp.pad(a, ((0, 0), (0, cfg.tq * cfg.n_qt - a.shape[1]), (0, 0)))
    out = _attention(cfg, rows(q), rows(qi), rows(tail), rows(za), mains, edges, bias, tri, main_lead)
    return out[:, :q.shape[1]]


def _forward(x_prompt, x_sample, cache_k, cache_v, cache_kidx, state_conv, meta_tokens,
             norm_g, w_in, conv_w, conv_b, conv_ln_g, conv_ln_b, w_out, rel_bias, final_g,
             *, tq_frames=256, tm_frames=512):
    depth = w_in.shape[0]
    bp, seq, d = x_prompt.shape
    bs, dec, _ = x_sample.shape
    past = cache_k.shape[2] - N_META
    assert seq % tq_frames == 0 and seq % CHUNK == 0 and (bp * seq) % tm_frames == 0

    n_qt = seq // tq_frames
    cfg_f = _AttnCfg(tq=tq_frames, n_rows=tq_frames, n_qt=n_qt, s_pad=KEY_BLOCK + seq,
                     lane_off=KEY_BLOCK - N_META, n_pos=N_META + seq, q_pos0=N_META,
                     near_step=tq_frames // KEY_BLOCK, near_base=0,
                     n_near=tq_frames // KEY_BLOCK + 1, top_k=min(TOPK_MAX, seq // 4),
                     edge_block=0, main_shift=1, n_main=seq // KEY_BLOCK)
    n_cache = N_META + past
    n_pos_s = n_cache + dec
    s_pad_s = -(-n_pos_s // KEY_BLOCK) * KEY_BLOCK
    last_s = s_pad_s // KEY_BLOCK - 1
    near_s = max(0, (n_cache - MAX_DISTANCE) // KEY_BLOCK)
    cfg_s = _AttnCfg(tq=LANE, n_rows=dec, n_qt=1, s_pad=s_pad_s, lane_off=0,
                     n_pos=n_pos_s, q_pos0=n_cache, near_step=0, near_base=near_s,
                     n_near=last_s - near_s + 1, top_k=min(TOPK_MAX, (past + dec) // 4),
                     edge_block=last_s, main_shift=0, n_main=last_s, dims_major=True)
    cfg_m = _AttnCfg(tq=LANE, n_rows=N_META, n_qt=1, s_pad=KEY_BLOCK, lane_off=KEY_BLOCK - N_META,
                     n_pos=N_META, q_pos0=0, near_step=0, near_base=0, n_near=1,
                     top_k=min(TOPK_MAX, seq // 4), edge_block=0, main_shift=1, n_main=0)
    assert dec <= LANE and n_cache >= last_s * KEY_BLOCK

    bias_f = _bias_tiles(cfg_f.bucket_table(), rel_bias)
    bias_s = _bias_tiles(cfg_s.bucket_table(), rel_bias)
    bias_m = _bias_tiles(cfg_m.bucket_table(), rel_bias)
    tri = (jnp.arange(KEY_BLOCK)[None, :] <= jnp.arange(KEY_BLOCK)[:, None]).astype(BF16)

    n_aux = bs * dec + N_META
    hf = x_prompt.reshape(bp * seq, d)
    haux = jnp.concatenate([x_sample.reshape(bs * dec, d), meta_tokens.astype(x_sample.dtype)], axis=0)

    w_all = w_in.astype(BF16)
    wo_all = w_out.astype(BF16)
    feat_major = lambda c: jnp.transpose(c, (0, 1, 3, 4, 2)).reshape(depth, bs, D_ATT, n_cache).astype(BF16)
    ck_t, cv_t = feat_major(cache_k), feat_major(cache_v)
    cx_t = jnp.swapaxes(cache_kidx, 2, 3).astype(BF16)
    cx_t = jnp.concatenate([cx_t, cx_t], axis=2)
    outs = {name: [] for name in ("kp", "vp", "kip", "cp", "ks", "vs", "kis", "cs")}
    for l in range(depth):
        w_tail = jnp.pad(w_all[l, :, N_GROUPS * GROUP:], ((0, 0), (0, LANE - IDX_DIM - IDX_HEADS)))
        g = norm_g[l][None, :]
        cw = jnp.repeat(conv_w[l], SUBLANE, axis=0)
        cb, lg, lb = conv_b[l][None, :], conv_ln_g[l][None, :], conv_ln_b[l][None, :]
        final = l == depth - 1
        fg = final_g[None, :]

        uA, zcA, qA, kA, vA, zaA, qiA, tailA, kbA, vbA = _inproj(haux, g, w_all, l, w_tail, n_aux)
        uF, zcF, qF, kF, vF, zaF, qiF, tailF, kbF, vbF = _inproj(hf, g, w_all, l, w_tail, tm_frames)
        ns = bs * dec
        split = lambda a: (a[:ns].reshape(bs, dec, -1), a[ns:][None])
        uS, uM = split(uA); zcS, zcM = split(zcA); qS, qM = split(qA); kS, kM = split(kA)
        vS, vM = split(vA); zaS, zaM = split(zaA); qiS, qiM = split(qiA); tailS, tailM = split(tailA)
        kbS, kbM = split(kbA); vbS, vbM = split(vbA)
        b3 = lambda a: a.reshape(bp, seq, -1)
        uF, zcF, qF, kF, vF, zaF, qiF, tailF, kbF, vbF = map(
            b3, (uF, zcF, qF, kF, vF, zaF, qiF, tailF, kbF, vbF))

        zero_hist = jnp.zeros((1, HIST_ROWS, D_CONV), F32)
        cM = _conv(uM, zero_hist, zcM, cw, cb, lg, lb)
        hist_f = jnp.concatenate([jnp.zeros((1, HIST_ROWS - N_META, D_CONV), F32), uM], axis=1)
        cF = _conv(uF, hist_f, zcF, cw, cb, lg, lb)
        hist_s = jnp.pad(state_conv[l].astype(F32), ((0, 0), (HIST_PAD, 0), (0, 0)))
        cS = _conv(uS, hist_s, zcS, cw, cb, lg, lb)

        kidx = lambda tail: tail[..., :IDX_DIM]
        twice = lambda a: jnp.concatenate([a, a], axis=-1).astype(BF16)
        meta_edges = tuple(_edge_block(cfg_f, [a], cfg_f.lane_off) for a in (kbM, vbM, twice(kidx(tailM))))
        aM = _attend(cfg_m, qM, qiM, tailM, zaM, meta_edges, meta_edges, bias_m, tri)
        aF = _attend(cfg_f, qF, qiF, tailF, zaF, (kbF, vbF, twice(kidx(tailF))), meta_edges, bias_f, tri)
        tr = lambda a: jnp.swapaxes(a, 1, 2)
        lo = cfg_s.edge_block * KEY_BLOCK
        sample_edges = tuple(_edge_block(cfg_s, [c[l, :, :, lo:], tr(new)], 0)
                             for c, new in ((ck_t, kbS), (cv_t, vbS), (cx_t, twice(kidx(tailS)))))
        aS = _attend(cfg_s, qS, qiS, tailS, zaS, (ck_t, cv_t, cx_t), sample_edges, bias_s, tri, main_lead=(l,))

        cAux = jnp.concatenate([cS.reshape(ns, D_CONV), cM[0]], axis=0)
        aAux = jnp.concatenate([aS.reshape(ns, D_ATT), aM[0]], axis=0)
        haux = _outproj(cAux, aAux, haux, wo_all, l, fg, n_aux, final)
        hf = _outproj(cF.reshape(bp * seq, D_CONV), aF.reshape(bp * seq, D_ATT), hf, wo_all, l, fg,
                      tm_frames, final)

        with_meta = lambda m, f: jnp.concatenate([jnp.broadcast_to(m, (bp,) + m.shape[1:]), f], axis=1)
        outs["kp"].append(with_meta(kM, kF).reshape(bp, N_META + seq, N_HEADS, HEAD_DIM))
        outs["vp"].append(with_meta(vM, vF).reshape(bp, N_META + seq, N_HEADS, HEAD_DIM))
        outs["kip"].append(with_meta(kidx(tailM), kidx(tailF)))
        u_ext_p = jnp.concatenate([jnp.zeros((bp, CONV_WIDTH - 1, D_CONV), F32),
                                   jnp.broadcast_to(uM, (bp, N_META, D_CONV)), uF], axis=1)
        outs["cp"].append(u_ext_p[:, -(CONV_WIDTH - 1):])
        outs["ks"].append(kS.reshape(bs, dec, N_HEADS, HEAD_DIM))
        outs["vs"].append(vS.reshape(bs, dec, N_HEADS, HEAD_DIM))
        outs["kis"].append(kidx(tailS))
        u_ext_s = jnp.concatenate([state_conv[l].astype(F32), uS], axis=1)
        outs["cs"].append(u_ext_s[:, -(CONV_WIDTH - 1):])

    y_prompt = hf.reshape(bp, seq, d)
    y_sample = haux[:bs * dec].reshape(bs, dec, d)
    st = lambda name: jnp.stack(outs[name])
    return (y_prompt, y_sample, st("kp"), st("vp"), st("kip"), st("cp"),
            st("ks"), st("vs"), st("kis"), st("cs"))


def kernel(x_prompt, x_sample, cache_k, cache_v, cache_kidx, state_conv, meta_tokens, norm_g, w_in,
           conv_w, conv_b, conv_ln_g, conv_ln_b, w_out, rel_bias, final_g):
    return _forward(x_prompt, x_sample, cache_k, cache_v, cache_kidx, state_conv, meta_tokens,
                    norm_g, w_in, conv_w, conv_b, conv_ln_g, conv_ln_b, w_out, rel_bias, final_g)
```

```python
import functools
import math

import jax
import jax.numpy as jnp
from jax import lax
from jax.experimental import pallas as pl
from jax.experimental.pallas import tpu as pltpu

N_META = 16
CHUNK = 64
CHUNK_SHIFT = 6
D_CONV = 1024
CONV_WIDTH = 31
N_HEADS = 16
HEAD_DIM = 64
D_ATT = N_HEADS * HEAD_DIM
IDX_HEADS = 16
IDX_DIM = 64
TOPK_MAX = 256
N_BUCKETS = 32
MAX_DISTANCE = 128
NORM_EPS = 1e-6
NEG = -1e30

LANE = 128
SUBLANE = 8
KEY_BLOCK = 256
HALF = KEY_BLOCK // LANE
N_PAIRS = N_HEADS // 2
PAIR_BUFFERS = 2
HIST_ROWS = 32
HIST_PAD = HIST_ROWS - (CONV_WIDTH - 1)
GROUP = 1024
N_GROUPS = 8
VMEM_LIMIT = 56 * 1024 * 1024
INT_MIN = -2147483648
M_INIT = -1e29
FAR_BUCKET = N_BUCKETS // 2 - 1

F32 = jnp.float32
BF16 = jnp.bfloat16
NT_DIMS = (((1,), (1,)), ((), ()))


def _silu(x):
    return x * jax.nn.sigmoid(x)


def _inproj_kernel(x_ref, g_ref, wm_ref, wt_ref,
                   u_ref, zc_ref, q_ref, k_ref, v_ref, za_ref, qi_ref, tail_ref, kb_ref, vb_ref,
                   xn_ref, a_ref):
    n = pl.program_id(1)
    j = pl.program_id(2)

    @pl.when(n == 0)
    def _():
        x = x_ref[...]
        ms = jnp.mean(x * x, axis=-1, keepdims=True)
        xn_ref[j] = (x * lax.rsqrt(ms + NORM_EPS) * g_ref[...]).astype(BF16)

    def group():
        return jnp.dot(xn_ref[j], wm_ref[...], preferred_element_type=F32)

    @pl.when(n == 0)
    def _():
        a_ref[j] = group()

    @pl.when(n == 1)
    def _():
        u_ref[...] = a_ref[j] * jax.nn.sigmoid(group())

    @pl.when(n == 2)
    def _():
        zc_ref[...] = group().astype(BF16)

    @pl.when(n == 3)
    def _():
        q_ref[...] = (group() * (HEAD_DIM ** -0.5)).astype(BF16)

    @pl.when(n == 4)
    def _():
        y = group()
        k_ref[...] = y
        kb_ref[...] = y.astype(BF16)

    @pl.when(n == 5)
    def _():
        y = group()
        v_ref[...] = y
        vb_ref[...] = y.astype(BF16)

    @pl.when(n == 6)
    def _():
        za_ref[...] = group().astype(BF16)

    @pl.when(n == 7)
    def _():
        qi_ref[...] = group().astype(BF16)

    @pl.when(n == N_GROUPS)
    def _():
        tail_ref[...] = jnp.dot(xn_ref[j], wt_ref[...], preferred_element_type=F32)


def _inproj(x2d, g, w_all, layer, w_tail, tm):
    m, d = x2d.shape
    assert m % tm == 0 and w_all.shape[2] >= N_GROUPS * GROUP
    n_tiles = m // tm
    share = 2 if n_tiles % 2 == 0 else 1
    last = n_tiles - 1
    row = lambda width, dtype: jax.ShapeDtypeStruct((m, width), dtype)
    out_shape = (row(GROUP, F32), row(GROUP, BF16), row(GROUP, BF16), row(GROUP, F32),
                 row(GROUP, F32), row(GROUP, BF16), row(GROUP, BF16), row(LANE, F32),
                 row(GROUP, BF16), row(GROUP, BF16))

    def tile_at(n_w):
        def index(i, n, j):
            first = i * share
            return jnp.where(n < n_w, jnp.maximum(first - 1, 0),
                             jnp.where(n == n_w, first + j, jnp.minimum(first + share - 1, last)))
        return index

    ospec = lambda width, n_w: pl.BlockSpec((tm, width), lambda i, n, j: (tile_at(n_w)(i, n, j), 0))
    return pl.pallas_call(
        _inproj_kernel,
        grid=(n_tiles // share, N_GROUPS + 1, share),
        in_specs=[
            pl.BlockSpec((tm, d), lambda i, n, j: (tile_at(0)(i, n, j), 0)),
            pl.BlockSpec((1, d), lambda i, n, j: (0, 0)),
            pl.BlockSpec((None, d, GROUP), lambda i, n, j: (layer, 0, jnp.minimum(n, N_GROUPS - 1))),
            pl.BlockSpec((d, LANE), lambda i, n, j: (0, 0)),
        ],
        out_specs=([ospec(GROUP, n_w) for n_w in range(1, N_GROUPS)] + [ospec(LANE, N_GROUPS)]
                   + [ospec(GROUP, 4), ospec(GROUP, 5)]),
        out_shape=out_shape,
        scratch_shapes=[pltpu.VMEM((share, tm, d), BF16), pltpu.VMEM((share, tm, GROUP), F32)],
        compiler_params=pltpu.CompilerParams(
            dimension_semantics=("arbitrary", "arbitrary", "arbitrary"), vmem_limit_bytes=VMEM_LIMIT),
        name="inproj",
    )(x2d, g, w_all, w_tail)


def _conv_kernel(u_ref, hist_ref, zc_ref, cw_ref, cb_ref, lg_ref, lb_ref, o_ref, ext_ref, c_ref,
                 *, tt, rc):
    t = pl.program_id(1)
    n_hist = CONV_WIDTH - 1

    for b in range(SUBLANE):
        keep = n_hist - b

        @pl.when(t == 0)
        def _(b=b, keep=keep):
            ext_ref[b, 0:keep, :] = hist_ref[HIST_PAD + b:HIST_ROWS, :]

        @pl.when(t > 0)
        def _(b=b, keep=keep):
            ext_ref[b, 0:keep, :] = ext_ref[b, tt:tt + keep, :]

    for b in range(SUBLANE):
        ext_ref[b, n_hist - b:n_hist - b + tt, :] = u_ref[...]

    half_c = D_CONV // 2

    def taps(r, carry):
        r0 = pl.multiple_of(r * (2 * SUBLANE), 2 * SUBLANE)
        for c0 in (0, half_c):
            cs = slice(c0, c0 + half_c)
            acc = [jnp.broadcast_to(cb_ref[:, cs], (SUBLANE, half_c)) for _ in range(2)]
            for b in range(SUBLANE):
                n_a = (CONV_WIDTH - 1 - b) // SUBLANE + 1
                win = ext_ref[b, pl.ds(r0, (n_a + 1) * SUBLANE), cs]
                for a in range(n_a):
                    j = a * SUBLANE + b
                    w8 = cw_ref[j * SUBLANE:(j + 1) * SUBLANE, cs]
                    for g in range(2):
                        lo = (a + g) * SUBLANE
                        acc[g] = acc[g] + w8 * win[lo:lo + SUBLANE, :]
            for g in range(2):
                c_ref[pl.ds(r0 + g * SUBLANE, SUBLANE), cs] = acc[g]
        return carry

    lax.fori_loop(0, tt // (2 * SUBLANE), taps, 0)

    def norm_gate(r, carry):
        r0 = pl.multiple_of(r * rc, rc)
        c = c_ref[pl.ds(r0, rc), :]
        mu = jnp.mean(c, axis=-1, keepdims=True)
        xc = c - mu
        var = jnp.mean(xc * xc, axis=-1, keepdims=True)
        y = xc * lax.rsqrt(var + NORM_EPS) * lg_ref[...] + lb_ref[...]
        z = zc_ref[pl.ds(r0, rc), :].astype(F32)
        o_ref[pl.ds(r0, rc), :] = (_silu(y) * _silu(z)).astype(BF16)
        return carry

    lax.fori_loop(0, tt // rc, norm_gate, 0)


def _conv(u, hist, zc, cw, cb, lg, lb):
    nb, t, c = u.shape
    tt = min(t, 256)
    rc = min(tt, 64)
    assert t % tt == 0 and tt % rc == 0 and tt % (2 * SUBLANE) == 0 and c == D_CONV
    hist_map = (lambda b, i: (b, 0, 0)) if hist.shape[0] == nb else (lambda b, i: (0, 0, 0))
    vec = pl.BlockSpec((1, c), lambda b, i: (0, 0))
    return pl.pallas_call(
        functools.partial(_conv_kernel, tt=tt, rc=rc),
        grid=(nb, t // tt),
        in_specs=[
            pl.BlockSpec((None, tt, c), lambda b, i: (b, i, 0)),
            pl.BlockSpec((None, HIST_ROWS, c), hist_map),
            pl.BlockSpec((None, tt, c), lambda b, i: (b, i, 0)),
            pl.BlockSpec((CONV_WIDTH * SUBLANE, c), lambda b, i: (0, 0)),
            vec, vec, vec,
        ],
        out_specs=pl.BlockSpec((None, tt, c), lambda b, i: (b, i, 0)),
        out_shape=jax.ShapeDtypeStruct((nb, t, c), BF16),
        scratch_shapes=[pltpu.VMEM((SUBLANE, HIST_ROWS + tt, c), F32), pltpu.VMEM((tt, c), F32)],
        compiler_params=pltpu.CompilerParams(
            dimension_semantics=("arbitrary", "arbitrary"), vmem_limit_bytes=VMEM_LIMIT),
        name="conv",
    )(u, hist, zc, cw, cb, lg, lb)


def _outproj_kernel(c_ref, a_ref, h_ref, wc_ref, wa_ref, fg_ref, o_ref, *, final):
    y = (jnp.dot(c_ref[...], wc_ref[...], preferred_element_type=F32)
         + jnp.dot(a_ref[...], wa_ref[...], preferred_element_type=F32))
    y = h_ref[...] + y
    if final:
        ms = jnp.mean(y * y, axis=-1, keepdims=True)
        y = y * lax.rsqrt(ms + NORM_EPS) * fg_ref[...]
    o_ref[...] = y


def _outproj(c, a, h, wo_all, layer, fg, tm, final):
    m, d = h.shape
    assert m % tm == 0 and D_CONV == D_ATT
    half = lambda: pl.BlockSpec((tm, D_CONV), lambda i: (i, 0))
    full = lambda: pl.BlockSpec((tm, d), lambda i: (i, 0))
    wspec = lambda part: pl.BlockSpec((None, D_CONV, d), lambda i: (layer, part, 0))
    return pl.pallas_call(
        functools.partial(_outproj_kernel, final=final),
        grid=(m // tm,),
        in_specs=[half(), half(), full(), wspec(0), wspec(1), pl.BlockSpec((1, d), lambda i: (0, 0))],
        out_specs=full(),
        out_shape=jax.ShapeDtypeStruct((m, d), F32),
        compiler_params=pltpu.CompilerParams(
            dimension_semantics=("arbitrary",), vmem_limit_bytes=VMEM_LIMIT),
        name="outproj",
    )(c, a, h, wo_all, wo_all, fg)


def _bias_kernel(bucket_ref, rb_ref, o_ref, *, n_near, tq):
    h = pl.program_id(0)
    far = rb_ref[FAR_BUCKET, h]
    for j in range(n_near):
        def body(r, carry):
            r0 = pl.multiple_of(r * SUBLANE, SUBLANE)
            bk = bucket_ref[j, pl.ds(r0, SUBLANE), :]
            acc = jnp.zeros(bk.shape, F32)
            for b in range(N_BUCKETS):
                acc = jnp.where(bk == b, rb_ref[b, h], acc)
            o_ref[j, pl.ds(r0, SUBLANE), :] = acc - far
            return carry
        lax.fori_loop(0, tq // SUBLANE, body, 0)


def _bias_tiles(bucket, rel_bias):
    n_near, tq, kb = bucket.shape
    return pl.pallas_call(
        functools.partial(_bias_kernel, n_near=n_near, tq=tq),
        grid=(N_HEADS,),
        in_specs=[pl.BlockSpec((n_near, tq, kb), lambda h: (0, 0, 0)),
                  pl.BlockSpec(memory_space=pltpu.SMEM)],
        out_specs=pl.BlockSpec((None, n_near, tq, kb), lambda h: (h, 0, 0, 0)),
        out_shape=jax.ShapeDtypeStruct((N_HEADS, n_near, tq, kb), F32),
        compiler_params=pltpu.CompilerParams(dimension_semantics=("arbitrary",)),
        name="bias_tiles",
    )(bucket, rel_bias)


def _t5_bucket(rel):
    half = N_BUCKETS // 2
    max_exact = half // 2
    n = jnp.abs(rel)
    large = max_exact + (jnp.log(jnp.maximum(n, 1).astype(jnp.float32) / max_exact)
                         / math.log(MAX_DISTANCE / max_exact) * (half - max_exact)).astype(jnp.int32)
    large = jnp.minimum(large, half - 1)
    return jnp.where(rel > 0, half, 0) + jnp.where(n < max_exact, n, large)


class _AttnCfg:
    def __init__(self, tq, n_rows, n_qt, s_pad, lane_off, n_pos, q_pos0, near_step, near_base, n_near, top_k,
                 edge_block, main_shift, n_main, dims_major=False):
        self.dims_major = dims_major
        self.edge_block, self.main_shift, self.n_main = edge_block, main_shift, n_main
        assert all(kb == edge_block or 0 <= kb - main_shift < n_main for kb in range(s_pad // KEY_BLOCK))
        self.tq, self.n_rows, self.n_qt, self.s_pad, self.lane_off = tq, n_rows, n_qt, s_pad, lane_off
        self.n_pos, self.q_pos0 = n_pos, q_pos0
        self.near_step, self.near_base, self.n_near, self.top_k = near_step, near_base, n_near, top_k
        self.rel0 = near_base * KEY_BLOCK - lane_off - q_pos0
        assert s_pad % KEY_BLOCK == 0 and tq % LANE == 0 and n_rows <= tq
        assert near_step * KEY_BLOCK == tq or n_qt == 1
        assert (near_base == 0 and near_step == 0) or 1 - self.rel0 >= MAX_DISTANCE
        assert (near_base + near_step * (n_qt - 1) + n_near) * KEY_BLOCK <= s_pad

    def bucket_table(self):
        j = jnp.arange(self.n_near, dtype=jnp.int32)[:, None, None]
        r = jnp.arange(self.tq, dtype=jnp.int32)[None, :, None]
        c = jnp.arange(KEY_BLOCK, dtype=jnp.int32)[None, None, :]
        rel = self.rel0 + KEY_BLOCK * j + c - r
        return _t5_bucket(lax.optimization_barrier(rel))


def _chunk_of(pos):
    return jnp.where(pos < N_META, 0, 1 + ((pos - N_META) >> CHUNK_SHIFT))


def _for_blocks(n, fn):
    def two(k2, carry):
        fn(2 * k2)
        fn(2 * k2 + 1)
        return carry

    if isinstance(n, int):
        lax.fori_loop(0, n // 2, two, 0)
        if n % 2:
            fn(n - 1)
    else:
        lax.fori_loop(0, n >> 1, two, 0)

        @pl.when((n & 1) == 1)
        def _():
            fn(n - 1)


def _attn_kernel(q_ref, qi_ref, w_ref, za_ref, k_ref, v_ref, kx_ref, ke_ref, ve_ref, kxe_ref,
                 bias_ref, tri_ref, o_ref,
                 qm_ref, qim_ref, wb_ref, keys_ref, madd_ref, s_ref, mx_ref, l_ref, acc_ref, ties_ref,
                 *, cfg):
    tq = cfg.tq
    i = pl.program_id(1)
    near0 = cfg.near_base + cfg.near_step * i if cfg.near_step else cfg.near_base
    nkb = near0 + cfg.n_near
    lane = lax.broadcasted_iota(jnp.int32, (tq, LANE), 1)
    row = lax.broadcasted_iota(jnp.int32, (tq, LANE), 0)
    low_half = lane < HEAD_DIM
    q_pos_base = cfg.q_pos0 + i * tq
    q_chunk = _chunk_of(q_pos_base + row)

    def admissible(blk):
        kpos = blk * LANE + lane - cfg.lane_off
        return (kpos >= 0) & (kpos < cfg.n_pos) & (_chunk_of(kpos) <= q_chunk)

    def key_operand(main_ref, edge_ref, fs, kb):
        def main(idx):
            ks = pl.ds(pl.multiple_of(idx * KEY_BLOCK, KEY_BLOCK), KEY_BLOCK)
            return main_ref[fs, ks] if cfg.dims_major else main_ref[ks, fs]

        edge = edge_ref[fs, :] if cfg.dims_major else edge_ref[:, fs]
        if isinstance(kb, int):
            return edge if kb == cfg.edge_block else main(kb - cfg.main_shift)
        if cfg.n_main == 0:
            return edge
        return jnp.where(kb == cfg.edge_block, edge, main(jnp.clip(kb - cfg.main_shift, 0, cfg.n_main - 1)))

    def dot_keys(lhs, rhs):
        if cfg.dims_major:
            return jnp.dot(lhs, rhs, preferred_element_type=F32)
        return lax.dot_general(lhs, rhs, NT_DIMS, preferred_element_type=F32)

    def dot_values(lhs, rhs):
        if cfg.dims_major:
            return lax.dot_general(lhs, rhs, NT_DIMS, preferred_element_type=F32)
        return jnp.dot(lhs, rhs, preferred_element_type=F32)

    all_feat = slice(0, LANE)
    pair_feat = lambda p: slice(p * LANE, (p + 1) * LANE)

    zero = jnp.zeros((tq, LANE), BF16)
    for p in range(N_PAIRS):
        sl = slice(p * LANE, (p + 1) * LANE)
        qp = q_ref[:, sl]
        qm_ref[p, 0:tq, :] = jnp.where(low_half, qp, zero)
        qm_ref[p, tq:2 * tq, :] = jnp.where(low_half, zero, qp)
        qip = qi_ref[:, sl]
        qim_ref[p, 0:tq, :] = jnp.where(low_half, qip, zero)
        qim_ref[p, tq:2 * tq, :] = jnp.where(low_half, zero, qip)
    w_scale = (IDX_HEADS ** -0.5) * (IDX_DIM ** -0.5)
    for h in range(IDX_HEADS):
        col = IDX_DIM + h
        wb_ref[h] = jnp.broadcast_to(w_ref[:, col:col + 1] * w_scale, (tq, LANE))

    def index_block(kb):
        kx = key_operand(kx_ref, kxe_ref, all_feat, kb)
        acc = [jnp.zeros((tq, LANE), F32) for _ in range(HALF)]
        for p in range(N_PAIRS):
            d = jnp.maximum(dot_keys(qim_ref[p], kx), 0.0)
            for hf in range(HALF):
                cs = slice(hf * LANE, (hf + 1) * LANE)
                acc[hf] = acc[hf] + wb_ref[2 * p] * d[0:tq, cs] + wb_ref[2 * p + 1] * d[tq:2 * tq, cs]
        for hf in range(HALF):
            blk = kb * HALF + hf
            sc = jnp.where(admissible(blk), acc[hf] + 0.0, NEG)
            bits = pltpu.bitcast(sc.T, jnp.int32)
            keys_ref[pl.ds(pl.multiple_of(blk * LANE, LANE), LANE), :] = (
                jnp.where(bits < 0, bits ^ jnp.int32(0x7FFFFFFF), bits))

    _for_blocks(nkb, index_block)

    kf = jnp.float32(cfg.top_k)
    n_part = 4

    def count(pred):
        def body(c_i, parts):
            chunk = keys_ref[pl.ds(pl.multiple_of(c_i * LANE, LANE), LANE), :]
            parts = list(parts)
            for j in range(LANE // SUBLANE):
                kk = chunk[j * SUBLANE:(j + 1) * SUBLANE, :]
                parts[j % n_part] = parts[j % n_part] + jnp.where(pred(kk), 1.0, 0.0)
            return tuple(parts)
        parts = lax.fori_loop(0, nkb * HALF, body, tuple(jnp.zeros((SUBLANE, tq), F32) for _ in range(n_part)))
        tot = (parts[0] + parts[1]) + (parts[2] + parts[3])
        return jnp.broadcast_to(jnp.sum(tot, axis=0, keepdims=True), (SUBLANE, tq))

    def bisect(it, lo):
        cand = lo + lax.shift_left(jnp.int32(1), 31 - it)
        return jnp.where(count(lambda kk: kk >= cand) >= kf, cand, lo)

    thr8 = lax.fori_loop(0, 32, bisect, jnp.full((SUBLANE, tq), INT_MIN, jnp.int32))
    need8 = kf - count(lambda kk: kk > thr8)
    thr = thr8[0:1, :]
    need = need8[0:1, :]

    key_row = lax.broadcasted_iota(jnp.int32, (KEY_BLOCK, tq), 0)
    q_chunk_t = _chunk_of(q_pos_base + lax.broadcasted_iota(jnp.int32, (KEY_BLOCK, tq), 1))

    ties_ref[...] = jnp.zeros((SUBLANE, tq), F32)

    def mask_block(kb):
        r0 = pl.multiple_of(kb * KEY_BLOCK, KEY_BLOCK)
        kk = keys_ref[pl.ds(r0, KEY_BLOCK), :]
        eq = kk == thr
        eqf = jnp.where(eq, 1.0, 0.0)
        seen = ties_ref[...]
        rank = seen[0:1, :] + jnp.dot(tri_ref[...], eqf.astype(BF16), preferred_element_type=F32)
        sel = jnp.where(eq, jnp.where(rank <= need, 0.0, NEG), jnp.where(kk > thr, 0.0, NEG))
        kpos = r0 + key_row - cfg.lane_off
        ok = (kpos >= 0) & (kpos < cfg.n_pos) & (_chunk_of(kpos) <= q_chunk_t)
        madd_ref[kb] = jnp.where(ok, sel, NEG).T
        ties_ref[...] = seen + jnp.sum(eqf, axis=0, keepdims=True)

    _for_blocks(nkb, mask_block)

    def logits(g, p, kb, near_j):
        s = dot_keys(qm_ref[p], key_operand(k_ref, ke_ref, pair_feat(p), kb))
        ma = madd_ref[kb]
        if near_j is None:
            add = jnp.concatenate([ma, ma], axis=0)
        else:
            add = jnp.concatenate([ma + bias_ref[2 * p, near_j], ma + bias_ref[2 * p + 1, near_j]], axis=0)
        s = s + add
        s_ref[g, kb] = s
        mx_ref[g] = jnp.maximum(mx_ref[g], jnp.maximum(s[:, 0:LANE], s[:, LANE:KEY_BLOCK]))

    def weighted_values(g, p, kb):
        m = mx_ref[g]
        pe = jnp.exp(s_ref[g, kb] - jnp.concatenate([m] * HALF, axis=1))
        l_ref[g] = l_ref[g] + (pe[:, 0:LANE] + pe[:, LANE:KEY_BLOCK])
        acc_ref[g] = acc_ref[g] + dot_values(pe.astype(BF16), key_operand(v_ref, ve_ref, pair_feat(p), kb))

    for stage in range(N_PAIRS + 1):
        p_a = stage if stage < N_PAIRS else None
        p_b = stage - 1 if stage > 0 else None

        def both(kb, near_j, p_a=p_a, p_b=p_b):
            if p_a is not None:
                logits(p_a % 2, p_a, kb, near_j)
            if p_b is not None:
                weighted_values(p_b % 2, p_b, kb)

        if p_a is not None:
            mx_ref[p_a % 2] = jnp.full((2 * tq, LANE), M_INIT, F32)

        _for_blocks(near0, lambda kb, both=both: both(kb, None))
        for j in range(cfg.n_near):
            both(near0 + j, j)

        if p_a is not None:
            g = p_a % 2
            mx_ref[g] = jnp.broadcast_to(jnp.max(mx_ref[g], axis=1, keepdims=True), (2 * tq, LANE))
            l_ref[g] = jnp.zeros((2 * tq, LANE), F32)
            acc_ref[g] = jnp.zeros((2 * tq, LANE), F32)
        if p_b is not None:
            g = p_b % 2
            o = acc_ref[g] / jnp.sum(l_ref[g], axis=1, keepdims=True)
            o = jnp.where(low_half, o[0:tq], o[tq:2 * tq])
            z = za_ref[:, p_b * LANE:(p_b + 1) * LANE].astype(F32)
            o_ref[:, p_b * LANE:(p_b + 1) * LANE] = (o * _silu(z)).astype(BF16)


def _attention(cfg, q, qi, w, za, mains, edges, bias, tri, main_lead=()):
    nb, t, _ = q.shape
    tq, s_pad = cfg.tq, cfg.s_pad
    assert t == tq * cfg.n_qt
    qspec = lambda width: pl.BlockSpec((None, tq, width), lambda b, i: (b, i, 0))

    def kspec(a, lead):
        per_batch = a.shape[len(lead)] == nb
        assert per_batch or a.shape[len(lead)] == 1
        return pl.BlockSpec((None,) * (len(lead) + 1) + a.shape[-2:],
                            lambda b, i: lead + ((b if per_batch else 0), 0, 0),
                            pipeline_mode=pl.Buffered(1))

    key_axis = -1 if cfg.dims_major else -2
    for a in mains:
        assert a.shape[key_axis] >= cfg.n_main * KEY_BLOCK
    for a in edges:
        assert a.shape[key_axis] == KEY_BLOCK
    nkb_max = s_pad // KEY_BLOCK
    return pl.pallas_call(
        functools.partial(_attn_kernel, cfg=cfg),
        grid=(nb, cfg.n_qt),
        in_specs=[
            qspec(D_ATT), qspec(D_ATT), qspec(LANE), qspec(D_ATT),
            *[kspec(a, tuple(main_lead)) for a in mains],
            *[kspec(a, ()) for a in edges],
            pl.BlockSpec((N_HEADS, cfg.n_near, tq, KEY_BLOCK), lambda b, i: (0, 0, 0, 0),
                         pipeline_mode=pl.Buffered(1)),
            pl.BlockSpec((KEY_BLOCK, KEY_BLOCK), lambda b, i: (0, 0)),
        ],
        out_specs=qspec(D_ATT),
        out_shape=jax.ShapeDtypeStruct((nb, t, D_ATT), BF16),
        scratch_shapes=[
            pltpu.VMEM((N_PAIRS, 2 * tq, LANE), BF16),
            pltpu.VMEM((N_PAIRS, 2 * tq, LANE), BF16),
            pltpu.VMEM((IDX_HEADS, tq, LANE), F32),
            pltpu.VMEM((s_pad, tq), jnp.int32),
            pltpu.VMEM((nkb_max, tq, KEY_BLOCK), F32),
            pltpu.VMEM((PAIR_BUFFERS, nkb_max, 2 * tq, KEY_BLOCK), F32),
            pltpu.VMEM((PAIR_BUFFERS, 2 * tq, LANE), F32),
            pltpu.VMEM((PAIR_BUFFERS, 2 * tq, LANE), F32),
            pltpu.VMEM((PAIR_BUFFERS, 2 * tq, LANE), F32),
            pltpu.VMEM((SUBLANE, tq), F32),
        ],
        compiler_params=pltpu.CompilerParams(
            dimension_semantics=("arbitrary", "arbitrary"), vmem_limit_bytes=VMEM_LIMIT),
        name="attention",
    )(q, qi, w, za, *mains, *edges, bias, tri)


def _edge_block(cfg, parts, lead_zeros):
    key_axis = 2 if cfg.dims_major else 1
    parts = [a.astype(BF16) for a in parts]

    def zeros(n):
        shape = list(parts[0].shape)
        shape[key_axis] = n
        return [jnp.zeros(shape, BF16)] if n else []

    n_end = KEY_BLOCK - lead_zeros - sum(a.shape[key_axis] for a in parts)
    return jnp.concatenate(zeros(lead_zeros) + parts + zeros(n_end), axis=key_axis)


def _attend(cfg, q, qi, tail, za, mains, edges, bias, tri, main_lead=()):
    rows = lambda a: jnp.pad(a, ((0, 0), (0, cfg.tq * cfg.n_qt - a.shape[1]), (0, 0)))
    out = _attention(cfg, rows(q), rows(qi), rows(tail), rows(za), mains, edges, bias, tri, main_lead)
    return out[:, :q.shape[1]]


def _forward(x_prompt, x_sample, cache_k, cache_v, cache_kidx, state_conv, meta_tokens,
             norm_g, w_in, conv_w, conv_b, conv_ln_g, conv_ln_b, w_out, rel_bias, final_g,
             *, tq_frames=256, tm_frames=512):
    depth = w_in.shape[0]
    bp, seq, d = x_prompt.shape
    bs, dec, _ = x_sample.shape
    past = cache_k.shape[2] - N_META
    assert seq % tq_frames == 0 and seq % CHUNK == 0 and (bp * seq) % tm_frames == 0

    n_qt = seq // tq_frames
    cfg_f = _AttnCfg(tq=tq_frames, n_rows=tq_frames, n_qt=n_qt, s_pad=KEY_BLOCK + seq,
                     lane_off=KEY_BLOCK - N_META, n_pos=N_META + seq, q_pos0=N_META,
                     near_step=tq_frames // KEY_BLOCK, near_base=0,
                     n_near=tq_frames // KEY_BLOCK + 1, top_k=min(TOPK_MAX, seq // 4),
                     edge_block=0, main_shift=1, n_main=seq // KEY_BLOCK)
    n_cache = N_META + past
    n_pos_s = n_cache + dec
    s_pad_s = -(-n_pos_s // KEY_BLOCK) * KEY_BLOCK
    last_s = s_pad_s // KEY_BLOCK - 1
    near_s = max(0, (n_cache - MAX_DISTANCE) // KEY_BLOCK)
    cfg_s = _AttnCfg(tq=LANE, n_rows=dec, n_qt=1, s_pad=s_pad_s, lane_off=0,
                     n_pos=n_pos_s, q_pos0=n_cache, near_step=0, near_base=near_s,
                     n_near=last_s - near_s + 1, top_k=min(TOPK_MAX, (past + dec) // 4),
                     edge_block=last_s, main_shift=0, n_main=last_s, dims_major=True)
    cfg_m = _AttnCfg(tq=LANE, n_rows=N_META, n_qt=1, s_pad=KEY_BLOCK, lane_off=KEY_BLOCK - N_META,
                     n_pos=N_META, q_pos0=0, near_step=0, near_base=0, n_near=1,
                     top_k=min(TOPK_MAX, seq // 4), edge_block=0, main_shift=1, n_main=0)
    assert dec <= LANE and n_cache >= last_s * KEY_BLOCK

    bias_f = _bias_tiles(cfg_f.bucket_table(), rel_bias)
    bias_s = _bias_tiles(cfg_s.bucket_table(), rel_bias)
    bias_m = _bias_tiles(cfg_m.bucket_table(), rel_bias)
    tri = (jnp.arange(KEY_BLOCK)[None, :] <= jnp.arange(KEY_BLOCK)[:, None]).astype(BF16)

    n_aux = bs * dec + N_META
    hf = x_prompt.reshape(bp * seq, d)
    haux = jnp.concatenate([x_sample.reshape(bs * dec, d), meta_tokens.astype(x_sample.dtype)], axis=0)

    w_all = w_in.astype(BF16)
    wo_all = w_out.astype(BF16)
    feat_major = lambda c: jnp.transpose(c, (0, 1, 3, 4, 2)).reshape(depth, bs, D_ATT, n_cache).astype(BF16)
    ck_t, cv_t = feat_major(cache_k), feat_major(cache_v)
    cx_t = jnp.swapaxes(cache_kidx, 2, 3).astype(BF16)
    cx_t = jnp.concatenate([cx_t, cx_t], axis=2)
    outs = {name: [] for name in ("kp", "vp", "kip", "cp", "ks", "vs", "kis", "cs")}
    for l in range(depth):
        w_tail = jnp.pad(w_all[l, :, N_GROUPS * GROUP:], ((0, 0), (0, LANE - IDX_DIM - IDX_HEADS)))
        g = norm_g[l][None, :]
        cw = jnp.repeat(conv_w[l], SUBLANE, axis=0)
        cb, lg, lb = conv_b[l][None, :], conv_ln_g[l][None, :], conv_ln_b[l][None, :]
        final = l == depth - 1
        fg = final_g[None, :]

        uA, zcA, qA, kA, vA, zaA, qiA, tailA, kbA, vbA = _inproj(haux, g, w_all, l, w_tail, n_aux)
        uF, zcF, qF, kF, vF, zaF, qiF, tailF, kbF, vbF = _inproj(hf, g, w_all, l, w_tail, tm_frames)
        ns = bs * dec
        split = lambda a: (a[:ns].reshape(bs, dec, -1), a[ns:][None])
        uS, uM = split(uA); zcS, zcM = split(zcA); qS, qM = split(qA); kS, kM = split(kA)
        vS, vM = split(vA); zaS, zaM = split(zaA); qiS, qiM = split(qiA); tailS, tailM = split(tailA)
        kbS, kbM = split(kbA); vbS, vbM = split(vbA)
        b3 = lambda a: a.reshape(bp, seq, -1)
        uF, zcF, qF, kF, vF, zaF, qiF, tailF, kbF, vbF = map(
            b3, (uF, zcF, qF, kF, vF, zaF, qiF, tailF, kbF, vbF))

        zero_hist = jnp.zeros((1, HIST_ROWS, D_CONV), F32)
        cM = _conv(uM, zero_hist, zcM, cw, cb, lg, lb)
        hist_f = jnp.concatenate([jnp.zeros((1, HIST_ROWS - N_META, D_CONV), F32), uM], axis=1)
        cF = _conv(uF, hist_f, zcF, cw, cb, lg, lb)
        hist_s = jnp.pad(state_conv[l].astype(F32), ((0, 0), (HIST_PAD, 0), (0, 0)))
        cS = _conv(uS, hist_s, zcS, cw, cb, lg, lb)

        kidx = lambda tail: tail[..., :IDX_DIM]
        twice = lambda a: jnp.concatenate([a, a], axis=-1).astype(BF16)
        meta_edges = tuple(_edge_block(cfg_f, [a], cfg_f.lane_off) for a in (kbM, vbM, twice(kidx(tailM))))
        aM = _attend(cfg_m, qM, qiM, tailM, zaM, meta_edges, meta_edges, bias_m, tri)
        aF = _attend(cfg_f, qF, qiF, tailF, zaF, (kbF, vbF, twice(kidx(tailF))), meta_edges, bias_f, tri)
        tr = lambda a: jnp.swapaxes(a, 1, 2)
        lo = cfg_s.edge_block * KEY_BLOCK
        sample_edges = tuple(_edge_block(cfg_s, [c[l, :, :, lo:], tr(new)], 0)
                             for c, new in ((ck_t, kbS), (cv_t, vbS), (cx_t, twice(kidx(tailS)))))
        aS = _attend(cfg_s, qS, qiS, tailS, zaS, (ck_t, cv_t, cx_t), sample_edges, bias_s, tri, main_lead=(l,))

        cAux = jnp.concatenate([cS.reshape(ns, D_CONV), cM[0]], axis=0)
        aAux = jnp.concatenate([aS.reshape(ns, D_ATT), aM[0]], axis=0)
        haux = _outproj(cAux, aAux, haux, wo_all, l, fg, n_aux, final)
        hf = _outproj(cF.reshape(bp * seq, D_CONV), aF.reshape(bp * seq, D_ATT), hf, wo_all, l, fg,
                      tm_frames, final)

        with_meta = lambda m, f: jnp.concatenate([jnp.broadcast_to(m, (bp,) + m.shape[1:]), f], axis=1)
        outs["kp"].append(with_meta(kM, kF).reshape(bp, N_META + seq, N_HEADS, HEAD_DIM))
        outs["vp"].append(with_meta(vM, vF).reshape(bp, N_META + seq, N_HEADS, HEAD_DIM))
        outs["kip"].append(with_meta(kidx(tailM), kidx(tailF)))
        u_ext_p = jnp.concatenate([jnp.zeros((bp, CONV_WIDTH - 1, D_CONV), F32),
                                   jnp.broadcast_to(uM, (bp, N_META, D_CONV)), uF], axis=1)
        outs["cp"].append(u_ext_p[:, -(CONV_WIDTH - 1):])
        outs["ks"].append(kS.reshape(bs, dec, N_HEADS, HEAD_DIM))
        outs["vs"].append(vS.reshape(bs, dec, N_HEADS, HEAD_DIM))
        outs["kis"].append(kidx(tailS))
        u_ext_s = jnp.concatenate([state_conv[l].astype(F32), uS], axis=1)
        outs["cs"].append(u_ext_s[:, -(CONV_WIDTH - 1):])

    y_prompt = hf.reshape(bp, seq, d)
    y_sample = haux[:bs * dec].reshape(bs, dec, d)
    st = lambda name: jnp.stack(outs[name])
    return (y_prompt, y_sample, st("kp"), st("vp"), st("kip"), st("cp"),
            st("ks"), st("vs"), st("kis"), st("cs"))


def kernel(x_prompt, x_sample, cache_k, cache_v, cache_kidx, state_conv, meta_tokens, norm_g, w_in,
           conv_w, conv_b, conv_ln_g, conv_ln_b, w_out, rel_bias, final_g):
    return _forward(x_prompt, x_sample, cache_k, cache_v, cache_kidx, state_conv, meta_tokens,
                    norm_g, w_in, conv_w, conv_b, conv_ln_g, conv_ln_b, w_out, rel_bias, final_g)
```

```python
import functools
import math

import jax
import jax.numpy as jnp
from jax import lax
from jax.experimental import pallas as pl
from jax.experimental.pallas import tpu as pltpu

N_META = 16
CHUNK = 64
CHUNK_SHIFT = 6
D_CONV = 1024
CONV_WIDTH = 31
N_HEADS = 16
HEAD_DIM = 64
D_ATT = N_HEADS * HEAD_DIM
IDX_HEADS = 16
IDX_DIM = 64
TOPK_MAX = 256
N_BUCKETS = 32
MAX_DISTANCE = 128
NORM_EPS = 1e-6
NEG = -1e30

LANE = 128
SUBLANE = 8
KEY_BLOCK = 256
HALF = KEY_BLOCK // LANE
N_PAIRS = N_HEADS // 2
PAIR_BUFFERS = 2
HIST_ROWS = 32
HIST_PAD = HIST_ROWS - (CONV_WIDTH - 1)
GROUP = 1024
N_GROUPS = 8
VMEM_LIMIT = 56 * 1024 * 1024
INT_MIN = -2147483648
HALF_BITS = 16
LOW_MASK = (1 << HALF_BITS) - 1
HALF_BIAS = 1 << (HALF_BITS - 1)
M_INIT = -1e29
FAR_BUCKET = N_BUCKETS // 2 - 1

F32 = jnp.float32
BF16 = jnp.bfloat16
NT_DIMS = (((1,), (1,)), ((), ()))


def _silu(x):
    return x * jax.nn.sigmoid(x)


def _inproj_kernel(x_ref, g_ref, wm_ref, wt_ref,
                   u_ref, zc_ref, q_ref, k_ref, v_ref, za_ref, qi_ref, tail_ref, kb_ref, vb_ref,
                   xn_ref, a_ref):
    n = pl.program_id(1)
    j = pl.program_id(2)

    @pl.when(n == 0)
    def _():
        x = x_ref[...]
        ms = jnp.mean(x * x, axis=-1, keepdims=True)
        xn_ref[j] = (x * lax.rsqrt(ms + NORM_EPS) * g_ref[...]).astype(BF16)

    def group():
        return jnp.dot(xn_ref[j], wm_ref[...], preferred_element_type=F32)

    @pl.when(n == 0)
    def _():
        a_ref[j] = group()

    @pl.when(n == 1)
    def _():
        u_ref[...] = a_ref[j] * jax.nn.sigmoid(group())

    @pl.when(n == 2)
    def _():
        zc_ref[...] = group().astype(BF16)

    @pl.when(n == 3)
    def _():
        q_ref[...] = (group() * (HEAD_DIM ** -0.5)).astype(BF16)

    @pl.when(n == 4)
    def _():
        y = group()
        k_ref[...] = y
        kb_ref[...] = y.astype(BF16)

    @pl.when(n == 5)
    def _():
        y = group()
        v_ref[...] = y
        vb_ref[...] = y.astype(BF16)

    @pl.when(n == 6)
    def _():
        za_ref[...] = group().astype(BF16)

    @pl.when(n == 7)
    def _():
        qi_ref[...] = group().astype(BF16)

    @pl.when(n == N_GROUPS)
    def _():
        tail_ref[...] = jnp.dot(xn_ref[j], wt_ref[...], preferred_element_type=F32)


def _inproj(x2d, g, w_all, layer, w_tail, tm):
    m, d = x2d.shape
    assert m % tm == 0 and w_all.shape[2] >= N_GROUPS * GROUP
    n_tiles = m // tm
    share = 2 if n_tiles % 2 == 0 else 1
    last = n_tiles - 1
    row = lambda width, dtype: jax.ShapeDtypeStruct((m, width), dtype)
    out_shape = (row(GROUP, F32), row(GROUP, BF16), row(GROUP, BF16), row(GROUP, F32),
                 row(GROUP, F32), row(GROUP, BF16), row(GROUP, BF16), row(LANE, F32),
                 row(GROUP, BF16), row(GROUP, BF16))

    def tile_at(n_w):
        def index(i, n, j):
            first = i * share
            return jnp.where(n < n_w, jnp.maximum(first - 1, 0),
                             jnp.where(n == n_w, first + j, jnp.minimum(first + share - 1, last)))
        return index

    ospec = lambda width, n_w: pl.BlockSpec((tm, width), lambda i, n, j: (tile_at(n_w)(i, n, j), 0))
    return pl.pallas_call(
        _inproj_kernel,
        grid=(n_tiles // share, N_GROUPS + 1, share),
        in_specs=[
            pl.BlockSpec((tm, d), lambda i, n, j: (tile_at(0)(i, n, j), 0)),
            pl.BlockSpec((1, d), lambda i, n, j: (0, 0)),
            pl.BlockSpec((None, d, GROUP), lambda i, n, j: (layer, 0, jnp.minimum(n, N_GROUPS - 1))),
            pl.BlockSpec((d, LANE), lambda i, n, j: (0, 0)),
        ],
        out_specs=([ospec(GROUP, n_w) for n_w in range(1, N_GROUPS)] + [ospec(LANE, N_GROUPS)]
                   + [ospec(GROUP, 4), ospec(GROUP, 5)]),
        out_shape=out_shape,
        scratch_shapes=[pltpu.VMEM((share, tm, d), BF16), pltpu.VMEM((share, tm, GROUP), F32)],
        compiler_params=pltpu.CompilerParams(
            dimension_semantics=("arbitrary", "arbitrary", "arbitrary"), vmem_limit_bytes=VMEM_LIMIT),
        name="inproj",
    )(x2d, g, w_all, w_tail)


def _conv_kernel(u_ref, hist_ref, zc_ref, cw_ref, cb_ref, lg_ref, lb_ref, o_ref, ext_ref, c_ref,
                 *, tt, rc):
    t = pl.program_id(1)
    n_hist = CONV_WIDTH - 1

    for b in range(SUBLANE):
        keep = n_hist - b

        @pl.when(t == 0)
        def _(b=b, keep=keep):
            ext_ref[b, 0:keep, :] = hist_ref[HIST_PAD + b:HIST_ROWS, :]

        @pl.when(t > 0)
        def _(b=b, keep=keep):
            ext_ref[b, 0:keep, :] = ext_ref[b, tt:tt + keep, :]

    for b in range(SUBLANE):
        ext_ref[b, n_hist - b:n_hist - b + tt, :] = u_ref[...]

    half_c = D_CONV // 2

    def taps(r, carry):
        r0 = pl.multiple_of(r * (2 * SUBLANE), 2 * SUBLANE)
        for c0 in (0, half_c):
            cs = slice(c0, c0 + half_c)
            acc = [jnp.broadcast_to(cb_ref[:, cs], (SUBLANE, half_c)) for _ in range(2)]
            for b in range(SUBLANE):
                n_a = (CONV_WIDTH - 1 - b) // SUBLANE + 1
                win = ext_ref[b, pl.ds(r0, (n_a + 1) * SUBLANE), cs]
                for a in range(n_a):
                    j = a * SUBLANE + b
                    w8 = cw_ref[j * SUBLANE:(j + 1) * SUBLANE, cs]
                    for g in range(2):
                        lo = (a + g) * SUBLANE
                        acc[g] = acc[g] + w8 * win[lo:lo + SUBLANE, :]
            for g in range(2):
                c_ref[pl.ds(r0 + g * SUBLANE, SUBLANE), cs] = acc[g]
        return carry

    lax.fori_loop(0, tt // (2 * SUBLANE), taps, 0)

    def norm_gate(r, carry):
        r0 = pl.multiple_of(r * rc, rc)
        c = c_ref[pl.ds(r0, rc), :]
        mu = jnp.mean(c, axis=-1, keepdims=True)
        xc = c - mu
        var = jnp.mean(xc * xc, axis=-1, keepdims=True)
        y = xc * lax.rsqrt(var + NORM_EPS) * lg_ref[...] + lb_ref[...]
        z = zc_ref[pl.ds(r0, rc), :].astype(F32)
        o_ref[pl.ds(r0, rc), :] = (_silu(y) * _silu(z)).astype(BF16)
        return carry

    lax.fori_loop(0, tt // rc, norm_gate, 0)


def _conv(u, hist, zc, cw, cb, lg, lb):
    nb, t, c = u.shape
    tt = min(t, 256)
    rc = min(tt, 64)
    assert t % tt == 0 and tt % rc == 0 and tt % (2 * SUBLANE) == 0 and c == D_CONV
    hist_map = (lambda b, i: (b, 0, 0)) if hist.shape[0] == nb else (lambda b, i: (0, 0, 0))
    vec = pl.BlockSpec((1, c), lambda b, i: (0, 0))
    return pl.pallas_call(
        functools.partial(_conv_kernel, tt=tt, rc=rc),
        grid=(nb, t // tt),
        in_specs=[
            pl.BlockSpec((None, tt, c), lambda b, i: (b, i, 0)),
            pl.BlockSpec((None, HIST_ROWS, c), hist_map),
            pl.BlockSpec((None, tt, c), lambda b, i: (b, i, 0)),
            pl.BlockSpec((CONV_WIDTH * SUBLANE, c), lambda b, i: (0, 0)),
            vec, vec, vec,
        ],
        out_specs=pl.BlockSpec((None, tt, c), lambda b, i: (b, i, 0)),
        out_shape=jax.ShapeDtypeStruct((nb, t, c), BF16),
        scratch_shapes=[pltpu.VMEM((SUBLANE, HIST_ROWS + tt, c), F32), pltpu.VMEM((tt, c), F32)],
        compiler_params=pltpu.CompilerParams(
            dimension_semantics=("arbitrary", "arbitrary"), vmem_limit_bytes=VMEM_LIMIT),
        name="conv",
    )(u, hist, zc, cw, cb, lg, lb)


def _outproj_kernel(c_ref, a_ref, h_ref, wc_ref, wa_ref, fg_ref, o_ref, *, final):
    y = (jnp.dot(c_ref[...], wc_ref[...], preferred_element_type=F32)
         + jnp.dot(a_ref[...], wa_ref[...], preferred_element_type=F32))
    y = h_ref[...] + y
    if final:
        ms = jnp.mean(y * y, axis=-1, keepdims=True)
        y = y * lax.rsqrt(ms + NORM_EPS) * fg_ref[...]
    o_ref[...] = y


def _outproj(c, a, h, wo_all, layer, fg, tm, final):
    m, d = h.shape
    assert m % tm == 0 and D_CONV == D_ATT
    half = lambda: pl.BlockSpec((tm, D_CONV), lambda i: (i, 0))
    full = lambda: pl.BlockSpec((tm, d), lambda i: (i, 0))
    wspec = lambda part: pl.BlockSpec((None, D_CONV, d), lambda i: (layer, part, 0))
    return pl.pallas_call(
        functools.partial(_outproj_kernel, final=final),
        grid=(m // tm,),
        in_specs=[half(), half(), full(), wspec(0), wspec(1), pl.BlockSpec((1, d), lambda i: (0, 0))],
        out_specs=full(),
        out_shape=jax.ShapeDtypeStruct((m, d), F32),
        compiler_params=pltpu.CompilerParams(
            dimension_semantics=("arbitrary",), vmem_limit_bytes=VMEM_LIMIT),
        name="outproj",
    )(c, a, h, wo_all, wo_all, fg)


def _bias_kernel(bucket_ref, rb_ref, o_ref, *, n_near, tq):
    h = pl.program_id(0)
    far = rb_ref[FAR_BUCKET, h]
    for j in range(n_near):
        def body(r, carry):
            r0 = pl.multiple_of(r * SUBLANE, SUBLANE)
            bk = bucket_ref[j, pl.ds(r0, SUBLANE), :]
            acc = jnp.zeros(bk.shape, F32)
            for b in range(N_BUCKETS):
                acc = jnp.where(bk == b, rb_ref[b, h], acc)
            o_ref[j, pl.ds(r0, SUBLANE), :] = acc - far
            return carry
        lax.fori_loop(0, tq // SUBLANE, body, 0)


def _bias_tiles(bucket, rel_bias):
    n_near, tq, kb = bucket.shape
    return pl.pallas_call(
        functools.partial(_bias_kernel, n_near=n_near, tq=tq),
        grid=(N_HEADS,),
        in_specs=[pl.BlockSpec((n_near, tq, kb), lambda h: (0, 0, 0)),
                  pl.BlockSpec(memory_space=pltpu.SMEM)],
        out_specs=pl.BlockSpec((None, n_near, tq, kb), lambda h: (h, 0, 0, 0)),
        out_shape=jax.ShapeDtypeStruct((N_HEADS, n_near, tq, kb), F32),
        compiler_params=pltpu.CompilerParams(dimension_semantics=("arbitrary",)),
        name="bias_tiles",
    )(bucket, rel_bias)


def _t5_bucket(rel):
    half = N_BUCKETS // 2
    max_exact = half // 2
    n = jnp.abs(rel)
    large = max_exact + (jnp.log(jnp.maximum(n, 1).astype(jnp.float32) / max_exact)
                         / math.log(MAX_DISTANCE / max_exact) * (half - max_exact)).astype(jnp.int32)
    large = jnp.minimum(large, half - 1)
    return jnp.where(rel > 0, half, 0) + jnp.where(n < max_exact, n, large)


class _AttnCfg:
    def __init__(self, tq, n_rows, n_qt, s_pad, lane_off, n_pos, q_pos0, near_step, near_base, n_near, top_k,
                 edge_block, main_shift, n_main, dims_major=False):
        self.dims_major = dims_major
        self.edge_block, self.main_shift, self.n_main = edge_block, main_shift, n_main
        assert all(kb == edge_block or 0 <= kb - main_shift < n_main for kb in range(s_pad // KEY_BLOCK))
        self.tq, self.n_rows, self.n_qt, self.s_pad, self.lane_off = tq, n_rows, n_qt, s_pad, lane_off
        self.n_pos, self.q_pos0 = n_pos, q_pos0
        self.near_step, self.near_base, self.n_near, self.top_k = near_step, near_base, n_near, top_k
        self.rel0 = near_base * KEY_BLOCK - lane_off - q_pos0
        assert s_pad % KEY_BLOCK == 0 and tq % LANE == 0 and n_rows <= tq
        assert near_step * KEY_BLOCK == tq or n_qt == 1
        assert (near_base == 0 and near_step == 0) or 1 - self.rel0 >= MAX_DISTANCE
        assert (near_base + near_step * (n_qt - 1) + n_near) * KEY_BLOCK <= s_pad

    def bucket_table(self):
        j = jnp.arange(self.n_near, dtype=jnp.int32)[:, None, None]
        r = jnp.arange(self.tq, dtype=jnp.int32)[None, :, None]
        c = jnp.arange(KEY_BLOCK, dtype=jnp.int32)[None, None, :]
        rel = self.rel0 + KEY_BLOCK * j + c - r
        return _t5_bucket(lax.optimization_barrier(rel))


def _chunk_of(pos):
    return jnp.where(pos < N_META, 0, 1 + ((pos - N_META) >> CHUNK_SHIFT))


def _for_blocks(n, fn):
    def two(k2, carry):
        fn(2 * k2)
        fn(2 * k2 + 1)
        return carry

    if isinstance(n, int):
        lax.fori_loop(0, n // 2, two, 0)
        if n % 2:
            fn(n - 1)
    else:
        lax.fori_loop(0, n >> 1, two, 0)

        @pl.when((n & 1) == 1)
        def _():
            fn(n - 1)


def _attn_kernel(q_ref, qi_ref, w_ref, za_ref, k_ref, v_ref, kx_ref, ke_ref, ve_ref, kxe_ref,
                 bias_ref, tri_ref, o_ref,
                 qm_ref, qim_ref, wb_ref, keys_ref, madd_ref, s_ref, mx_ref, l_ref, acc_ref, ties_ref,
                 hi_ref, lo_ref, sel_ref, *, cfg):
    tq = cfg.tq
    i = pl.program_id(1)
    near0 = cfg.near_base + cfg.near_step * i if cfg.near_step else cfg.near_base
    nkb = near0 + cfg.n_near
    lane = lax.broadcasted_iota(jnp.int32, (tq, LANE), 1)
    row = lax.broadcasted_iota(jnp.int32, (tq, LANE), 0)
    low_half = lane < HEAD_DIM
    q_pos_base = cfg.q_pos0 + i * tq
    q_chunk = _chunk_of(q_pos_base + row)

    def admissible(blk):
        kpos = blk * LANE + lane - cfg.lane_off
        return (kpos >= 0) & (kpos < cfg.n_pos) & (_chunk_of(kpos) <= q_chunk)

    def key_operand(main_ref, edge_ref, fs, kb):
        def main(idx):
            ks = pl.ds(pl.multiple_of(idx * KEY_BLOCK, KEY_BLOCK), KEY_BLOCK)
            return main_ref[fs, ks] if cfg.dims_major else main_ref[ks, fs]

        edge = edge_ref[fs, :] if cfg.dims_major else edge_ref[:, fs]
        if isinstance(kb, int):
            return edge if kb == cfg.edge_block else main(kb - cfg.main_shift)
        if cfg.n_main == 0:
            return edge
        return jnp.where(kb == cfg.edge_block, edge, main(jnp.clip(kb - cfg.main_shift, 0, cfg.n_main - 1)))

    def dot_keys(lhs, rhs):
        if cfg.dims_major:
            return jnp.dot(lhs, rhs, preferred_element_type=F32)
        return lax.dot_general(lhs, rhs, NT_DIMS, preferred_element_type=F32)

    def dot_values(lhs, rhs):
        if cfg.dims_major:
            return lax.dot_general(lhs, rhs, NT_DIMS, preferred_element_type=F32)
        return jnp.dot(lhs, rhs, preferred_element_type=F32)

    all_feat = slice(0, LANE)
    pair_feat = lambda p: slice(p * LANE, (p + 1) * LANE)

    zero = jnp.zeros((tq, LANE), BF16)
    for p in range(N_PAIRS):
        sl = slice(p * LANE, (p + 1) * LANE)
        qp = q_ref[:, sl]
        qm_ref[p, 0:tq, :] = jnp.where(low_half, qp, zero)
        qm_ref[p, tq:2 * tq, :] = jnp.where(low_half, zero, qp)
        qip = qi_ref[:, sl]
        qim_ref[p, 0:tq, :] = jnp.where(low_half, qip, zero)
        qim_ref[p, tq:2 * tq, :] = jnp.where(low_half, zero, qip)
    w_scale = (IDX_HEADS ** -0.5) * (IDX_DIM ** -0.5)
    for h in range(IDX_HEADS):
        col = IDX_DIM + h
        wb_ref[h] = jnp.broadcast_to(w_ref[:, col:col + 1] * w_scale, (tq, LANE))

    def index_block(kb):
        kx = key_operand(kx_ref, kxe_ref, all_feat, kb)
        acc = [jnp.zeros((tq, LANE), F32) for _ in range(HALF)]
        for p in range(N_PAIRS):
            d = jnp.maximum(dot_keys(qim_ref[p], kx), 0.0)
            for hf in range(HALF):
                cs = slice(hf * LANE, (hf + 1) * LANE)
                acc[hf] = acc[hf] + wb_ref[2 * p] * d[0:tq, cs] + wb_ref[2 * p + 1] * d[tq:2 * tq, cs]
        for hf in range(HALF):
            blk = kb * HALF + hf
            sc = jnp.where(admissible(blk), acc[hf] + 0.0, NEG)
            bits = pltpu.bitcast(sc.T, jnp.int32)
            key = jnp.where(bits < 0, bits ^ jnp.int32(0x7FFFFFFF), bits)
            rows = pl.ds(pl.multiple_of(blk * LANE, LANE), LANE)
            keys_ref[rows, :] = key
            hi_ref[rows, :] = (key >> HALF_BITS).astype(jnp.int16)
            lo_ref[rows, :] = ((key & LOW_MASK) - HALF_BIAS).astype(jnp.int16)

    _for_blocks(nkb, index_block)

    kf = jnp.float32(cfg.top_k)
    n_part = 4
    packed_rows = 2 * SUBLANE

    def count16(ref, pred):
        def body(c_i, parts):
            chunk = ref[pl.ds(pl.multiple_of(c_i * LANE, LANE), LANE), :]
            parts = list(parts)
            for j in range(LANE // packed_rows):
                kk = chunk[j * packed_rows:(j + 1) * packed_rows, :]
                parts[j % n_part] = parts[j % n_part] + jnp.where(pred(kk), jnp.int16(1), jnp.int16(0))
            return tuple(parts)
        parts = lax.fori_loop(0, nkb * HALF, body,
                              tuple(jnp.zeros((packed_rows, tq), jnp.int16) for _ in range(n_part)))
        tot = ((parts[0] + parts[1]) + (parts[2] + parts[3])).astype(jnp.int32)
        return jnp.broadcast_to(jnp.sum(tot, axis=0, keepdims=True), (SUBLANE, tq)).astype(F32)

    def pack16(v):
        return jnp.concatenate([v, v], axis=0).astype(jnp.int16)

    def bisect16(ref, target):
        def step(it, lo):
            cand = lo + lax.shift_left(jnp.int32(1), HALF_BITS - 1 - it)
            cand16 = pack16(cand)
            return jnp.where(count16(ref, lambda kk: kk >= cand16) >= target, cand, lo)
        return lax.fori_loop(0, HALF_BITS, step, jnp.full((SUBLANE, tq), -HALF_BIAS, jnp.int32))

    hi_thr = bisect16(hi_ref, kf)
    hi_thr16 = pack16(hi_thr)
    above = count16(hi_ref, lambda kk: kk > hi_thr16)

    def low_keys(c_i, carry):
        rows = pl.ds(pl.multiple_of(c_i * LANE, LANE), LANE)
        hi, lo = hi_ref[rows, :], lo_ref[rows, :]
        for j in range(LANE // packed_rows):
            rs = slice(j * packed_rows, (j + 1) * packed_rows)
            sel_ref[pl.ds(pl.multiple_of(c_i * LANE, LANE) + j * packed_rows, packed_rows), :] = jnp.where(
                hi[rs, :] == hi_thr16, lo[rs, :], jnp.int16(-HALF_BIAS))
        return carry

    lax.fori_loop(0, nkb * HALF, low_keys, 0)
    lo_thr = bisect16(sel_ref, kf - above)
    thr8 = (hi_thr << HALF_BITS) | ((lo_thr + HALF_BIAS) & LOW_MASK)

    def count(pred):
        def body(c_i, parts):
            chunk = keys_ref[pl.ds(pl.multiple_of(c_i * LANE, LANE), LANE), :]
            parts = list(parts)
            for j in range(LANE // SUBLANE):
                kk = chunk[j * SUBLANE:(j + 1) * SUBLANE, :]
                parts[j % n_part] = parts[j % n_part] + jnp.where(pred(kk), 1.0, 0.0)
            return tuple(parts)
        parts = lax.fori_loop(0, nkb * HALF, body, tuple(jnp.zeros((SUBLANE, tq), F32) for _ in range(n_part)))
        tot = (parts[0] + parts[1]) + (parts[2] + parts[3])
        return jnp.broadcast_to(jnp.sum(tot, axis=0, keepdims=True), (SUBLANE, tq))

    need8 = kf - count(lambda kk: kk > thr8)
    thr = thr8[0:1, :]
    need = need8[0:1, :]

    key_row = lax.broadcasted_iota(jnp.int32, (KEY_BLOCK, tq), 0)
    q_chunk_t = _chunk_of(q_pos_base + lax.broadcasted_iota(jnp.int32, (KEY_BLOCK, tq), 1))

    ties_ref[...] = jnp.zeros((SUBLANE, tq), F32)

    def mask_block(kb):
        r0 = pl.multiple_of(kb * KEY_BLOCK, KEY_BLOCK)
        kk = keys_ref[pl.ds(r0, KEY_BLOCK), :]
        eq = kk == thr
        eqf = jnp.where(eq, 1.0, 0.0)
        seen = ties_ref[...]
        rank = seen[0:1, :] + jnp.dot(tri_ref[...], eqf.astype(BF16), preferred_element_type=F32)
        sel = jnp.where(eq, jnp.where(rank <= need, 0.0, NEG), jnp.where(kk > thr, 0.0, NEG))
        kpos = r0 + key_row - cfg.lane_off
        ok = (kpos >= 0) & (kpos < cfg.n_pos) & (_chunk_of(kpos) <= q_chunk_t)
        madd_ref[kb] = jnp.where(ok, sel, NEG).T
        ties_ref[...] = seen + jnp.sum(eqf, axis=0, keepdims=True)

    _for_blocks(nkb, mask_block)

    def logits(g, p, kb, near_j):
        s = dot_keys(qm_ref[p], key_operand(k_ref, ke_ref, pair_feat(p), kb))
        ma = madd_ref[kb]
        if near_j is None:
            add = jnp.concatenate([ma, ma], axis=0)
        else:
            add = jnp.concatenate([ma + bias_ref[2 * p, near_j], ma + bias_ref[2 * p + 1, near_j]], axis=0)
        s = s + add
        s_ref[g, kb] = s
        mx_ref[g] = jnp.maximum(mx_ref[g], jnp.maximum(s[:, 0:LANE], s[:, LANE:KEY_BLOCK]))

    def weighted_values(g, p, kb):
        m = mx_ref[g]
        pe = jnp.exp(s_ref[g, kb] - jnp.concatenate([m] * HALF, axis=1))
        l_ref[g] = l_ref[g] + (pe[:, 0:LANE] + pe[:, LANE:KEY_BLOCK])
        acc_ref[g] = acc_ref[g] + dot_values(pe.astype(BF16), key_operand(v_ref, ve_ref, pair_feat(p), kb))

    for stage in range(N_PAIRS + 1):
        p_a = stage if stage < N_PAIRS else None
        p_b = stage - 1 if stage > 0 else None

        def both(kb, near_j, p_a=p_a, p_b=p_b):
            if p_a is not None:
                logits(p_a % 2, p_a, kb, near_j)
            if p_b is not None:
                weighted_values(p_b % 2, p_b, kb)

        if p_a is not None:
            mx_ref[p_a % 2] = jnp.full((2 * tq, LANE), M_INIT, F32)

        _for_blocks(near0, lambda kb, both=both: both(kb, None))
        for j in range(cfg.n_near):
            both(near0 + j, j)

        if p_a is not None:
            g = p_a % 2
            mx_ref[g] = jnp.broadcast_to(jnp.max(mx_ref[g], axis=1, keepdims=True), (2 * tq, LANE))
            l_ref[g] = jnp.zeros((2 * tq, LANE), F32)
            acc_ref[g] = jnp.zeros((2 * tq, LANE), F32)
        if p_b is not None:
            g = p_b % 2
            o = acc_ref[g] / jnp.sum(l_ref[g], axis=1, keepdims=True)
            o = jnp.where(low_half, o[0:tq], o[tq:2 * tq])
            z = za_ref[:, p_b * LANE:(p_b + 1) * LANE].astype(F32)
            o_ref[:, p_b * LANE:(p_b + 1) * LANE] = (o * _silu(z)).astype(BF16)


def _attention(cfg, q, qi, w, za, mains, edges, bias, tri, main_lead=()):
    nb, t, _ = q.shape
    tq, s_pad = cfg.tq, cfg.s_pad
    assert t == tq * cfg.n_qt
    qspec = lambda width: pl.BlockSpec((None, tq, width), lambda b, i: (b, i, 0))

    def kspec(a, lead):
        per_batch = a.shape[len(lead)] == nb
        assert per_batch or a.shape[len(lead)] == 1
        return pl.BlockSpec((None,) * (len(lead) + 1) + a.shape[-2:],
                            lambda b, i: lead + ((b if per_batch else 0), 0, 0),
                            pipeline_mode=pl.Buffered(1))

    key_axis = -1 if cfg.dims_major else -2
    for a in mains:
        assert a.shape[key_axis] >= cfg.n_main * KEY_BLOCK
    for a in edges:
        assert a.shape[key_axis] == KEY_BLOCK
    nkb_max = s_pad // KEY_BLOCK
    return pl.pallas_call(
        functools.partial(_attn_kernel, cfg=cfg),
        grid=(nb, cfg.n_qt),
        in_specs=[
            qspec(D_ATT), qspec(D_ATT), qspec(LANE), qspec(D_ATT),
            *[kspec(a, tuple(main_lead)) for a in mains],
            *[kspec(a, ()) for a in edges],
            pl.BlockSpec((N_HEADS, cfg.n_near, tq, KEY_BLOCK), lambda b, i: (0, 0, 0, 0),
                         pipeline_mode=pl.Buffered(1)),
            pl.BlockSpec((KEY_BLOCK, KEY_BLOCK), lambda b, i: (0, 0)),
        ],
        out_specs=qspec(D_ATT),
        out_shape=jax.ShapeDtypeStruct((nb, t, D_ATT), BF16),
        scratch_shapes=[
            pltpu.VMEM((N_PAIRS, 2 * tq, LANE), BF16),
            pltpu.VMEM((N_PAIRS, 2 * tq, LANE), BF16),
            pltpu.VMEM((IDX_HEADS, tq, LANE), F32),
            pltpu.VMEM((s_pad, tq), jnp.int32),
            pltpu.VMEM((nkb_max, tq, KEY_BLOCK), F32),
            pltpu.VMEM((PAIR_BUFFERS, nkb_max, 2 * tq, KEY_BLOCK), F32),
            pltpu.VMEM((PAIR_BUFFERS, 2 * tq, LANE), F32),
            pltpu.VMEM((PAIR_BUFFERS, 2 * tq, LANE), F32),
            pltpu.VMEM((PAIR_BUFFERS, 2 * tq, LANE), F32),
            pltpu.VMEM((SUBLANE, tq), F32),
            pltpu.VMEM((s_pad, tq), jnp.int16),
            pltpu.VMEM((s_pad, tq), jnp.int16),
            pltpu.VMEM((s_pad, tq), jnp.int16),
        ],
        compiler_params=pltpu.CompilerParams(
            dimension_semantics=("arbitrary", "arbitrary"), vmem_limit_bytes=VMEM_LIMIT),
        name="attention",
    )(q, qi, w, za, *mains, *edges, bias, tri)


def _edge_block(cfg, parts, lead_zeros):
    key_axis = 2 if cfg.dims_major else 1
    parts = [a.astype(BF16) for a in parts]

    def zeros(n):
        shape = list(parts[0].shape)
        shape[key_axis] = n
        return [jnp.zeros(shape, BF16)] if n else []

    n_end = KEY_BLOCK - lead_zeros - sum(a.shape[key_axis] for a in parts)
    return jnp.concatenate(zeros(lead_zeros) + parts + zeros(n_end), axis=key_axis)


def _attend(cfg, q, qi, tail, za, mains, edges, bias, tri, main_lead=()):
    rows = lambda a: jnp.pad(a, ((0, 0), (0, cfg.tq * cfg.n_qt - a.shape[1]), (0, 0)))
    out = _attention(cfg, rows(q), rows(qi), rows(tail), rows(za), mains, edges, bias, tri, main_lead)
    return out[:, :q.shape[1]]


def _forward(x_prompt, x_sample, cache_k, cache_v, cache_kidx, state_conv, meta_tokens,
             norm_g, w_in, conv_w, conv_b, conv_ln_g, conv_ln_b, w_out, rel_bias, final_g,
             *, tq_frames=256, tm_frames=512):
    depth = w_in.shape[0]
    bp, seq, d = x_prompt.shape
    bs, dec, _ = x_sample.shape
    past = cache_k.shape[2] - N_META
    assert seq % tq_frames == 0 and seq % CHUNK == 0 and (bp * seq) % tm_frames == 0

    n_qt = seq // tq_frames
    cfg_f = _AttnCfg(tq=tq_frames, n_rows=tq_frames, n_qt=n_qt, s_pad=KEY_BLOCK + seq,
                     lane_off=KEY_BLOCK - N_META, n_pos=N_META + seq, q_pos0=N_META,
                     near_step=tq_frames // KEY_BLOCK, near_base=0,
                     n_near=tq_frames // KEY_BLOCK + 1, top_k=min(TOPK_MAX, seq // 4),
                     edge_block=0, main_shift=1, n_main=seq // KEY_BLOCK)
    n_cache = N_META + past
    n_pos_s = n_cache + dec
    s_pad_s = -(-n_pos_s // KEY_BLOCK) * KEY_BLOCK
    last_s = s_pad_s // KEY_BLOCK - 1
    near_s = max(0, (n_cache - MAX_DISTANCE) // KEY_BLOCK)
    cfg_s = _AttnCfg(tq=LANE, n_rows=dec, n_qt=1, s_pad=s_pad_s, lane_off=0,
                     n_pos=n_pos_s, q_pos0=n_cache, near_step=0, near_base=near_s,
                     n_near=last_s - near_s + 1, top_k=min(TOPK_MAX, (past + dec) // 4),
                     edge_block=last_s, main_shift=0, n_main=last_s, dims_major=True)
    cfg_m = _AttnCfg(tq=LANE, n_rows=N_META, n_qt=1, s_pad=KEY_BLOCK, lane_off=KEY_BLOCK - N_META,
                     n_pos=N_META, q_pos0=0, near_step=0, near_base=0, n_near=1,
                     top_k=min(TOPK_MAX, seq // 4), edge_block=0, main_shift=1, n_main=0)
    assert dec <= LANE and n_cache >= last_s * KEY_BLOCK

    bias_f = _bias_tiles(cfg_f.bucket_table(), rel_bias)
    bias_s = _bias_tiles(cfg_s.bucket_table(), rel_bias)
    bias_m = _bias_tiles(cfg_m.bucket_table(), rel_bias)
    tri = (jnp.arange(KEY_BLOCK)[None, :] <= jnp.arange(KEY_BLOCK)[:, None]).astype(BF16)

    n_aux = bs * dec + N_META
    hf = x_prompt.reshape(bp * seq, d)
    haux = jnp.concatenate([x_sample.reshape(bs * dec, d), meta_tokens.astype(x_sample.dtype)], axis=0)

    w_all = w_in.astype(BF16)
    wo_all = w_out.astype(BF16)
    feat_major = lambda c: jnp.transpose(c, (0, 1, 3, 4, 2)).reshape(depth, bs, D_ATT, n_cache).astype(BF16)
    ck_t, cv_t = feat_major(cache_k), feat_major(cache_v)
    cx_t = jnp.swapaxes(cache_kidx, 2, 3).astype(BF16)
    cx_t = jnp.concatenate([cx_t, cx_t], axis=2)
    outs = {name: [] for name in ("kp", "vp", "kip", "cp", "ks", "vs", "kis", "cs")}
    for l in range(depth):
        w_tail = jnp.pad(w_all[l, :, N_GROUPS * GROUP:], ((0, 0), (0, LANE - IDX_DIM - IDX_HEADS)))
        g = norm_g[l][None, :]
        cw = jnp.repeat(conv_w[l], SUBLANE, axis=0)
        cb, lg, lb = conv_b[l][None, :], conv_ln_g[l][None, :], conv_ln_b[l][None, :]
        final = l == depth - 1
        fg = final_g[None, :]

        uA, zcA, qA, kA, vA, zaA, qiA, tailA, kbA, vbA = _inproj(haux, g, w_all, l, w_tail, n_aux)
        uF, zcF, qF, kF, vF, zaF, qiF, tailF, kbF, vbF = _inproj(hf, g, w_all, l, w_tail, tm_frames)
        ns = bs * dec
        split = lambda a: (a[:ns].reshape(bs, dec, -1), a[ns:][None])
        uS, uM = split(uA); zcS, zcM = split(zcA); qS, qM = split(qA); kS, kM = split(kA)
        vS, vM = split(vA); zaS, zaM = split(zaA); qiS, qiM = split(qiA); tailS, tailM = split(tailA)
        kbS, kbM = split(kbA); vbS, vbM = split(vbA)
        b3 = lambda a: a.reshape(bp, seq, -1)
        uF, zcF, qF, kF, vF, zaF, qiF, tailF, kbF, vbF = map(
            b3, (uF, zcF, qF, kF, vF, zaF, qiF, tailF, kbF, vbF))

        zero_hist = jnp.zeros((1, HIST_ROWS, D_CONV), F32)
        cM = _conv(uM, zero_hist, zcM, cw, cb, lg, lb)
        hist_f = jnp.concatenate([jnp.zeros((1, HIST_ROWS - N_META, D_CONV), F32), uM], axis=1)
        cF = _conv(uF, hist_f, zcF, cw, cb, lg, lb)
        hist_s = jnp.pad(state_conv[l].astype(F32), ((0, 0), (HIST_PAD, 0), (0, 0)))
        cS = _conv(uS, hist_s, zcS, cw, cb, lg, lb)

        kidx = lambda tail: tail[..., :IDX_DIM]
        twice = lambda a: jnp.concatenate([a, a], axis=-1).astype(BF16)
        meta_edges = tuple(_edge_block(cfg_f, [a], cfg_f.lane_off) for a in (kbM, vbM, twice(kidx(tailM))))
        aM = _attend(cfg_m, qM, qiM, tailM, zaM, meta_edges, meta_edges, bias_m, tri)
        aF = _attend(cfg_f, qF, qiF, tailF, zaF, (kbF, vbF, twice(kidx(tailF))), meta_edges, bias_f, tri)
        tr = lambda a: jnp.swapaxes(a, 1, 2)
        lo = cfg_s.edge_block * KEY_BLOCK
        sample_edges = tuple(_edge_block(cfg_s, [c[l, :, :, lo:], tr(new)], 0)
                             for c, new in ((ck_t, kbS), (cv_t, vbS), (cx_t, twice(kidx(tailS)))))
        aS = _attend(cfg_s, qS, qiS, tailS, zaS, (ck_t, cv_t, cx_t), sample_edges, bias_s, tri, main_lead=(l,))

        cAux = jnp.concatenate([cS.reshape(ns, D_CONV), cM[0]], axis=0)
        aAux = jnp.concatenate([aS.reshape(ns, D_ATT), aM[0]], axis=0)
        haux = _outproj(cAux, aAux, haux, wo_all, l, fg, n_aux, final)
        hf = _outproj(cF.reshape(bp * seq, D_CONV), aF.reshape(bp * seq, D_ATT), hf, wo_all, l, fg,
                      tm_frames, final)

        with_meta = lambda m, f: jnp.concatenate([jnp.broadcast_to(m, (bp,) + m.shape[1:]), f], axis=1)
        outs["kp"].append(with_meta(kM, kF).reshape(bp, N_META + seq, N_HEADS, HEAD_DIM))
        outs["vp"].append(with_meta(vM, vF).reshape(bp, N_META + seq, N_HEADS, HEAD_DIM))
        outs["kip"].append(with_meta(kidx(tailM), kidx(tailF)))
        u_ext_p = jnp.concatenate([jnp.zeros((bp, CONV_WIDTH - 1, D_CONV), F32),
                                   jnp.broadcast_to(uM, (bp, N_META, D_CONV)), uF], axis=1)
        outs["cp"].append(u_ext_p[:, -(CONV_WIDTH - 1):])
        outs["ks"].append(kS.reshape(bs, dec, N_HEADS, HEAD_DIM))
        outs["vs"].append(vS.reshape(bs, dec, N_HEADS, HEAD_DIM))
        outs["kis"].append(kidx(tailS))
        u_ext_s = jnp.concatenate([state_conv[l].astype(F32), uS], axis=1)
        outs["cs"].append(u_ext_s[:, -(CONV_WIDTH - 1):])

    y_prompt = hf.reshape(bp, seq, d)
    y_sample = haux[:bs * dec].reshape(bs, dec, d)
    st = lambda name: jnp.stack(outs[name])
    return (y_prompt, y_sample, st("kp"), st("vp"), st("kip"), st("cp"),
            st("ks"), st("vs"), st("kis"), st("cs"))


def kernel(x_prompt, x_sample, cache_k, cache_v, cache_kidx, state_conv, meta_tokens, norm_g, w_in,
           conv_w, conv_b, conv_ln_g, conv_ln_b, w_out, rel_bias, final_g):
    return _forward(x_prompt, x_sample, cache_k, cache_v, cache_kidx, state_conv, meta_tokens,
                    norm_g, w_in, conv_w, conv_b, conv_ln_g, conv_ln_b, w_out, rel_bias, final_g)
```

```python
import functools
import math

import jax
import jax.numpy as jnp
from jax import lax
from jax.experimental import pallas as pl
from jax.experimental.pallas import tpu as pltpu

N_META = 16
CHUNK = 64
CHUNK_SHIFT = 6
D_CONV = 1024
CONV_WIDTH = 31
N_HEADS = 16
HEAD_DIM = 64
D_ATT = N_HEADS * HEAD_DIM
IDX_HEADS = 16
IDX_DIM = 64
TOPK_MAX = 256
N_BUCKETS = 32
MAX_DISTANCE = 128
NORM_EPS = 1e-6
NEG = -1e30

LANE = 128
SUBLANE = 8
KEY_BLOCK = 256
HALF = KEY_BLOCK // LANE
N_PAIRS = N_HEADS // 2
PAIR_BUFFERS = 2
HIST_ROWS = 32
HIST_PAD = HIST_ROWS - (CONV_WIDTH - 1)
GROUP = 1024
N_GROUPS = 8
VMEM_LIMIT = 56 * 1024 * 1024
INT_MIN = -2147483648
HALF_BITS = 16
LOW_MASK = (1 << HALF_BITS) - 1
HALF_BIAS = 1 << (HALF_BITS - 1)
M_INIT = -1e29
FAR_BUCKET = N_BUCKETS // 2 - 1

F32 = jnp.float32
BF16 = jnp.bfloat16
NT_DIMS = (((1,), (1,)), ((), ()))


def _silu(x):
    return x * jax.nn.sigmoid(x)


def _inproj_kernel(x_ref, g_ref, wm_ref, wt_ref,
                   u_ref, zc_ref, q_ref, k_ref, v_ref, za_ref, qi_ref, tail_ref, kb_ref, vb_ref,
                   xn_ref, a_ref):
    n = pl.program_id(1)
    j = pl.program_id(2)

    @pl.when(n == 0)
    def _():
        x = x_ref[...]
        ms = jnp.mean(x * x, axis=-1, keepdims=True)
        xn_ref[j] = (x * lax.rsqrt(ms + NORM_EPS) * g_ref[...]).astype(BF16)

    def group():
        return jnp.dot(xn_ref[j], wm_ref[...], preferred_element_type=F32)

    @pl.when(n == 0)
    def _():
        a_ref[j] = group()

    @pl.when(n == 1)
    def _():
        u_ref[...] = a_ref[j] * jax.nn.sigmoid(group())

    @pl.when(n == 2)
    def _():
        zc_ref[...] = group().astype(BF16)

    @pl.when(n == 3)
    def _():
        q_ref[...] = (group() * (HEAD_DIM ** -0.5)).astype(BF16)

    @pl.when(n == 4)
    def _():
        y = group()
        k_ref[...] = y
        kb_ref[...] = y.astype(BF16)

    @pl.when(n == 5)
    def _():
        y = group()
        v_ref[...] = y
        vb_ref[...] = y.astype(BF16)

    @pl.when(n == 6)
    def _():
        za_ref[...] = group().astype(BF16)

    @pl.when(n == 7)
    def _():
        qi_ref[...] = group().astype(BF16)

    @pl.when(n == N_GROUPS)
    def _():
        tail_ref[...] = jnp.dot(xn_ref[j], wt_ref[...], preferred_element_type=F32)


def _inproj(x2d, g, w_all, layer, w_tail, tm):
    m, d = x2d.shape
    assert m % tm == 0 and w_all.shape[2] >= N_GROUPS * GROUP
    n_tiles = m // tm
    share = 2 if n_tiles % 2 == 0 else 1
    last = n_tiles - 1
    row = lambda width, dtype: jax.ShapeDtypeStruct((m, width), dtype)
    out_shape = (row(GROUP, F32), row(GROUP, BF16), row(GROUP, BF16), row(GROUP, F32),
                 row(GROUP, F32), row(GROUP, BF16), row(GROUP, BF16), row(LANE, F32),
                 row(GROUP, BF16), row(GROUP, BF16))

    def tile_at(n_w):
        def index(i, n, j):
            first = i * share
            return jnp.where(n < n_w, jnp.maximum(first - 1, 0),
                             jnp.where(n == n_w, first + j, jnp.minimum(first + share - 1, last)))
        return index

    ospec = lambda width, n_w: pl.BlockSpec((tm, width), lambda i, n, j: (tile_at(n_w)(i, n, j), 0))
    return pl.pallas_call(
        _inproj_kernel,
        grid=(n_tiles // share, N_GROUPS + 1, share),
        in_specs=[
            pl.BlockSpec((tm, d), lambda i, n, j: (tile_at(0)(i, n, j), 0)),
            pl.BlockSpec((1, d), lambda i, n, j: (0, 0)),
            pl.BlockSpec((None, d, GROUP), lambda i, n, j: (layer, 0, jnp.minimum(n, N_GROUPS - 1))),
            pl.BlockSpec((d, LANE), lambda i, n, j: (0, 0)),
        ],
        out_specs=([ospec(GROUP, n_w) for n_w in range(1, N_GROUPS)] + [ospec(LANE, N_GROUPS)]
                   + [ospec(GROUP, 4), ospec(GROUP, 5)]),
        out_shape=out_shape,
        scratch_shapes=[pltpu.VMEM((share, tm, d), BF16), pltpu.VMEM((share, tm, GROUP), F32)],
        compiler_params=pltpu.CompilerParams(
            dimension_semantics=("arbitrary", "arbitrary", "arbitrary"), vmem_limit_bytes=VMEM_LIMIT),
        name="inproj",
    )(x2d, g, w_all, w_tail)


def _conv_kernel(u_ref, hist_ref, zc_ref, cw_ref, cb_ref, lg_ref, lb_ref, o_ref, ext_ref, c_ref,
                 *, tt, rc):
    t = pl.program_id(1)
    n_hist = CONV_WIDTH - 1

    for b in range(SUBLANE):
        keep = n_hist - b

        @pl.when(t == 0)
        def _(b=b, keep=keep):
            ext_ref[b, 0:keep, :] = hist_ref[HIST_PAD + b:HIST_ROWS, :]

        @pl.when(t > 0)
        def _(b=b, keep=keep):
            ext_ref[b, 0:keep, :] = ext_ref[b, tt:tt + keep, :]

    for b in range(SUBLANE):
        ext_ref[b, n_hist - b:n_hist - b + tt, :] = u_ref[...]

    half_c = D_CONV // 2

    def taps(r, carry):
        r0 = pl.multiple_of(r * (2 * SUBLANE), 2 * SUBLANE)
        for c0 in (0, half_c):
            cs = slice(c0, c0 + half_c)
            acc = [jnp.broadcast_to(cb_ref[:, cs], (SUBLANE, half_c)) for _ in range(2)]
            for b in range(SUBLANE):
                n_a = (CONV_WIDTH - 1 - b) // SUBLANE + 1
                win = ext_ref[b, pl.ds(r0, (n_a + 1) * SUBLANE), cs]
                for a in range(n_a):
                    j = a * SUBLANE + b
                    w8 = cw_ref[j * SUBLANE:(j + 1) * SUBLANE, cs]
                    for g in range(2):
                        lo = (a + g) * SUBLANE
                        acc[g] = acc[g] + w8 * win[lo:lo + SUBLANE, :]
            for g in range(2):
                c_ref[pl.ds(r0 + g * SUBLANE, SUBLANE), cs] = acc[g]
        return carry

    lax.fori_loop(0, tt // (2 * SUBLANE), taps, 0)

    def norm_gate(r, carry):
        r0 = pl.multiple_of(r * rc, rc)
        c = c_ref[pl.ds(r0, rc), :]
        mu = jnp.mean(c, axis=-1, keepdims=True)
        xc = c - mu
        var = jnp.mean(xc * xc, axis=-1, keepdims=True)
        y = xc * lax.rsqrt(var + NORM_EPS) * lg_ref[...] + lb_ref[...]
        z = zc_ref[pl.ds(r0, rc), :].astype(F32)
        o_ref[pl.ds(r0, rc), :] = (_silu(y) * _silu(z)).astype(BF16)
        return carry

    lax.fori_loop(0, tt // rc, norm_gate, 0)


def _conv(u, hist, zc, cw, cb, lg, lb):
    nb, t, c = u.shape
    tt = min(t, 256)
    rc = min(tt, 64)
    assert t % tt == 0 and tt % rc == 0 and tt % (2 * SUBLANE) == 0 and c == D_CONV
    hist_map = (lambda b, i: (b, 0, 0)) if hist.shape[0] == nb else (lambda b, i: (0, 0, 0))
    vec = pl.BlockSpec((1, c), lambda b, i: (0, 0))
    return pl.pallas_call(
        functools.partial(_conv_kernel, tt=tt, rc=rc),
        grid=(nb, t // tt),
        in_specs=[
            pl.BlockSpec((None, tt, c), lambda b, i: (b, i, 0)),
            pl.BlockSpec((None, HIST_ROWS, c), hist_map),
            pl.BlockSpec((None, tt, c), lambda b, i: (b, i, 0)),
            pl.BlockSpec((CONV_WIDTH * SUBLANE, c), lambda b, i: (0, 0)),
            vec, vec, vec,
        ],
        out_specs=pl.BlockSpec((None, tt, c), lambda b, i: (b, i, 0)),
        out_shape=jax.ShapeDtypeStruct((nb, t, c), BF16),
        scratch_shapes=[pltpu.VMEM((SUBLANE, HIST_ROWS + tt, c), F32), pltpu.VMEM((tt, c), F32)],
        compiler_params=pltpu.CompilerParams(
            dimension_semantics=("arbitrary", "arbitrary"), vmem_limit_bytes=VMEM_LIMIT),
        name="conv",
    )(u, hist, zc, cw, cb, lg, lb)


def _outproj_kernel(c_ref, a_ref, h_ref, wc_ref, wa_ref, fg_ref, o_ref, *, final):
    y = (jnp.dot(c_ref[...], wc_ref[...], preferred_element_type=F32)
         + jnp.dot(a_ref[...], wa_ref[...], preferred_element_type=F32))
    y = h_ref[...] + y
    if final:
        ms = jnp.mean(y * y, axis=-1, keepdims=True)
        y = y * lax.rsqrt(ms + NORM_EPS) * fg_ref[...]
    o_ref[...] = y


def _outproj(c, a, h, wo_all, layer, fg, tm, final):
    m, d = h.shape
    assert m % tm == 0 and D_CONV == D_ATT
    half = lambda: pl.BlockSpec((tm, D_CONV), lambda i: (i, 0))
    full = lambda: pl.BlockSpec((tm, d), lambda i: (i, 0))
    wspec = lambda part: pl.BlockSpec((None, D_CONV, d), lambda i: (layer, part, 0))
    return pl.pallas_call(
        functools.partial(_outproj_kernel, final=final),
        grid=(m // tm,),
        in_specs=[half(), half(), full(), wspec(0), wspec(1), pl.BlockSpec((1, d), lambda i: (0, 0))],
        out_specs=full(),
        out_shape=jax.ShapeDtypeStruct((m, d), F32),
        compiler_params=pltpu.CompilerParams(
            dimension_semantics=("arbitrary",), vmem_limit_bytes=VMEM_LIMIT),
        name="outproj",
    )(c, a, h, wo_all, wo_all, fg)


def _bias_kernel(bucket_ref, rb_ref, o_ref, *, n_near, tq):
    h = pl.program_id(0)
    far = rb_ref[FAR_BUCKET, h]
    for j in range(n_near):
        def body(r, carry):
            r0 = pl.multiple_of(r * SUBLANE, SUBLANE)
            bk = bucket_ref[j, pl.ds(r0, SUBLANE), :]
            acc = jnp.zeros(bk.shape, F32)
            for b in range(N_BUCKETS):
                acc = jnp.where(bk == b, rb_ref[b, h], acc)
            o_ref[j, pl.ds(r0, SUBLANE), :] = acc - far
            return carry
        lax.fori_loop(0, tq // SUBLANE, body, 0)


def _bias_tiles(bucket, rel_bias):
    n_near, tq, kb = bucket.shape
    return pl.pallas_call(
        functools.partial(_bias_kernel, n_near=n_near, tq=tq),
        grid=(N_HEADS,),
        in_specs=[pl.BlockSpec((n_near, tq, kb), lambda h: (0, 0, 0)),
                  pl.BlockSpec(memory_space=pltpu.SMEM)],
        out_specs=pl.BlockSpec((None, n_near, tq, kb), lambda h: (h, 0, 0, 0)),
        out_shape=jax.ShapeDtypeStruct((N_HEADS, n_near, tq, kb), F32),
        compiler_params=pltpu.CompilerParams(dimension_semantics=("arbitrary",)),
        name="bias_tiles",
    )(bucket, rel_bias)


def _t5_bucket(rel):
    half = N_BUCKETS // 2
    max_exact = half // 2
    n = jnp.abs(rel)
    large = max_exact + (jnp.log(jnp.maximum(n, 1).astype(jnp.float32) / max_exact)
                         / math.log(MAX_DISTANCE / max_exact) * (half - max_exact)).astype(jnp.int32)
    large = jnp.minimum(large, half - 1)
    return jnp.where(rel > 0, half, 0) + jnp.where(n < max_exact, n, large)


class _AttnCfg:
    def __init__(self, tq, n_rows, n_qt, s_pad, lane_off, n_pos, q_pos0, near_step, near_base, n_near, top_k,
                 edge_block, main_shift, n_main, dims_major=False):
        self.dims_major = dims_major
        self.edge_block, self.main_shift, self.n_main = edge_block, main_shift, n_main
        assert all(kb == edge_block or 0 <= kb - main_shift < n_main for kb in range(s_pad // KEY_BLOCK))
        self.tq, self.n_rows, self.n_qt, self.s_pad, self.lane_off = tq, n_rows, n_qt, s_pad, lane_off
        self.n_pos, self.q_pos0 = n_pos, q_pos0
        self.near_step, self.near_base, self.n_near, self.top_k = near_step, near_base, n_near, top_k
        self.rel0 = near_base * KEY_BLOCK - lane_off - q_pos0
        assert s_pad % KEY_BLOCK == 0 and tq % LANE == 0 and n_rows <= tq
        assert near_step * KEY_BLOCK == tq or n_qt == 1
        assert (near_base == 0 and near_step == 0) or 1 - self.rel0 >= MAX_DISTANCE
        assert (near_base + near_step * (n_qt - 1) + n_near) * KEY_BLOCK <= s_pad

    def bucket_table(self):
        j = jnp.arange(self.n_near, dtype=jnp.int32)[:, None, None]
        r = jnp.arange(self.tq, dtype=jnp.int32)[None, :, None]
        c = jnp.arange(KEY_BLOCK, dtype=jnp.int32)[None, None, :]
        rel = self.rel0 + KEY_BLOCK * j + c - r
        return _t5_bucket(lax.optimization_barrier(rel))


def _chunk_of(pos):
    return jnp.where(pos < N_META, 0, 1 + ((pos - N_META) >> CHUNK_SHIFT))


def _for_blocks(n, fn, group=2):
    def body(kg, carry):
        for r in range(group):
            fn(group * kg + r)
        return carry

    if isinstance(n, int):
        lax.fori_loop(0, n // group, body, 0)
        for kb in range(n - n % group, n):
            fn(kb)
        return
    lax.fori_loop(0, n >> (group.bit_length() - 1), body, 0)
    part = group // 2
    while part:
        start = n & ~(2 * part - 1)

        @pl.when((n & part) != 0)
        def _(start=start, part=part):
            for r in range(part):
                fn(start + r)

        part //= 2


def _attn_kernel(q_ref, qi_ref, w_ref, za_ref, k_ref, v_ref, kx_ref, ke_ref, ve_ref, kxe_ref,
                 bias_ref, tri_ref, o_ref,
                 qm_ref, qim_ref, wb_ref, keys_ref, madd_ref, s_ref, mx_ref, l_ref, acc_ref, ties_ref,
                 hi_ref, lo_ref, sel_ref, *, cfg):
    tq = cfg.tq
    i = pl.program_id(1)
    near0 = cfg.near_base + cfg.near_step * i if cfg.near_step else cfg.near_base
    nkb = near0 + cfg.n_near
    lane = lax.broadcasted_iota(jnp.int32, (tq, LANE), 1)
    row = lax.broadcasted_iota(jnp.int32, (tq, LANE), 0)
    low_half = lane < HEAD_DIM
    q_pos_base = cfg.q_pos0 + i * tq
    q_chunk = _chunk_of(q_pos_base + row)

    def admissible(blk):
        kpos = blk * LANE + lane - cfg.lane_off
        return (kpos >= 0) & (kpos < cfg.n_pos) & (_chunk_of(kpos) <= q_chunk)

    def key_operand(main_ref, edge_ref, fs, kb):
        def main(idx):
            ks = pl.ds(pl.multiple_of(idx * KEY_BLOCK, KEY_BLOCK), KEY_BLOCK)
            return (main_ref[fs, ks] if cfg.dims_major else main_ref[ks, fs]).astype(BF16)

        edge = edge_ref[fs, :] if cfg.dims_major else edge_ref[:, fs]
        if isinstance(kb, int):
            return edge if kb == cfg.edge_block else main(kb - cfg.main_shift)
        if cfg.n_main == 0:
            return edge
        return jnp.where(kb == cfg.edge_block, edge, main(jnp.clip(kb - cfg.main_shift, 0, cfg.n_main - 1)))

    def dot_keys(lhs, rhs):
        if cfg.dims_major:
            return jnp.dot(lhs, rhs, preferred_element_type=F32)
        return lax.dot_general(lhs, rhs, NT_DIMS, preferred_element_type=F32)

    def dot_values(lhs, rhs):
        if cfg.dims_major:
            return lax.dot_general(lhs, rhs, NT_DIMS, preferred_element_type=F32)
        return jnp.dot(lhs, rhs, preferred_element_type=F32)

    all_feat = slice(0, LANE)
    pair_feat = lambda p: slice(p * LANE, (p + 1) * LANE)

    zero = jnp.zeros((tq, LANE), BF16)
    for p in range(N_PAIRS):
        sl = slice(p * LANE, (p + 1) * LANE)
        qp = q_ref[:, sl]
        qm_ref[p, 0:tq, :] = jnp.where(low_half, qp, zero)
        qm_ref[p, tq:2 * tq, :] = jnp.where(low_half, zero, qp)
        qip = qi_ref[:, sl]
        qim_ref[p, 0:tq, :] = jnp.where(low_half, qip, zero)
        qim_ref[p, tq:2 * tq, :] = jnp.where(low_half, zero, qip)
    w_scale = (IDX_HEADS ** -0.5) * (IDX_DIM ** -0.5)
    for h in range(IDX_HEADS):
        col = IDX_DIM + h
        wb_ref[h] = jnp.broadcast_to(w_ref[:, col:col + 1] * w_scale, (tq, LANE))

    def index_block(kb):
        kx = key_operand(kx_ref, kxe_ref, all_feat, kb)
        acc = [jnp.zeros((tq, LANE), F32) for _ in range(HALF)]
        for p in range(N_PAIRS):
            d = jnp.maximum(dot_keys(qim_ref[p], kx), 0.0)
            for hf in range(HALF):
                cs = slice(hf * LANE, (hf + 1) * LANE)
                acc[hf] = acc[hf] + wb_ref[2 * p] * d[0:tq, cs] + wb_ref[2 * p + 1] * d[tq:2 * tq, cs]
        for hf in range(HALF):
            blk = kb * HALF + hf
            sc = jnp.where(admissible(blk), acc[hf] + 0.0, NEG)
            bits = pltpu.bitcast(sc.T, jnp.int32)
            key = jnp.where(bits < 0, bits ^ jnp.int32(0x7FFFFFFF), bits)
            rows = pl.ds(pl.multiple_of(blk * LANE, LANE), LANE)
            keys_ref[rows, :] = key
            hi_ref[rows, :] = (key >> HALF_BITS).astype(jnp.int16)
            lo_ref[rows, :] = ((key & LOW_MASK) - HALF_BIAS).astype(jnp.int16)

    _for_blocks(nkb, index_block)

    kf = jnp.float32(cfg.top_k)
    n_part = 4
    packed_rows = 2 * SUBLANE

    def count16(ref, pred):
        def body(c_i, parts):
            chunk = ref[pl.ds(pl.multiple_of(c_i * LANE, LANE), LANE), :]
            parts = list(parts)
            for j in range(LANE // packed_rows):
                kk = chunk[j * packed_rows:(j + 1) * packed_rows, :]
                parts[j % n_part] = parts[j % n_part] + jnp.where(pred(kk), jnp.int16(1), jnp.int16(0))
            return tuple(parts)
        parts = lax.fori_loop(0, nkb * HALF, body,
                              tuple(jnp.zeros((packed_rows, tq), jnp.int16) for _ in range(n_part)))
        tot = ((parts[0] + parts[1]) + (parts[2] + parts[3])).astype(jnp.int32)
        return jnp.broadcast_to(jnp.sum(tot, axis=0, keepdims=True), (SUBLANE, tq)).astype(F32)

    def pack16(v):
        return jnp.concatenate([v, v], axis=0).astype(jnp.int16)

    def bisect16(ref, target):
        def step(it, lo):
            cand = lo + lax.shift_left(jnp.int32(1), HALF_BITS - 1 - it)
            cand16 = pack16(cand)
            return jnp.where(count16(ref, lambda kk: kk >= cand16) >= target, cand, lo)
        return lax.fori_loop(0, HALF_BITS, step, jnp.full((SUBLANE, tq), -HALF_BIAS, jnp.int32))

    hi_thr = bisect16(hi_ref, kf)
    hi_thr16 = pack16(hi_thr)
    above = count16(hi_ref, lambda kk: kk > hi_thr16)

    def low_keys(c_i, carry):
        rows = pl.ds(pl.multiple_of(c_i * LANE, LANE), LANE)
        hi, lo = hi_ref[rows, :], lo_ref[rows, :]
        for j in range(LANE // packed_rows):
            rs = slice(j * packed_rows, (j + 1) * packed_rows)
            sel_ref[pl.ds(pl.multiple_of(c_i * LANE, LANE) + j * packed_rows, packed_rows), :] = jnp.where(
                hi[rs, :] == hi_thr16, lo[rs, :], jnp.int16(-HALF_BIAS))
        return carry

    lax.fori_loop(0, nkb * HALF, low_keys, 0)
    lo_thr = bisect16(sel_ref, kf - above)
    thr8 = (hi_thr << HALF_BITS) | ((lo_thr + HALF_BIAS) & LOW_MASK)

    def count(pred):
        def body(c_i, parts):
            chunk = keys_ref[pl.ds(pl.multiple_of(c_i * LANE, LANE), LANE), :]
            parts = list(parts)
            for j in range(LANE // SUBLANE):
                kk = chunk[j * SUBLANE:(j + 1) * SUBLANE, :]
                parts[j % n_part] = parts[j % n_part] + jnp.where(pred(kk), 1.0, 0.0)
            return tuple(parts)
        parts = lax.fori_loop(0, nkb * HALF, body, tuple(jnp.zeros((SUBLANE, tq), F32) for _ in range(n_part)))
        tot = (parts[0] + parts[1]) + (parts[2] + parts[3])
        return jnp.broadcast_to(jnp.sum(tot, axis=0, keepdims=True), (SUBLANE, tq))

    need8 = kf - count(lambda kk: kk > thr8)
    thr = thr8[0:1, :]
    need = need8[0:1, :]

    key_row = lax.broadcasted_iota(jnp.int32, (KEY_BLOCK, tq), 0)
    q_chunk_t = _chunk_of(q_pos_base + lax.broadcasted_iota(jnp.int32, (KEY_BLOCK, tq), 1))

    ties_ref[...] = jnp.zeros((SUBLANE, tq), F32)

    def mask_block(kb):
        r0 = pl.multiple_of(kb * KEY_BLOCK, KEY_BLOCK)
        kk = keys_ref[pl.ds(r0, KEY_BLOCK), :]
        eq = kk == thr
        eqf = jnp.where(eq, 1.0, 0.0)
        seen = ties_ref[...]
        rank = seen[0:1, :] + jnp.dot(tri_ref[...], eqf.astype(BF16), preferred_element_type=F32)
        sel = jnp.where(eq, jnp.where(rank <= need, 0.0, NEG), jnp.where(kk > thr, 0.0, NEG))
        kpos = r0 + key_row - cfg.lane_off
        ok = (kpos >= 0) & (kpos < cfg.n_pos) & (_chunk_of(kpos) <= q_chunk_t)
        madd_ref[kb] = jnp.where(ok, sel, NEG).T
        ties_ref[...] = seen + jnp.sum(eqf, axis=0, keepdims=True)

    _for_blocks(nkb, mask_block)

    def logits(g, p, kb, near_j):
        s = dot_keys(qm_ref[p], key_operand(k_ref, ke_ref, pair_feat(p), kb))
        ma = madd_ref[kb]
        if near_j is None:
            add = jnp.concatenate([ma, ma], axis=0)
        else:
            add = jnp.concatenate([ma + bias_ref[2 * p, near_j], ma + bias_ref[2 * p + 1, near_j]], axis=0)
        s = s + add
        s_ref[g, kb] = s
        mx_ref[g] = jnp.maximum(mx_ref[g], jnp.maximum(s[:, 0:LANE], s[:, LANE:KEY_BLOCK]))

    def weighted_values(g, p, kb):
        m = mx_ref[g]
        pe = jnp.exp(s_ref[g, kb] - jnp.concatenate([m] * HALF, axis=1))
        l_ref[g] = l_ref[g] + (pe[:, 0:LANE] + pe[:, LANE:KEY_BLOCK])
        acc_ref[g] = acc_ref[g] + dot_values(pe.astype(BF16), key_operand(v_ref, ve_ref, pair_feat(p), kb))

    for stage in range(N_PAIRS + 1):
        p_a = stage if stage < N_PAIRS else None
        p_b = stage - 1 if stage > 0 else None

        def both(kb, near_j, p_a=p_a, p_b=p_b):
            if p_a is not None:
                logits(p_a % 2, p_a, kb, near_j)
            if p_b is not None:
                weighted_values(p_b % 2, p_b, kb)

        if p_a is not None:
            mx_ref[p_a % 2] = jnp.full((2 * tq, LANE), M_INIT, F32)

        _for_blocks(near0, lambda kb, both=both: both(kb, None), group=4)
        for j in range(cfg.n_near):
            both(near0 + j, j)

        if p_a is not None:
            g = p_a % 2
            mx_ref[g] = jnp.broadcast_to(jnp.max(mx_ref[g], axis=1, keepdims=True), (2 * tq, LANE))
            l_ref[g] = jnp.zeros((2 * tq, LANE), F32)
            acc_ref[g] = jnp.zeros((2 * tq, LANE), F32)
        if p_b is not None:
            g = p_b % 2
            o = acc_ref[g] / jnp.sum(l_ref[g], axis=1, keepdims=True)
            o = jnp.where(low_half, o[0:tq], o[tq:2 * tq])
            z = za_ref[:, p_b * LANE:(p_b + 1) * LANE].astype(F32)
            o_ref[:, p_b * LANE:(p_b + 1) * LANE] = (o * _silu(z)).astype(BF16)


def _attention(cfg, q, qi, w, za, mains, edges, bias, tri, main_lead=()):
    nb, t, _ = q.shape
    tq, s_pad = cfg.tq, cfg.s_pad
    assert t == tq * cfg.n_qt
    qspec = lambda width: pl.BlockSpec((None, tq, width), lambda b, i: (b, i, 0))

    def kspec(a, lead):
        per_batch = a.shape[len(lead)] == nb
        assert per_batch or a.shape[len(lead)] == 1
        return pl.BlockSpec((None,) * (len(lead) + 1) + a.shape[-2:],
                            lambda b, i: lead + ((b if per_batch else 0), 0, 0),
                            pipeline_mode=pl.Buffered(1))

    key_axis = -1 if cfg.dims_major else -2
    for a in mains:
        assert a.shape[key_axis] >= cfg.n_main * KEY_BLOCK
    for a in edges:
        assert a.shape[key_axis] == KEY_BLOCK
    nkb_max = s_pad // KEY_BLOCK
    return pl.pallas_call(
        functools.partial(_attn_kernel, cfg=cfg),
        grid=(nb, cfg.n_qt),
        in_specs=[
            qspec(D_ATT), qspec(D_ATT), qspec(LANE), qspec(D_ATT),
            *[kspec(a, tuple(main_lead)) for a in mains],
            *[kspec(a, ()) for a in edges],
            pl.BlockSpec((N_HEADS, cfg.n_near, tq, KEY_BLOCK), lambda b, i: (0, 0, 0, 0),
                         pipeline_mode=pl.Buffered(1)),
            pl.BlockSpec((KEY_BLOCK, KEY_BLOCK), lambda b, i: (0, 0)),
        ],
        out_specs=qspec(D_ATT),
        out_shape=jax.ShapeDtypeStruct((nb, t, D_ATT), BF16),
        scratch_shapes=[
            pltpu.VMEM((N_PAIRS, 2 * tq, LANE), BF16),
            pltpu.VMEM((N_PAIRS, 2 * tq, LANE), BF16),
            pltpu.VMEM((IDX_HEADS, tq, LANE), F32),
            pltpu.VMEM((s_pad, tq), jnp.int32),
            pltpu.VMEM((nkb_max, tq, KEY_BLOCK), F32),
            pltpu.VMEM((PAIR_BUFFERS, nkb_max, 2 * tq, KEY_BLOCK), F32),
            pltpu.VMEM((PAIR_BUFFERS, 2 * tq, LANE), F32),
            pltpu.VMEM((PAIR_BUFFERS, 2 * tq, LANE), F32),
            pltpu.VMEM((PAIR_BUFFERS, 2 * tq, LANE), F32),
            pltpu.VMEM((SUBLANE, tq), F32),
            pltpu.VMEM((s_pad, tq), jnp.int16),
            pltpu.VMEM((s_pad, tq), jnp.int16),
            pltpu.VMEM((s_pad, tq), jnp.int16),
        ],
        compiler_params=pltpu.CompilerParams(
            dimension_semantics=("arbitrary", "arbitrary"), vmem_limit_bytes=VMEM_LIMIT),
        name="attention",
    )(q, qi, w, za, *mains, *edges, bias, tri)


def _edge_block(cfg, parts, lead_zeros):
    key_axis = 2 if cfg.dims_major else 1
    parts = [a.astype(BF16) for a in parts]

    def zeros(n):
        shape = list(parts[0].shape)
        shape[key_axis] = n
        return [jnp.zeros(shape, BF16)] if n else []

    n_end = KEY_BLOCK - lead_zeros - sum(a.shape[key_axis] for a in parts)
    return jnp.concatenate(zeros(lead_zeros) + parts + zeros(n_end), axis=key_axis)


def _attend(cfg, q, qi, tail, za, mains, edges, bias, tri, main_lead=()):
    rows = lambda a: jnp.pad(a, ((0, 0), (0, cfg.tq * cfg.n_qt - a.shape[1]), (0, 0)))
    out = _attention(cfg, rows(q), rows(qi), rows(tail), rows(za), mains, edges, bias, tri, main_lead)
    return out[:, :q.shape[1]]


def _forward(x_prompt, x_sample, cache_k, cache_v, cache_kidx, state_conv, meta_tokens,
             norm_g, w_in, conv_w, conv_b, conv_ln_g, conv_ln_b, w_out, rel_bias, final_g,
             *, tq_frames=256, tm_frames=512):
    depth = w_in.shape[0]
    bp, seq, d = x_prompt.shape
    bs, dec, _ = x_sample.shape
    past = cache_k.shape[2] - N_META
    assert seq % tq_frames == 0 and seq % CHUNK == 0 and (bp * seq) % tm_frames == 0

    n_qt = seq // tq_frames
    cfg_f = _AttnCfg(tq=tq_frames, n_rows=tq_frames, n_qt=n_qt, s_pad=KEY_BLOCK + seq,
                     lane_off=KEY_BLOCK - N_META, n_pos=N_META + seq, q_pos0=N_META,
                     near_step=tq_frames // KEY_BLOCK, near_base=0,
                     n_near=tq_frames // KEY_BLOCK + 1, top_k=min(TOPK_MAX, seq // 4),
                     edge_block=0, main_shift=1, n_main=seq // KEY_BLOCK)
    n_cache = N_META + past
    n_pos_s = n_cache + dec
    s_pad_s = -(-n_pos_s // KEY_BLOCK) * KEY_BLOCK
    last_s = s_pad_s // KEY_BLOCK - 1
    near_s = max(0, (n_cache - MAX_DISTANCE) // KEY_BLOCK)
    cfg_s = _AttnCfg(tq=LANE, n_rows=dec, n_qt=1, s_pad=s_pad_s, lane_off=0,
                     n_pos=n_pos_s, q_pos0=n_cache, near_step=0, near_base=near_s,
                     n_near=last_s - near_s + 1, top_k=min(TOPK_MAX, (past + dec) // 4),
                     edge_block=last_s, main_shift=0, n_main=last_s, dims_major=True)
    cfg_m = _AttnCfg(tq=LANE, n_rows=N_META, n_qt=1, s_pad=KEY_BLOCK, lane_off=KEY_BLOCK - N_META,
                     n_pos=N_META, q_pos0=0, near_step=0, near_base=0, n_near=1,
                     top_k=min(TOPK_MAX, seq // 4), edge_block=0, main_shift=1, n_main=0)
    assert dec <= LANE and n_cache >= last_s * KEY_BLOCK

    bias_f = _bias_tiles(cfg_f.bucket_table(), rel_bias)
    bias_s = _bias_tiles(cfg_s.bucket_table(), rel_bias)
    bias_m = _bias_tiles(cfg_m.bucket_table(), rel_bias)
    tri = (jnp.arange(KEY_BLOCK)[None, :] <= jnp.arange(KEY_BLOCK)[:, None]).astype(BF16)

    n_aux = bs * dec + N_META
    hf = x_prompt.reshape(bp * seq, d)
    haux = jnp.concatenate([x_sample.reshape(bs * dec, d), meta_tokens.astype(x_sample.dtype)], axis=0)

    w_all = w_in.astype(BF16)
    wo_all = w_out.astype(BF16)
    feat_major = lambda c: jnp.transpose(c, (0, 1, 3, 4, 2)).reshape(depth, bs, D_ATT, n_cache)
    ck_t, cv_t = feat_major(cache_k), feat_major(cache_v)
    cx_t = jnp.swapaxes(cache_kidx, 2, 3).astype(BF16)
    cx_t = jnp.concatenate([cx_t, cx_t], axis=2)
    outs = {name: [] for name in ("kp", "vp", "kip", "cp", "ks", "vs", "kis", "cs")}
    for l in range(depth):
        w_tail = jnp.pad(w_all[l, :, N_GROUPS * GROUP:], ((0, 0), (0, LANE - IDX_DIM - IDX_HEADS)))
        g = norm_g[l][None, :]
        cw = jnp.repeat(conv_w[l], SUBLANE, axis=0)
        cb, lg, lb = conv_b[l][None, :], conv_ln_g[l][None, :], conv_ln_b[l][None, :]
        final = l == depth - 1
        fg = final_g[None, :]

        uA, zcA, qA, kA, vA, zaA, qiA, tailA, kbA, vbA = _inproj(haux, g, w_all, l, w_tail, n_aux)
        uF, zcF, qF, kF, vF, zaF, qiF, tailF, kbF, vbF = _inproj(hf, g, w_all, l, w_tail, tm_frames)
        ns = bs * dec
        split = lambda a: (a[:ns].reshape(bs, dec, -1), a[ns:][None])
        uS, uM = split(uA); zcS, zcM = split(zcA); qS, qM = split(qA); kS, kM = split(kA)
        vS, vM = split(vA); zaS, zaM = split(zaA); qiS, qiM = split(qiA); tailS, tailM = split(tailA)
        kbS, kbM = split(kbA); vbS, vbM = split(vbA)
        b3 = lambda a: a.reshape(bp, seq, -1)
        uF, zcF, qF, kF, vF, zaF, qiF, tailF, kbF, vbF = map(
            b3, (uF, zcF, qF, kF, vF, zaF, qiF, tailF, kbF, vbF))

        zero_hist = jnp.zeros((1, HIST_ROWS, D_CONV), F32)
        cM = _conv(uM, zero_hist, zcM, cw, cb, lg, lb)
        hist_f = jnp.concatenate([jnp.zeros((1, HIST_ROWS - N_META, D_CONV), F32), uM], axis=1)
        cF = _conv(uF, hist_f, zcF, cw, cb, lg, lb)
        hist_s = jnp.pad(state_conv[l].astype(F32), ((0, 0), (HIST_PAD, 0), (0, 0)))
        cS = _conv(uS, hist_s, zcS, cw, cb, lg, lb)

        kidx = lambda tail: tail[..., :IDX_DIM]
        twice = lambda a: jnp.concatenate([a, a], axis=-1).astype(BF16)
        meta_edges = tuple(_edge_block(cfg_f, [a], cfg_f.lane_off) for a in (kbM, vbM, twice(kidx(tailM))))
        aM = _attend(cfg_m, qM, qiM, tailM, zaM, meta_edges, meta_edges, bias_m, tri)
        aF = _attend(cfg_f, qF, qiF, tailF, zaF, (kbF, vbF, twice(kidx(tailF))), meta_edges, bias_f, tri)
        tr = lambda a: jnp.swapaxes(a, 1, 2)
        lo = cfg_s.edge_block * KEY_BLOCK
        sample_edges = tuple(_edge_block(cfg_s, [c[l, :, :, lo:], tr(new)], 0)
                             for c, new in ((ck_t, kbS), (cv_t, vbS), (cx_t, twice(kidx(tailS)))))
        aS = _attend(cfg_s, qS, qiS, tailS, zaS, (ck_t, cv_t, cx_t), sample_edges, bias_s, tri, main_lead=(l,))

        cAux = jnp.concatenate([cS.reshape(ns, D_CONV), cM[0]], axis=0)
        aAux = jnp.concatenate([aS.reshape(ns, D_ATT), aM[0]], axis=0)
        haux = _outproj(cAux, aAux, haux, wo_all, l, fg, n_aux, final)
        hf = _outproj(cF.reshape(bp * seq, D_CONV), aF.reshape(bp * seq, D_ATT), hf, wo_all, l, fg,
                      tm_frames, final)

        with_meta = lambda m, f: jnp.concatenate([jnp.broadcast_to(m, (bp,) + m.shape[1:]), f], axis=1)
        outs["kp"].append(with_meta(kM, kF).reshape(bp, N_META + seq, N_HEADS, HEAD_DIM))
        outs["vp"].append(with_meta(vM, vF).reshape(bp, N_META + seq, N_HEADS, HEAD_DIM))
        outs["kip"].append(with_meta(kidx(tailM), kidx(tailF)))
        u_ext_p = jnp.concatenate([jnp.zeros((bp, CONV_WIDTH - 1, D_CONV), F32),
                                   jnp.broadcast_to(uM, (bp, N_META, D_CONV)), uF], axis=1)
        outs["cp"].append(u_ext_p[:, -(CONV_WIDTH - 1):])
        outs["ks"].append(kS.reshape(bs, dec, N_HEADS, HEAD_DIM))
        outs["vs"].append(vS.reshape(bs, dec, N_HEADS, HEAD_DIM))
        outs["kis"].append(kidx(tailS))
        u_ext_s = jnp.concatenate([state_conv[l].astype(F32), uS], axis=1)
        outs["cs"].append(u_ext_s[:, -(CONV_WIDTH - 1):])

    y_prompt = hf.reshape(bp, seq, d)
    y_sample = haux[:bs * dec].reshape(bs, dec, d)
    st = lambda name: jnp.stack(outs[name])
    return (y_prompt, y_sample, st("kp"), st("vp"), st("kip"), st("cp"),
            st("ks"), st("vs"), st("kis"), st("cs"))


def kernel(x_prompt, x_sample, cache_k, cache_v, cache_kidx, state_conv, meta_tokens, norm_g, w_in,
           conv_w, conv_b, conv_ln_g, conv_ln_b, w_out, rel_bias, final_g):
    return _forward(x_prompt, x_sample, cache_k, cache_v, cache_kidx, state_conv, meta_tokens,
                    norm_g, w_in, conv_w, conv_b, conv_ln_g, conv_ln_b, w_out, rel_bias, final_g)
```

```python
import functools
import math

import jax
import jax.numpy as jnp
from jax import lax
from jax.experimental import pallas as pl
from jax.experimental.pallas import tpu as pltpu

N_META = 16
CHUNK = 64
CHUNK_SHIFT = 6
D_CONV = 1024
CONV_WIDTH = 31
N_HEADS = 16
HEAD_DIM = 64
D_ATT = N_HEADS * HEAD_DIM
IDX_HEADS = 16
IDX_DIM = 64
TOPK_MAX = 256
N_BUCKETS = 32
MAX_DISTANCE = 128
NORM_EPS = 1e-6
NEG = -1e30

LANE = 128
SUBLANE = 8
KEY_BLOCK = 256
HALF = KEY_BLOCK // LANE
N_PAIRS = N_HEADS // 2
PAIR_BUFFERS = 2
HIST_ROWS = 32
HIST_PAD = HIST_ROWS - (CONV_WIDTH - 1)
GROUP = 1024
N_GROUPS = 8
VMEM_LIMIT = 56 * 1024 * 1024
INT_MIN = -2147483648
HALF_BITS = 16
LOW_MASK = (1 << HALF_BITS) - 1
HALF_BIAS = 1 << (HALF_BITS - 1)
M_INIT = -1e29
FAR_BUCKET = N_BUCKETS // 2 - 1

F32 = jnp.float32
BF16 = jnp.bfloat16
NT_DIMS = (((1,), (1,)), ((), ()))


def _silu(x):
    return x * jax.nn.sigmoid(x)


def _inproj_kernel(x_ref, g_ref, wm_ref, wt_ref,
                   u_ref, zc_ref, q_ref, k_ref, v_ref, za_ref, qi_ref, tail_ref, kb_ref, vb_ref,
                   xn_ref, a_ref):
    n = pl.program_id(1)
    j = pl.program_id(2)

    @pl.when(n == 0)
    def _():
        x = x_ref[...]
        ms = jnp.mean(x * x, axis=-1, keepdims=True)
        xn_ref[j] = (x * lax.rsqrt(ms + NORM_EPS) * g_ref[...]).astype(BF16)

    def group():
        return jnp.dot(xn_ref[j], wm_ref[...], preferred_element_type=F32)

    @pl.when(n == 0)
    def _():
        a_ref[j] = group()

    @pl.when(n == 1)
    def _():
        u_ref[...] = a_ref[j] * jax.nn.sigmoid(group())

    @pl.when(n == 2)
    def _():
        zc_ref[...] = group().astype(BF16)

    @pl.when(n == 3)
    def _():
        q_ref[...] = (group() * (HEAD_DIM ** -0.5)).astype(BF16)

    @pl.when(n == 4)
    def _():
        y = group()
        k_ref[...] = y
        kb_ref[...] = y.astype(BF16)

    @pl.when(n == 5)
    def _():
        y = group()
        v_ref[...] = y
        vb_ref[...] = y.astype(BF16)

    @pl.when(n == 6)
    def _():
        za_ref[...] = group().astype(BF16)

    @pl.when(n == 7)
    def _():
        qi_ref[...] = group().astype(BF16)

    @pl.when(n == N_GROUPS)
    def _():
        tail_ref[...] = jnp.dot(xn_ref[j], wt_ref[...], preferred_element_type=F32)


def _inproj(x2d, g, w_all, layer, w_tail, tm):
    m, d = x2d.shape
    assert m % tm == 0 and w_all.shape[2] >= N_GROUPS * GROUP
    n_tiles = m // tm
    share = 2 if n_tiles % 2 == 0 else 1
    last = n_tiles - 1
    row = lambda width, dtype: jax.ShapeDtypeStruct((m, width), dtype)
    out_shape = (row(GROUP, F32), row(GROUP, BF16), row(GROUP, BF16), row(GROUP, F32),
                 row(GROUP, F32), row(GROUP, BF16), row(GROUP, BF16), row(LANE, F32),
                 row(GROUP, BF16), row(GROUP, BF16))

    def tile_at(n_w):
        def index(i, n, j):
            first = i * share
            return jnp.where(n < n_w, jnp.maximum(first - 1, 0),
                             jnp.where(n == n_w, first + j, jnp.minimum(first + share - 1, last)))
        return index

    ospec = lambda width, n_w: pl.BlockSpec((tm, width), lambda i, n, j: (tile_at(n_w)(i, n, j), 0))
    return pl.pallas_call(
        _inproj_kernel,
        grid=(n_tiles // share, N_GROUPS + 1, share),
        in_specs=[
            pl.BlockSpec((tm, d), lambda i, n, j: (tile_at(0)(i, n, j), 0)),
            pl.BlockSpec((1, d), lambda i, n, j: (0, 0)),
            pl.BlockSpec((None, d, GROUP), lambda i, n, j: (layer, 0, jnp.minimum(n, N_GROUPS - 1))),
            pl.BlockSpec((d, LANE), lambda i, n, j: (0, 0)),
        ],
        out_specs=([ospec(GROUP, n_w) for n_w in range(1, N_GROUPS)] + [ospec(LANE, N_GROUPS)]
                   + [ospec(GROUP, 4), ospec(GROUP, 5)]),
        out_shape=out_shape,
        scratch_shapes=[pltpu.VMEM((share, tm, d), BF16), pltpu.VMEM((share, tm, GROUP), F32)],
        compiler_params=pltpu.CompilerParams(
            dimension_semantics=("arbitrary", "arbitrary", "arbitrary"), vmem_limit_bytes=VMEM_LIMIT),
        name="inproj",
    )(x2d, g, w_all, w_tail)


def _conv_kernel(u_ref, hist_ref, zc_ref, cw_ref, cb_ref, lg_ref, lb_ref, o_ref, ext_ref, c_ref,
                 *, tt, rc):
    t = pl.program_id(1)
    n_hist = CONV_WIDTH - 1

    for b in range(SUBLANE):
        keep = n_hist - b

        @pl.when(t == 0)
        def _(b=b, keep=keep):
            ext_ref[b, 0:keep, :] = hist_ref[HIST_PAD + b:HIST_ROWS, :]

        @pl.when(t > 0)
        def _(b=b, keep=keep):
            ext_ref[b, 0:keep, :] = ext_ref[b, tt:tt + keep, :]

    for b in range(SUBLANE):
        ext_ref[b, n_hist - b:n_hist - b + tt, :] = u_ref[...]

    half_c = D_CONV // 2

    def taps(r, carry):
        r0 = pl.multiple_of(r * (2 * SUBLANE), 2 * SUBLANE)
        for c0 in (0, half_c):
            cs = slice(c0, c0 + half_c)
            acc = [jnp.broadcast_to(cb_ref[:, cs], (SUBLANE, half_c)) for _ in range(2)]
            for b in range(SUBLANE):
                n_a = (CONV_WIDTH - 1 - b) // SUBLANE + 1
                win = ext_ref[b, pl.ds(r0, (n_a + 1) * SUBLANE), cs]
                for a in range(n_a):
                    j = a * SUBLANE + b
                    w8 = cw_ref[j * SUBLANE:(j + 1) * SUBLANE, cs]
                    for g in range(2):
                        lo = (a + g) * SUBLANE
                        acc[g] = acc[g] + w8 * win[lo:lo + SUBLANE, :]
            for g in range(2):
                c_ref[pl.ds(r0 + g * SUBLANE, SUBLANE), cs] = acc[g]
        return carry

    lax.fori_loop(0, tt // (2 * SUBLANE), taps, 0)

    def norm_gate(r, carry):
        r0 = pl.multiple_of(r * rc, rc)
        c = c_ref[pl.ds(r0, rc), :]
        mu = jnp.mean(c, axis=-1, keepdims=True)
        xc = c - mu
        var = jnp.mean(xc * xc, axis=-1, keepdims=True)
        y = xc * lax.rsqrt(var + NORM_EPS) * lg_ref[...] + lb_ref[...]
        z = zc_ref[pl.ds(r0, rc), :].astype(F32)
        o_ref[pl.ds(r0, rc), :] = (_silu(y) * _silu(z)).astype(BF16)
        return carry

    lax.fori_loop(0, tt // rc, norm_gate, 0)


def _conv(u, hist, zc, cw, cb, lg, lb):
    nb, t, c = u.shape
    tt = min(t, 256)
    rc = min(tt, 64)
    assert t % tt == 0 and tt % rc == 0 and tt % (2 * SUBLANE) == 0 and c == D_CONV
    hist_map = (lambda b, i: (b, 0, 0)) if hist.shape[0] == nb else (lambda b, i: (0, 0, 0))
    vec = pl.BlockSpec((1, c), lambda b, i: (0, 0))
    return pl.pallas_call(
        functools.partial(_conv_kernel, tt=tt, rc=rc),
        grid=(nb, t // tt),
        in_specs=[
            pl.BlockSpec((None, tt, c), lambda b, i: (b, i, 0)),
            pl.BlockSpec((None, HIST_ROWS, c), hist_map),
            pl.BlockSpec((None, tt, c), lambda b, i: (b, i, 0)),
            pl.BlockSpec((CONV_WIDTH * SUBLANE, c), lambda b, i: (0, 0)),
            vec, vec, vec,
        ],
        out_specs=pl.BlockSpec((None, tt, c), lambda b, i: (b, i, 0)),
        out_shape=jax.ShapeDtypeStruct((nb, t, c), BF16),
        scratch_shapes=[pltpu.VMEM((SUBLANE, HIST_ROWS + tt, c), F32), pltpu.VMEM((tt, c), F32)],
        compiler_params=pltpu.CompilerParams(
            dimension_semantics=("arbitrary", "arbitrary"), vmem_limit_bytes=VMEM_LIMIT),
        name="conv",
    )(u, hist, zc, cw, cb, lg, lb)


def _outproj_kernel(c_ref, a_ref, h_ref, wc_ref, wa_ref, fg_ref, o_ref, *, final):
    y = (jnp.dot(c_ref[...], wc_ref[...], preferred_element_type=F32)
         + jnp.dot(a_ref[...], wa_ref[...], preferred_element_type=F32))
    y = h_ref[...] + y
    if final:
        ms = jnp.mean(y * y, axis=-1, keepdims=True)
        y = y * lax.rsqrt(ms + NORM_EPS) * fg_ref[...]
    o_ref[...] = y


def _outproj(c, a, h, wo_all, layer, fg, tm, final):
    m, d = h.shape
    assert m % tm == 0 and D_CONV == D_ATT
    half = lambda: pl.BlockSpec((tm, D_CONV), lambda i: (i, 0))
    full = lambda: pl.BlockSpec((tm, d), lambda i: (i, 0))
    wspec = lambda part: pl.BlockSpec((None, D_CONV, d), lambda i: (layer, part, 0))
    return pl.pallas_call(
        functools.partial(_outproj_kernel, final=final),
        grid=(m // tm,),
        in_specs=[half(), half(), full(), wspec(0), wspec(1), pl.BlockSpec((1, d), lambda i: (0, 0))],
        out_specs=full(),
        out_shape=jax.ShapeDtypeStruct((m, d), F32),
        compiler_params=pltpu.CompilerParams(
            dimension_semantics=("arbitrary",), vmem_limit_bytes=VMEM_LIMIT),
        name="outproj",
    )(c, a, h, wo_all, wo_all, fg)


def _bias_kernel(bucket_ref, rb_ref, o_ref, *, n_near, tq):
    h = pl.program_id(0)
    far = rb_ref[FAR_BUCKET, h]
    o_ref[0] = jnp.zeros((tq, KEY_BLOCK), F32)
    for j in range(n_near):
        def body(r, carry):
            r0 = pl.multiple_of(r * SUBLANE, SUBLANE)
            bk = bucket_ref[j, pl.ds(r0, SUBLANE), :]
            acc = jnp.zeros(bk.shape, F32)
            for b in range(N_BUCKETS):
                acc = jnp.where(bk == b, rb_ref[b, h], acc)
            o_ref[j + 1, pl.ds(r0, SUBLANE), :] = acc - far
            return carry
        lax.fori_loop(0, tq // SUBLANE, body, 0)


def _bias_tiles(bucket, rel_bias):
    n_near, tq, kb = bucket.shape
    return pl.pallas_call(
        functools.partial(_bias_kernel, n_near=n_near, tq=tq),
        grid=(N_HEADS,),
        in_specs=[pl.BlockSpec((n_near, tq, kb), lambda h: (0, 0, 0)),
                  pl.BlockSpec(memory_space=pltpu.SMEM)],
        out_specs=pl.BlockSpec((None, 1 + n_near, tq, kb), lambda h: (h, 0, 0, 0)),
        out_shape=jax.ShapeDtypeStruct((N_HEADS, 1 + n_near, tq, kb), F32),
        compiler_params=pltpu.CompilerParams(dimension_semantics=("arbitrary",)),
        name="bias_tiles",
    )(bucket, rel_bias)


def _t5_bucket(rel):
    half = N_BUCKETS // 2
    max_exact = half // 2
    n = jnp.abs(rel)
    large = max_exact + (jnp.log(jnp.maximum(n, 1).astype(jnp.float32) / max_exact)
                         / math.log(MAX_DISTANCE / max_exact) * (half - max_exact)).astype(jnp.int32)
    large = jnp.minimum(large, half - 1)
    return jnp.where(rel > 0, half, 0) + jnp.where(n < max_exact, n, large)


class _AttnCfg:
    def __init__(self, tq, n_rows, n_qt, s_pad, lane_off, n_pos, q_pos0, near_step, near_base, n_near, top_k,
                 edge_block, main_shift, n_main, dims_major=False):
        self.dims_major = dims_major
        self.edge_block, self.main_shift, self.n_main = edge_block, main_shift, n_main
        assert all(kb == edge_block or 0 <= kb - main_shift < n_main for kb in range(s_pad // KEY_BLOCK))
        self.tq, self.n_rows, self.n_qt, self.s_pad, self.lane_off = tq, n_rows, n_qt, s_pad, lane_off
        self.n_pos, self.q_pos0 = n_pos, q_pos0
        self.near_step, self.near_base, self.n_near, self.top_k = near_step, near_base, n_near, top_k
        self.rel0 = near_base * KEY_BLOCK - lane_off - q_pos0
        assert s_pad % KEY_BLOCK == 0 and tq % LANE == 0 and n_rows <= tq
        assert near_step * KEY_BLOCK == tq or n_qt == 1
        assert (near_base == 0 and near_step == 0) or 1 - self.rel0 >= MAX_DISTANCE
        assert (near_base + near_step * (n_qt - 1) + n_near) * KEY_BLOCK <= s_pad

    def bucket_table(self):
        j = jnp.arange(self.n_near, dtype=jnp.int32)[:, None, None]
        r = jnp.arange(self.tq, dtype=jnp.int32)[None, :, None]
        c = jnp.arange(KEY_BLOCK, dtype=jnp.int32)[None, None, :]
        rel = self.rel0 + KEY_BLOCK * j + c - r
        return _t5_bucket(lax.optimization_barrier(rel))


def _chunk_of(pos):
    return jnp.where(pos < N_META, 0, 1 + ((pos - N_META) >> CHUNK_SHIFT))


def _for_blocks(n, fn, group=2):
    def body(kg, carry):
        for r in range(group):
            fn(group * kg + r)
        return carry

    if isinstance(n, int):
        lax.fori_loop(0, n // group, body, 0)
        for kb in range(n - n % group, n):
            fn(kb)
        return
    lax.fori_loop(0, n >> (group.bit_length() - 1), body, 0)
    part = group // 2
    while part:
        start = n & ~(2 * part - 1)

        @pl.when((n & part) != 0)
        def _(start=start, part=part):
            for r in range(part):
                fn(start + r)

        part //= 2


def _attn_kernel(q_ref, qi_ref, w_ref, za_ref, k_ref, v_ref, kx_ref, ke_ref, ve_ref, kxe_ref,
                 bias_ref, tri_ref, o_ref,
                 qm_ref, qim_ref, wb_ref, keys_ref, madd_ref, s_ref, mx_ref, l_ref, acc_ref, ties_ref,
                 hi_ref, lo_ref, sel_ref, *, cfg):
    tq = cfg.tq
    i = pl.program_id(1)
    near0 = cfg.near_base + cfg.near_step * i if cfg.near_step else cfg.near_base
    nkb = near0 + cfg.n_near
    lane = lax.broadcasted_iota(jnp.int32, (tq, LANE), 1)
    row = lax.broadcasted_iota(jnp.int32, (tq, LANE), 0)
    low_half = lane < HEAD_DIM
    q_pos_base = cfg.q_pos0 + i * tq
    q_chunk = _chunk_of(q_pos_base + row)

    def admissible(blk):
        kpos = blk * LANE + lane - cfg.lane_off
        return (kpos >= 0) & (kpos < cfg.n_pos) & (_chunk_of(kpos) <= q_chunk)

    def key_operand(main_ref, edge_ref, fs, kb):
        def main(idx):
            ks = pl.ds(pl.multiple_of(idx * KEY_BLOCK, KEY_BLOCK), KEY_BLOCK)
            return (main_ref[fs, ks] if cfg.dims_major else main_ref[ks, fs]).astype(BF16)

        edge = edge_ref[fs, :] if cfg.dims_major else edge_ref[:, fs]
        if isinstance(kb, int):
            return edge if kb == cfg.edge_block else main(kb - cfg.main_shift)
        if cfg.n_main == 0:
            return edge
        return jnp.where(kb == cfg.edge_block, edge, main(jnp.clip(kb - cfg.main_shift, 0, cfg.n_main - 1)))

    def dot_keys(lhs, rhs):
        if cfg.dims_major:
            return jnp.dot(lhs, rhs, preferred_element_type=F32)
        return lax.dot_general(lhs, rhs, NT_DIMS, preferred_element_type=F32)

    def dot_values(lhs, rhs):
        if cfg.dims_major:
            return lax.dot_general(lhs, rhs, NT_DIMS, preferred_element_type=F32)
        return jnp.dot(lhs, rhs, preferred_element_type=F32)

    all_feat = slice(0, LANE)
    pair_feat = lambda p: slice(p * LANE, (p + 1) * LANE)

    zero = jnp.zeros((tq, LANE), BF16)
    for p in range(N_PAIRS):
        sl = slice(p * LANE, (p + 1) * LANE)
        qp = q_ref[:, sl]
        qm_ref[p, 0:tq, :] = jnp.where(low_half, qp, zero)
        qm_ref[p, tq:2 * tq, :] = jnp.where(low_half, zero, qp)
        qip = qi_ref[:, sl]
        qim_ref[p, 0:tq, :] = jnp.where(low_half, qip, zero)
        qim_ref[p, tq:2 * tq, :] = jnp.where(low_half, zero, qip)
    w_scale = (IDX_HEADS ** -0.5) * (IDX_DIM ** -0.5)
    for h in range(IDX_HEADS):
        col = IDX_DIM + h
        wb_ref[h] = jnp.broadcast_to(w_ref[:, col:col + 1] * w_scale, (tq, LANE))

    def index_block(kb):
        kx = key_operand(kx_ref, kxe_ref, all_feat, kb)
        acc = [jnp.zeros((tq, LANE), F32) for _ in range(HALF)]
        for p in range(N_PAIRS):
            d = jnp.maximum(dot_keys(qim_ref[p], kx), 0.0)
            for hf in range(HALF):
                cs = slice(hf * LANE, (hf + 1) * LANE)
                acc[hf] = acc[hf] + wb_ref[2 * p] * d[0:tq, cs] + wb_ref[2 * p + 1] * d[tq:2 * tq, cs]
        for hf in range(HALF):
            blk = kb * HALF + hf
            sc = jnp.where(admissible(blk), acc[hf] + 0.0, NEG)
            bits = pltpu.bitcast(sc.T, jnp.int32)
            key = jnp.where(bits < 0, bits ^ jnp.int32(0x7FFFFFFF), bits)
            rows = pl.ds(pl.multiple_of(blk * LANE, LANE), LANE)
            keys_ref[rows, :] = key
            hi_ref[rows, :] = (key >> HALF_BITS).astype(jnp.int16)
            lo_ref[rows, :] = ((key & LOW_MASK) - HALF_BIAS).astype(jnp.int16)

    _for_blocks(nkb, index_block)

    kf = jnp.float32(cfg.top_k)
    n_part = 4
    packed_rows = 2 * SUBLANE

    def count16(ref, pred):
        def body(c_i, parts):
            chunk = ref[pl.ds(pl.multiple_of(c_i * LANE, LANE), LANE), :]
            parts = list(parts)
            for j in range(LANE // packed_rows):
                kk = chunk[j * packed_rows:(j + 1) * packed_rows, :]
                parts[j % n_part] = parts[j % n_part] + jnp.where(pred(kk), jnp.int16(1), jnp.int16(0))
            return tuple(parts)
        parts = lax.fori_loop(0, nkb * HALF, body,
                              tuple(jnp.zeros((packed_rows, tq), jnp.int16) for _ in range(n_part)))
        tot = ((parts[0] + parts[1]) + (parts[2] + parts[3])).astype(jnp.int32)
        return jnp.broadcast_to(jnp.sum(tot, axis=0, keepdims=True), (SUBLANE, tq)).astype(F32)

    def pack16(v):
        return jnp.concatenate([v, v], axis=0).astype(jnp.int16)

    def bisect16(ref, target):
        def step(it, lo):
            cand = lo + lax.shift_left(jnp.int32(1), HALF_BITS - 1 - it)
            cand16 = pack16(cand)
            return jnp.where(count16(ref, lambda kk: kk >= cand16) >= target, cand, lo)
        return lax.fori_loop(0, HALF_BITS, step, jnp.full((SUBLANE, tq), -HALF_BIAS, jnp.int32))

    hi_thr = bisect16(hi_ref, kf)
    hi_thr16 = pack16(hi_thr)
    above = count16(hi_ref, lambda kk: kk > hi_thr16)

    def low_keys(c_i, carry):
        rows = pl.ds(pl.multiple_of(c_i * LANE, LANE), LANE)
        hi, lo = hi_ref[rows, :], lo_ref[rows, :]
        for j in range(LANE // packed_rows):
            rs = slice(j * packed_rows, (j + 1) * packed_rows)
            sel_ref[pl.ds(pl.multiple_of(c_i * LANE, LANE) + j * packed_rows, packed_rows), :] = jnp.where(
                hi[rs, :] == hi_thr16, lo[rs, :], jnp.int16(-HALF_BIAS))
        return carry

    lax.fori_loop(0, nkb * HALF, low_keys, 0)
    lo_thr = bisect16(sel_ref, kf - above)
    thr8 = (hi_thr << HALF_BITS) | ((lo_thr + HALF_BIAS) & LOW_MASK)

    def count(pred):
        def body(c_i, parts):
            chunk = keys_ref[pl.ds(pl.multiple_of(c_i * LANE, LANE), LANE), :]
            parts = list(parts)
            for j in range(LANE // SUBLANE):
                kk = chunk[j * SUBLANE:(j + 1) * SUBLANE, :]
                parts[j % n_part] = parts[j % n_part] + jnp.where(pred(kk), 1.0, 0.0)
            return tuple(parts)
        parts = lax.fori_loop(0, nkb * HALF, body, tuple(jnp.zeros((SUBLANE, tq), F32) for _ in range(n_part)))
        tot = (parts[0] + parts[1]) + (parts[2] + parts[3])
        return jnp.broadcast_to(jnp.sum(tot, axis=0, keepdims=True), (SUBLANE, tq))

    need8 = kf - count(lambda kk: kk > thr8)
    thr = thr8[0:1, :]
    need = need8[0:1, :]

    key_row = lax.broadcasted_iota(jnp.int32, (KEY_BLOCK, tq), 0)
    q_chunk_t = _chunk_of(q_pos_base + lax.broadcasted_iota(jnp.int32, (KEY_BLOCK, tq), 1))

    ties_ref[...] = jnp.zeros((SUBLANE, tq), F32)

    def mask_block(kb):
        r0 = pl.multiple_of(kb * KEY_BLOCK, KEY_BLOCK)
        kk = keys_ref[pl.ds(r0, KEY_BLOCK), :]
        eq = kk == thr
        eqf = jnp.where(eq, 1.0, 0.0)
        seen = ties_ref[...]
        rank = seen[0:1, :] + jnp.dot(tri_ref[...], eqf.astype(BF16), preferred_element_type=F32)
        sel = jnp.where(eq, jnp.where(rank <= need, 0.0, NEG), jnp.where(kk > thr, 0.0, NEG))
        kpos = r0 + key_row - cfg.lane_off
        ok = (kpos >= 0) & (kpos < cfg.n_pos) & (_chunk_of(kpos) <= q_chunk_t)
        madd_ref[kb] = jnp.where(ok, sel, NEG).T
        ties_ref[...] = seen + jnp.sum(eqf, axis=0, keepdims=True)

    _for_blocks(nkb, mask_block)

    def logits(g, p, kb):
        s = dot_keys(qm_ref[p], key_operand(k_ref, ke_ref, pair_feat(p), kb))
        ma = madd_ref[kb]
        tile = kb - near0 + 1
        tile = max(tile, 0) if isinstance(tile, int) else jnp.maximum(tile, 0)
        s = s + jnp.concatenate([ma + bias_ref[2 * p, tile], ma + bias_ref[2 * p + 1, tile]], axis=0)
        s_ref[g, kb] = s
        mx_ref[g] = jnp.maximum(mx_ref[g], jnp.maximum(s[:, 0:LANE], s[:, LANE:KEY_BLOCK]))

    def weighted_values(g, p, kb):
        m = mx_ref[g]
        pe = jnp.exp(s_ref[g, kb] - jnp.concatenate([m] * HALF, axis=1))
        l_ref[g] = l_ref[g] + (pe[:, 0:LANE] + pe[:, LANE:KEY_BLOCK])
        acc_ref[g] = acc_ref[g] + dot_values(pe.astype(BF16), key_operand(v_ref, ve_ref, pair_feat(p), kb))

    for stage in range(N_PAIRS + 1):
        p_a = stage if stage < N_PAIRS else None
        p_b = stage - 1 if stage > 0 else None

        def both(kb, p_a=p_a, p_b=p_b):
            if p_a is not None:
                logits(p_a % 2, p_a, kb)
            if p_b is not None:
                weighted_values(p_b % 2, p_b, kb)

        if p_a is not None:
            mx_ref[p_a % 2] = jnp.full((2 * tq, LANE), M_INIT, F32)

        _for_blocks(nkb, both, group=4)

        if p_a is not None:
            g = p_a % 2
            mx_ref[g] = jnp.broadcast_to(jnp.max(mx_ref[g], axis=1, keepdims=True), (2 * tq, LANE))
            l_ref[g] = jnp.zeros((2 * tq, LANE), F32)
            acc_ref[g] = jnp.zeros((2 * tq, LANE), F32)
        if p_b is not None:
            g = p_b % 2
            o = acc_ref[g] / jnp.sum(l_ref[g], axis=1, keepdims=True)
            o = jnp.where(low_half, o[0:tq], o[tq:2 * tq])
            z = za_ref[:, p_b * LANE:(p_b + 1) * LANE].astype(F32)
            o_ref[:, p_b * LANE:(p_b + 1) * LANE] = (o * _silu(z)).astype(BF16)


def _attention(cfg, q, qi, w, za, mains, edges, bias, tri, main_lead=()):
    nb, t, _ = q.shape
    tq, s_pad = cfg.tq, cfg.s_pad
    assert t == tq * cfg.n_qt
    qspec = lambda width: pl.BlockSpec((None, tq, width), lambda b, i: (b, i, 0))

    def kspec(a, lead):
        per_batch = a.shape[len(lead)] == nb
        assert per_batch or a.shape[len(lead)] == 1
        return pl.BlockSpec((None,) * (len(lead) + 1) + a.shape[-2:],
                            lambda b, i: lead + ((b if per_batch else 0), 0, 0),
                            pipeline_mode=pl.Buffered(1))

    key_axis = -1 if cfg.dims_major else -2
    for a in mains:
        assert a.shape[key_axis] >= cfg.n_main * KEY_BLOCK
    for a in edges:
        assert a.shape[key_axis] == KEY_BLOCK
    nkb_max = s_pad // KEY_BLOCK
    return pl.pallas_call(
        functools.partial(_attn_kernel, cfg=cfg),
        grid=(nb, cfg.n_qt),
        in_specs=[
            qspec(D_ATT), qspec(D_ATT), qspec(LANE), qspec(D_ATT),
            *[kspec(a, tuple(main_lead)) for a in mains],
            *[kspec(a, ()) for a in edges],
            pl.BlockSpec((N_HEADS, 1 + cfg.n_near, tq, KEY_BLOCK), lambda b, i: (0, 0, 0, 0),
                         pipeline_mode=pl.Buffered(1)),
            pl.BlockSpec((KEY_BLOCK, KEY_BLOCK), lambda b, i: (0, 0)),
        ],
        out_specs=qspec(D_ATT),
        out_shape=jax.ShapeDtypeStruct((nb, t, D_ATT), BF16),
        scratch_shapes=[
            pltpu.VMEM((N_PAIRS, 2 * tq, LANE), BF16),
            pltpu.VMEM((N_PAIRS, 2 * tq, LANE), BF16),
            pltpu.VMEM((IDX_HEADS, tq, LANE), F32),
            pltpu.VMEM((s_pad, tq), jnp.int32),
            pltpu.VMEM((nkb_max, tq, KEY_BLOCK), F32),
            pltpu.VMEM((PAIR_BUFFERS, nkb_max, 2 * tq, KEY_BLOCK), F32),
            pltpu.VMEM((PAIR_BUFFERS, 2 * tq, LANE), F32),
            pltpu.VMEM((PAIR_BUFFERS, 2 * tq, LANE), F32),
            pltpu.VMEM((PAIR_BUFFERS, 2 * tq, LANE), F32),
            pltpu.VMEM((SUBLANE, tq), F32),
            pltpu.VMEM((s_pad, tq), jnp.int16),
            pltpu.VMEM((s_pad, tq), jnp.int16),
            pltpu.VMEM((s_pad, tq), jnp.int16),
        ],
        compiler_params=pltpu.CompilerParams(
            dimension_semantics=("arbitrary", "arbitrary"), vmem_limit_bytes=VMEM_LIMIT),
        name="attention",
    )(q, qi, w, za, *mains, *edges, bias, tri)


def _edge_block(cfg, parts, lead_zeros):
    key_axis = 2 if cfg.dims_major else 1
    parts = [a.astype(BF16) for a in parts]

    def zeros(n):
        shape = list(parts[0].shape)
        shape[key_axis] = n
        return [jnp.zeros(shape, BF16)] if n else []

    n_end = KEY_BLOCK - lead_zeros - sum(a.shape[key_axis] for a in parts)
    return jnp.concatenate(zeros(lead_zeros) + parts + zeros(n_end), axis=key_axis)


def _attend(cfg, q, qi, tail, za, mains, edges, bias, tri, main_lead=()):
    rows = lambda a: jnp.pad(a, ((0, 0), (0, cfg.tq * cfg.n_qt - a.shape[1]), (0, 0)))
    out = _attention(cfg, rows(q), rows(qi), rows(tail), rows(za), mains, edges, bias, tri, main_lead)
    return out[:, :q.shape[1]]


def _forward(x_prompt, x_sample, cache_k, cache_v, cache_kidx, state_conv, meta_tokens,
             norm_g, w_in, conv_w, conv_b, conv_ln_g, conv_ln_b, w_out, rel_bias, final_g,
             *, tq_frames=256, tm_frames=512):
    depth = w_in.shape[0]
    bp, seq, d = x_prompt.shape
    bs, dec, _ = x_sample.shape
    past = cache_k.shape[2] - N_META
    assert seq % tq_frames == 0 and seq % CHUNK == 0 and (bp * seq) % tm_frames == 0

    n_qt = seq // tq_frames
    cfg_f = _AttnCfg(tq=tq_frames, n_rows=tq_frames, n_qt=n_qt, s_pad=KEY_BLOCK + seq,
                     lane_off=KEY_BLOCK - N_META, n_pos=N_META + seq, q_pos0=N_META,
                     near_step=tq_frames // KEY_BLOCK, near_base=0,
                     n_near=tq_frames // KEY_BLOCK + 1, top_k=min(TOPK_MAX, seq // 4),
                     edge_block=0, main_shift=1, n_main=seq // KEY_BLOCK)
    n_cache = N_META + past
    n_pos_s = n_cache + dec
    s_pad_s = -(-n_pos_s // KEY_BLOCK) * KEY_BLOCK
    last_s = s_pad_s // KEY_BLOCK - 1
    near_s = max(0, (n_cache - MAX_DISTANCE) // KEY_BLOCK)
    cfg_s = _AttnCfg(tq=LANE, n_rows=dec, n_qt=1, s_pad=s_pad_s, lane_off=0,
                     n_pos=n_pos_s, q_pos0=n_cache, near_step=0, near_base=near_s,
                     n_near=last_s - near_s + 1, top_k=min(TOPK_MAX, (past + dec) // 4),
                     edge_block=last_s, main_shift=0, n_main=last_s, dims_major=True)
    cfg_m = _AttnCfg(tq=LANE, n_rows=N_META, n_qt=1, s_pad=KEY_BLOCK, lane_off=KEY_BLOCK - N_META,
                     n_pos=N_META, q_pos0=0, near_step=0, near_base=0, n_near=1,
                     top_k=min(TOPK_MAX, seq // 4), edge_block=0, main_shift=1, n_main=0)
    assert dec <= LANE and n_cache >= last_s * KEY_BLOCK

    bias_f = _bias_tiles(cfg_f.bucket_table(), rel_bias)
    bias_s = _bias_tiles(cfg_s.bucket_table(), rel_bias)
    bias_m = _bias_tiles(cfg_m.bucket_table(), rel_bias)
    tri = (jnp.arange(KEY_BLOCK)[None, :] <= jnp.arange(KEY_BLOCK)[:, None]).astype(BF16)

    n_aux = bs * dec + N_META
    hf = x_prompt.reshape(bp * seq, d)
    haux = jnp.concatenate([x_sample.reshape(bs * dec, d), meta_tokens.astype(x_sample.dtype)], axis=0)

    w_all = w_in.astype(BF16)
    wo_all = w_out.astype(BF16)
    feat_major = lambda c: jnp.transpose(c, (0, 1, 3, 4, 2)).reshape(depth, bs, D_ATT, n_cache)
    ck_t, cv_t = feat_major(cache_k), feat_major(cache_v)
    cx_t = jnp.swapaxes(cache_kidx, 2, 3).astype(BF16)
    cx_t = jnp.concatenate([cx_t, cx_t], axis=2)
    outs = {name: [] for name in ("kp", "vp", "kip", "cp", "ks", "vs", "kis", "cs")}
    for l in range(depth):
        w_tail = jnp.pad(w_all[l, :, N_GROUPS * GROUP:], ((0, 0), (0, LANE - IDX_DIM - IDX_HEADS)))
        g = norm_g[l][None, :]
        cw = jnp.repeat(conv_w[l], SUBLANE, axis=0)
        cb, lg, lb = conv_b[l][None, :], conv_ln_g[l][None, :], conv_ln_b[l][None, :]
        final = l == depth - 1
        fg = final_g[None, :]

        uA, zcA, qA, kA, vA, zaA, qiA, tailA, kbA, vbA = _inproj(haux, g, w_all, l, w_tail, n_aux)
        uF, zcF, qF, kF, vF, zaF, qiF, tailF, kbF, vbF = _inproj(hf, g, w_all, l, w_tail, tm_frames)
        ns = bs * dec
        split = lambda a: (a[:ns].reshape(bs, dec, -1), a[ns:][None])
        uS, uM = split(uA); zcS, zcM = split(zcA); qS, qM = split(qA); kS, kM = split(kA)
        vS, vM = split(vA); zaS, zaM = split(zaA); qiS, qiM = split(qiA); tailS, tailM = split(tailA)
        kbS, kbM = split(kbA); vbS, vbM = split(vbA)
        b3 = lambda a: a.reshape(bp, seq, -1)
        uF, zcF, qF, kF, vF, zaF, qiF, tailF, kbF, vbF = map(
            b3, (uF, zcF, qF, kF, vF, zaF, qiF, tailF, kbF, vbF))

        zero_hist = jnp.zeros((1, HIST_ROWS, D_CONV), F32)
        cM = _conv(uM, zero_hist, zcM, cw, cb, lg, lb)
        hist_f = jnp.concatenate([jnp.zeros((1, HIST_ROWS - N_META, D_CONV), F32), uM], axis=1)
        cF = _conv(uF, hist_f, zcF, cw, cb, lg, lb)
        hist_s = jnp.pad(state_conv[l].astype(F32), ((0, 0), (HIST_PAD, 0), (0, 0)))
        cS = _conv(uS, hist_s, zcS, cw, cb, lg, lb)

        kidx = lambda tail: tail[..., :IDX_DIM]
        twice = lambda a: jnp.concatenate([a, a], axis=-1).astype(BF16)
        meta_edges = tuple(_edge_block(cfg_f, [a], cfg_f.lane_off) for a in (kbM, vbM, twice(kidx(tailM))))
        aM = _attend(cfg_m, qM, qiM, tailM, zaM, meta_edges, meta_edges, bias_m, tri)
        aF = _attend(cfg_f, qF, qiF, tailF, zaF, (kbF, vbF, twice(kidx(tailF))), meta_edges, bias_f, tri)
        tr = lambda a: jnp.swapaxes(a, 1, 2)
        lo = cfg_s.edge_block * KEY_BLOCK
        sample_edges = tuple(_edge_block(cfg_s, [c[l, :, :, lo:], tr(new)], 0)
                             for c, new in ((ck_t, kbS), (cv_t, vbS), (cx_t, twice(kidx(tailS)))))
        aS = _attend(cfg_s, qS, qiS, tailS, zaS, (ck_t, cv_t, cx_t), sample_edges, bias_s, tri, main_lead=(l,))

        cAux = jnp.concatenate([cS.reshape(ns, D_CONV), cM[0]], axis=0)
        aAux = jnp.concatenate([aS.reshape(ns, D_ATT), aM[0]], axis=0)
        haux = _outproj(cAux, aAux, haux, wo_all, l, fg, n_aux, final)
        hf = _outproj(cF.reshape(bp * seq, D_CONV), aF.reshape(bp * seq, D_ATT), hf, wo_all, l, fg,
                      tm_frames, final)

        with_meta = lambda m, f: jnp.concatenate([jnp.broadcast_to(m, (bp,) + m.shape[1:]), f], axis=1)
        outs["kp"].append(with_meta(kM, kF).reshape(bp, N_META + seq, N_HEADS, HEAD_DIM))
        outs["vp"].append(with_meta(vM, vF).reshape(bp, N_META + seq, N_HEADS, HEAD_DIM))
        outs["kip"].append(with_meta(kidx(tailM), kidx(tailF)))
        u_ext_p = jnp.concatenate([jnp.zeros((bp, CONV_WIDTH - 1, D_CONV), F32),
                                   jnp.broadcast_to(uM, (bp, N_META, D_CONV)), uF], axis=1)
        outs["cp"].append(u_ext_p[:, -(CONV_WIDTH - 1):])
        outs["ks"].append(kS.reshape(bs, dec, N_HEADS, HEAD_DIM))
        outs["vs"].append(vS.reshape(bs, dec, N_HEADS, HEAD_DIM))
        outs["kis"].append(kidx(tailS))
        u_ext_s = jnp.concatenate([state_conv[l].astype(F32), uS], axis=1)
        outs["cs"].append(u_ext_s[:, -(CONV_WIDTH - 1):])

    y_prompt = hf.reshape(bp, seq, d)
    y_sample = haux[:bs * dec].reshape(bs, dec, d)
    st = lambda name: jnp.stack(outs[name])
    return (y_prompt, y_sample, st("kp"), st("vp"), st("kip"), st("cp"),
            st("ks"), st("vs"), st("kis"), st("cs"))


def kernel(x_prompt, x_sample, cache_k, cache_v, cache_kidx, state_conv, meta_tokens, norm_g, w_in,
           conv_w, conv_b, conv_ln_g, conv_ln_b, w_out, rel_bias, final_g):
    return _forward(x_prompt, x_sample, cache_k, cache_v, cache_kidx, state_conv, meta_tokens,
                    norm_g, w_in, conv_w, conv_b, conv_ln_g, conv_ln_b, w_out, rel_bias, final_g)
```

```python
import functools
import math

import jax
import jax.numpy as jnp
from jax import lax
from jax.experimental import pallas as pl
from jax.experimental.pallas import tpu as pltpu

N_META = 16
CHUNK = 64
CHUNK_SHIFT = 6
D_CONV = 1024
CONV_WIDTH = 31
N_HEADS = 16
HEAD_DIM = 64
D_ATT = N_HEADS * HEAD_DIM
IDX_HEADS = 16
IDX_DIM = 64
TOPK_MAX = 256
N_BUCKETS = 32
MAX_DISTANCE = 128
NORM_EPS = 1e-6
NEG = -1e30

LANE = 128
SUBLANE = 8
KEY_BLOCK = 256
HALF = KEY_BLOCK // LANE
N_PAIRS = N_HEADS // 2
PAIR_BUFFERS = 2
HIST_ROWS = 32
HIST_PAD = HIST_ROWS - (CONV_WIDTH - 1)
GROUP = 1024
N_GROUPS = 8
VMEM_LIMIT = 56 * 1024 * 1024
INT_MIN = -2147483648
HALF_BITS = 16
LOW_MASK = (1 << HALF_BITS) - 1
HALF_BIAS = 1 << (HALF_BITS - 1)
M_INIT = -1e29
FAR_BUCKET = N_BUCKETS // 2 - 1

F32 = jnp.float32
BF16 = jnp.bfloat16
NT_DIMS = (((1,), (1,)), ((), ()))


def _silu(x):
    return x * jax.nn.sigmoid(x)


def _inproj_kernel(x_ref, g_ref, wm_ref, wt_ref,
                   u_ref, zc_ref, q_ref, k_ref, v_ref, za_ref, qi_ref, tail_ref, kb_ref, vb_ref,
                   xn_ref, a_ref):
    n = pl.program_id(1)
    j = pl.program_id(2)

    @pl.when(n == 0)
    def _():
        x = x_ref[...]
        ms = jnp.mean(x * x, axis=-1, keepdims=True)
        xn_ref[j] = (x * lax.rsqrt(ms + NORM_EPS) * g_ref[...]).astype(BF16)

    def group():
        return jnp.dot(xn_ref[j], wm_ref[...], preferred_element_type=F32)

    @pl.when(n == 0)
    def _():
        a_ref[j] = group()

    @pl.when(n == 1)
    def _():
        u_ref[...] = a_ref[j] * jax.nn.sigmoid(group())

    @pl.when(n == 2)
    def _():
        zc_ref[...] = group().astype(BF16)

    @pl.when(n == 3)
    def _():
        q_ref[...] = (group() * (HEAD_DIM ** -0.5)).astype(BF16)

    @pl.when(n == 4)
    def _():
        y = group()
        k_ref[...] = y
        kb_ref[...] = y.astype(BF16)

    @pl.when(n == 5)
    def _():
        y = group()
        v_ref[...] = y
        vb_ref[...] = y.astype(BF16)

    @pl.when(n == 6)
    def _():
        za_ref[...] = group().astype(BF16)

    @pl.when(n == 7)
    def _():
        qi_ref[...] = group().astype(BF16)
        tail_ref[...] = jnp.dot(xn_ref[j], wt_ref[...], preferred_element_type=F32)


def _inproj(x2d, g, w_all, layer, w_tail, tm):
    m, d = x2d.shape
    assert m % tm == 0 and w_all.shape[2] >= N_GROUPS * GROUP
    n_tiles = m // tm
    share = 2 if n_tiles % 2 == 0 else 1
    last = n_tiles - 1
    row = lambda width, dtype: jax.ShapeDtypeStruct((m, width), dtype)
    out_shape = (row(GROUP, F32), row(GROUP, BF16), row(GROUP, BF16), row(GROUP, F32),
                 row(GROUP, F32), row(GROUP, BF16), row(GROUP, BF16), row(LANE, F32),
                 row(GROUP, BF16), row(GROUP, BF16))

    def tile_at(n_w):
        def index(i, n, j):
            first = i * share
            return jnp.where(n < n_w, jnp.maximum(first - 1, 0),
                             jnp.where(n == n_w, first + j, jnp.minimum(first + share - 1, last)))
        return index

    ospec = lambda width, n_w: pl.BlockSpec((tm, width), lambda i, n, j: (tile_at(n_w)(i, n, j), 0))
    return pl.pallas_call(
        _inproj_kernel,
        grid=(n_tiles // share, N_GROUPS, share),
        in_specs=[
            pl.BlockSpec((tm, d), lambda i, n, j: (tile_at(0)(i, n, j), 0)),
            pl.BlockSpec((1, d), lambda i, n, j: (0, 0)),
            pl.BlockSpec((None, d, GROUP), lambda i, n, j: (layer, 0, n)),
            pl.BlockSpec((d, LANE), lambda i, n, j: (0, 0)),
        ],
        out_specs=([ospec(GROUP, n_w) for n_w in range(1, N_GROUPS)] + [ospec(LANE, N_GROUPS - 1)]
                   + [ospec(GROUP, 4), ospec(GROUP, 5)]),
        out_shape=out_shape,
        scratch_shapes=[pltpu.VMEM((share, tm, d), BF16), pltpu.VMEM((share, tm, GROUP), F32)],
        compiler_params=pltpu.CompilerParams(
            dimension_semantics=("arbitrary", "arbitrary", "arbitrary"), vmem_limit_bytes=VMEM_LIMIT),
        name="inproj",
    )(x2d, g, w_all, w_tail)


def _conv_kernel(u_ref, hist_ref, zc_ref, cw_ref, cb_ref, lg_ref, lb_ref, o_ref, ext_ref, c_ref,
                 *, tt, rc):
    t = pl.program_id(1)
    n_hist = CONV_WIDTH - 1

    for b in range(SUBLANE):
        keep = n_hist - b

        @pl.when(t == 0)
        def _(b=b, keep=keep):
            ext_ref[b, 0:keep, :] = hist_ref[HIST_PAD + b:HIST_ROWS, :]

        @pl.when(t > 0)
        def _(b=b, keep=keep):
            ext_ref[b, 0:keep, :] = ext_ref[b, tt:tt + keep, :]

    for b in range(SUBLANE):
        ext_ref[b, n_hist - b:n_hist - b + tt, :] = u_ref[...]

    half_c = D_CONV // 2

    def taps(r, carry):
        r0 = pl.multiple_of(r * (2 * SUBLANE), 2 * SUBLANE)
        for c0 in (0, half_c):
            cs = slice(c0, c0 + half_c)
            acc = [jnp.broadcast_to(cb_ref[:, cs], (SUBLANE, half_c)) for _ in range(2)]
            for b in range(SUBLANE):
                n_a = (CONV_WIDTH - 1 - b) // SUBLANE + 1
                win = ext_ref[b, pl.ds(r0, (n_a + 1) * SUBLANE), cs]
                for a in range(n_a):
                    j = a * SUBLANE + b
                    w8 = cw_ref[j * SUBLANE:(j + 1) * SUBLANE, cs]
                    for g in range(2):
                        lo = (a + g) * SUBLANE
                        acc[g] = acc[g] + w8 * win[lo:lo + SUBLANE, :]
            for g in range(2):
                c_ref[pl.ds(r0 + g * SUBLANE, SUBLANE), cs] = acc[g]
        return carry

    lax.fori_loop(0, tt // (2 * SUBLANE), taps, 0)

    def norm_gate(r, carry):
        r0 = pl.multiple_of(r * rc, rc)
        c = c_ref[pl.ds(r0, rc), :]
        mu = jnp.mean(c, axis=-1, keepdims=True)
        xc = c - mu
        var = jnp.mean(xc * xc, axis=-1, keepdims=True)
        y = xc * lax.rsqrt(var + NORM_EPS) * lg_ref[...] + lb_ref[...]
        z = zc_ref[pl.ds(r0, rc), :].astype(F32)
        o_ref[pl.ds(r0, rc), :] = (_silu(y) * _silu(z)).astype(BF16)
        return carry

    lax.fori_loop(0, tt // rc, norm_gate, 0)


def _conv(u, hist, zc, cw, cb, lg, lb):
    nb, t, c = u.shape
    tt = min(t, 256)
    rc = min(tt, 64)
    assert t % tt == 0 and tt % rc == 0 and tt % (2 * SUBLANE) == 0 and c == D_CONV
    hist_map = (lambda b, i: (b, 0, 0)) if hist.shape[0] == nb else (lambda b, i: (0, 0, 0))
    vec = pl.BlockSpec((1, c), lambda b, i: (0, 0))
    return pl.pallas_call(
        functools.partial(_conv_kernel, tt=tt, rc=rc),
        grid=(nb, t // tt),
        in_specs=[
            pl.BlockSpec((None, tt, c), lambda b, i: (b, i, 0)),
            pl.BlockSpec((None, HIST_ROWS, c), hist_map),
            pl.BlockSpec((None, tt, c), lambda b, i: (b, i, 0)),
            pl.BlockSpec((CONV_WIDTH * SUBLANE, c), lambda b, i: (0, 0)),
            vec, vec, vec,
        ],
        out_specs=pl.BlockSpec((None, tt, c), lambda b, i: (b, i, 0)),
        out_shape=jax.ShapeDtypeStruct((nb, t, c), BF16),
        scratch_shapes=[pltpu.VMEM((SUBLANE, HIST_ROWS + tt, c), F32), pltpu.VMEM((tt, c), F32)],
        compiler_params=pltpu.CompilerParams(
            dimension_semantics=("arbitrary", "arbitrary"), vmem_limit_bytes=VMEM_LIMIT),
        name="conv",
    )(u, hist, zc, cw, cb, lg, lb)


def _outproj_kernel(c_ref, a_ref, h_ref, wc_ref, wa_ref, fg_ref, o_ref, *, final):
    y = (jnp.dot(c_ref[...], wc_ref[...], preferred_element_type=F32)
         + jnp.dot(a_ref[...], wa_ref[...], preferred_element_type=F32))
    y = h_ref[...] + y
    if final:
        ms = jnp.mean(y * y, axis=-1, keepdims=True)
        y = y * lax.rsqrt(ms + NORM_EPS) * fg_ref[...]
    o_ref[...] = y


def _outproj(c, a, h, wo_all, layer, fg, tm, final):
    m, d = h.shape
    assert m % tm == 0 and D_CONV == D_ATT
    half = lambda: pl.BlockSpec((tm, D_CONV), lambda i: (i, 0))
    full = lambda: pl.BlockSpec((tm, d), lambda i: (i, 0))
    wspec = lambda part: pl.BlockSpec((None, D_CONV, d), lambda i: (layer, part, 0))
    return pl.pallas_call(
        functools.partial(_outproj_kernel, final=final),
        grid=(m // tm,),
        in_specs=[half(), half(), full(), wspec(0), wspec(1), pl.BlockSpec((1, d), lambda i: (0, 0))],
        out_specs=full(),
        out_shape=jax.ShapeDtypeStruct((m, d), F32),
        compiler_params=pltpu.CompilerParams(
            dimension_semantics=("arbitrary",), vmem_limit_bytes=VMEM_LIMIT),
        name="outproj",
    )(c, a, h, wo_all, wo_all, fg)


def _bias_kernel(bucket_ref, rb_ref, o_ref, *, n_near, tq):
    h = pl.program_id(0)
    far = rb_ref[FAR_BUCKET, h]
    for j in range(n_near):
        def body(r, carry):
            r0 = pl.multiple_of(r * SUBLANE, SUBLANE)
            bk = bucket_ref[j, pl.ds(r0, SUBLANE), :]
            acc = jnp.zeros(bk.shape, F32)
            for b in range(N_BUCKETS):
                acc = jnp.where(bk == b, rb_ref[b, h], acc)
            o_ref[j, pl.ds(r0, SUBLANE), :] = acc - far
            return carry
        lax.fori_loop(0, tq // SUBLANE, body, 0)


def _bias_tiles(bucket, rel_bias):
    n_near, tq, kb = bucket.shape
    return pl.pallas_call(
        functools.partial(_bias_kernel, n_near=n_near, tq=tq),
        grid=(N_HEADS,),
        in_specs=[pl.BlockSpec((n_near, tq, kb), lambda h: (0, 0, 0)),
                  pl.BlockSpec(memory_space=pltpu.SMEM)],
        out_specs=pl.BlockSpec((None, n_near, tq, kb), lambda h: (h, 0, 0, 0)),
        out_shape=jax.ShapeDtypeStruct((N_HEADS, n_near, tq, kb), F32),
        compiler_params=pltpu.CompilerParams(dimension_semantics=("arbitrary",)),
        name="bias_tiles",
    )(bucket, rel_bias)


def _t5_bucket(rel):
    half = N_BUCKETS // 2
    max_exact = half // 2
    n = jnp.abs(rel)
    large = max_exact + (jnp.log(jnp.maximum(n, 1).astype(jnp.float32) / max_exact)
                         / math.log(MAX_DISTANCE / max_exact) * (half - max_exact)).astype(jnp.int32)
    large = jnp.minimum(large, half - 1)
    return jnp.where(rel > 0, half, 0) + jnp.where(n < max_exact, n, large)


class _AttnCfg:
    def __init__(self, tq, n_rows, n_qt, s_pad, lane_off, n_pos, q_pos0, near_step, near_base, n_near, top_k,
                 edge_block, main_shift, n_main, dims_major=False):
        self.dims_major = dims_major
        self.edge_block, self.main_shift, self.n_main = edge_block, main_shift, n_main
        assert all(kb == edge_block or 0 <= kb - main_shift < n_main for kb in range(s_pad // KEY_BLOCK))
        self.tq, self.n_rows, self.n_qt, self.s_pad, self.lane_off = tq, n_rows, n_qt, s_pad, lane_off
        self.n_pos, self.q_pos0 = n_pos, q_pos0
        self.near_step, self.near_base, self.n_near, self.top_k = near_step, near_base, n_near, top_k
        self.rel0 = near_base * KEY_BLOCK - lane_off - q_pos0
        assert s_pad % KEY_BLOCK == 0 and tq % LANE == 0 and n_rows <= tq and n_rows % (2 * SUBLANE) == 0
        assert n_rows == tq or n_qt == 1
        assert near_step * KEY_BLOCK == tq or n_qt == 1
        assert (near_base == 0 and near_step == 0) or 1 - self.rel0 >= MAX_DISTANCE
        assert (near_base + near_step * (n_qt - 1) + n_near) * KEY_BLOCK <= s_pad

    def bucket_table(self):
        j = jnp.arange(self.n_near, dtype=jnp.int32)[:, None, None]
        r = jnp.arange(self.n_rows, dtype=jnp.int32)[None, :, None]
        c = jnp.arange(KEY_BLOCK, dtype=jnp.int32)[None, None, :]
        rel = self.rel0 + KEY_BLOCK * j + c - r
        return _t5_bucket(lax.optimization_barrier(rel))


def _chunk_of(pos):
    return jnp.where(pos < N_META, 0, 1 + ((pos - N_META) >> CHUNK_SHIFT))


def _for_blocks(n, fn, group=2):
    def body(kg, carry):
        for r in range(group):
            fn(group * kg + r)
        return carry

    if isinstance(n, int):
        lax.fori_loop(0, n // group, body, 0)
        for kb in range(n - n % group, n):
            fn(kb)
        return
    lax.fori_loop(0, n >> (group.bit_length() - 1), body, 0)
    part = group // 2
    while part:
        start = n & ~(2 * part - 1)

        @pl.when((n & part) != 0)
        def _(start=start, part=part):
            for r in range(part):
                fn(start + r)

        part //= 2


def _attn_kernel(q_ref, qi_ref, w_ref, za_ref, k_ref, v_ref, kx_ref, ke_ref, ve_ref, kxe_ref,
                 bias_ref, tri_ref, o_ref,
                 qm_ref, qim_ref, wb_ref, keys_ref, madd_ref, s_ref, mx_ref, l_ref, acc_ref, ties_ref,
                 hi_ref, lo_ref, sel_ref, *, cfg):
    tq, rq = cfg.tq, cfg.n_rows
    i = pl.program_id(1)
    near0 = cfg.near_base + cfg.near_step * i if cfg.near_step else cfg.near_base
    nkb = near0 + cfg.n_near
    lane = lax.broadcasted_iota(jnp.int32, (rq, LANE), 1)
    row = lax.broadcasted_iota(jnp.int32, (rq, LANE), 0)
    low_half = lane < HEAD_DIM
    q_pos_base = cfg.q_pos0 + i * tq
    q_chunk = _chunk_of(q_pos_base + row)

    def admissible(blk):
        kpos = blk * LANE + lane - cfg.lane_off
        return (kpos >= 0) & (kpos < cfg.n_pos) & (_chunk_of(kpos) <= q_chunk)

    def key_operand(main_ref, edge_ref, fs, kb):
        def main(idx):
            ks = pl.ds(pl.multiple_of(idx * KEY_BLOCK, KEY_BLOCK), KEY_BLOCK)
            return (main_ref[fs, ks] if cfg.dims_major else main_ref[ks, fs]).astype(BF16)

        edge = edge_ref[fs, :] if cfg.dims_major else edge_ref[:, fs]
        if isinstance(kb, int):
            return edge if kb == cfg.edge_block else main(kb - cfg.main_shift)
        if cfg.n_main == 0:
            return edge
        return jnp.where(kb == cfg.edge_block, edge, main(jnp.clip(kb - cfg.main_shift, 0, cfg.n_main - 1)))

    def dot_keys(lhs, rhs):
        if cfg.dims_major:
            return jnp.dot(lhs, rhs, preferred_element_type=F32)
        return lax.dot_general(lhs, rhs, NT_DIMS, preferred_element_type=F32)

    def dot_values(lhs, rhs):
        if cfg.dims_major:
            return lax.dot_general(lhs, rhs, NT_DIMS, preferred_element_type=F32)
        return jnp.dot(lhs, rhs, preferred_element_type=F32)

    all_feat = slice(0, LANE)
    pair_feat = lambda p: slice(p * LANE, (p + 1) * LANE)

    zero = jnp.zeros((rq, LANE), BF16)
    for p in range(N_PAIRS):
        sl = slice(p * LANE, (p + 1) * LANE)
        qp = q_ref[:, sl]
        qm_ref[p, 0:rq, :] = jnp.where(low_half, qp, zero)
        qm_ref[p, rq:2 * rq, :] = jnp.where(low_half, zero, qp)
        qip = qi_ref[:, sl]
        qim_ref[p, 0:rq, :] = jnp.where(low_half, qip, zero)
        qim_ref[p, rq:2 * rq, :] = jnp.where(low_half, zero, qip)
    w_scale = (IDX_HEADS ** -0.5) * (IDX_DIM ** -0.5)
    for h in range(IDX_HEADS):
        col = IDX_DIM + h
        wb_ref[h] = jnp.broadcast_to(w_ref[:, col:col + 1] * w_scale, (rq, LANE))

    no_query = jnp.full((tq - rq, LANE), NEG, F32) if rq < tq else None

    def index_block(kb):
        kx = key_operand(kx_ref, kxe_ref, all_feat, kb)
        acc = [jnp.zeros((rq, LANE), F32) for _ in range(HALF)]
        for p in range(N_PAIRS):
            d = jnp.maximum(dot_keys(qim_ref[p], kx), 0.0)
            for hf in range(HALF):
                cs = slice(hf * LANE, (hf + 1) * LANE)
                acc[hf] = acc[hf] + wb_ref[2 * p] * d[0:rq, cs] + wb_ref[2 * p + 1] * d[rq:2 * rq, cs]
        for hf in range(HALF):
            blk = kb * HALF + hf
            sc = jnp.where(admissible(blk), acc[hf] + 0.0, NEG)
            if no_query is not None:
                sc = jnp.concatenate([sc, no_query], axis=0)
            bits = pltpu.bitcast(sc.T, jnp.int32)
            key = jnp.where(bits < 0, bits ^ jnp.int32(0x7FFFFFFF), bits)
            rows = pl.ds(pl.multiple_of(blk * LANE, LANE), LANE)
            keys_ref[rows, :] = key
            hi_ref[rows, :] = (key >> HALF_BITS).astype(jnp.int16)
            lo_ref[rows, :] = ((key & LOW_MASK) - HALF_BIAS).astype(jnp.int16)

    _for_blocks(nkb, index_block)

    kf = jnp.float32(cfg.top_k)
    n_part = 4
    packed_rows = 2 * SUBLANE

    def count16(ref, pred):
        def body(c_i, parts):
            chunk = ref[pl.ds(pl.multiple_of(c_i * LANE, LANE), LANE), :]
            parts = list(parts)
            for j in range(LANE // packed_rows):
                kk = chunk[j * packed_rows:(j + 1) * packed_rows, :]
                parts[j % n_part] = parts[j % n_part] + jnp.where(pred(kk), jnp.int16(1), jnp.int16(0))
            return tuple(parts)
        parts = lax.fori_loop(0, nkb * HALF, body,
                              tuple(jnp.zeros((packed_rows, tq), jnp.int16) for _ in range(n_part)))
        tot = ((parts[0] + parts[1]) + (parts[2] + parts[3])).astype(jnp.int32)
        return jnp.broadcast_to(jnp.sum(tot, axis=0, keepdims=True), (SUBLANE, tq)).astype(F32)

    def pack16(v):
        return jnp.concatenate([v, v], axis=0).astype(jnp.int16)

    def bisect16(ref, target):
        def step(it, lo):
            cand = lo + lax.shift_left(jnp.int32(1), HALF_BITS - 1 - it)
            cand16 = pack16(cand)
            return jnp.where(count16(ref, lambda kk: kk >= cand16) >= target, cand, lo)
        return lax.fori_loop(0, HALF_BITS, step, jnp.full((SUBLANE, tq), -HALF_BIAS, jnp.int32))

    hi_thr = bisect16(hi_ref, kf)
    hi_thr16 = pack16(hi_thr)
    above = count16(hi_ref, lambda kk: kk > hi_thr16)

    def low_keys(c_i, carry):
        rows = pl.ds(pl.multiple_of(c_i * LANE, LANE), LANE)
        hi, lo = hi_ref[rows, :], lo_ref[rows, :]
        for j in range(LANE // packed_rows):
            rs = slice(j * packed_rows, (j + 1) * packed_rows)
            sel_ref[pl.ds(pl.multiple_of(c_i * LANE, LANE) + j * packed_rows, packed_rows), :] = jnp.where(
                hi[rs, :] == hi_thr16, lo[rs, :], jnp.int16(-HALF_BIAS))
        return carry

    lax.fori_loop(0, nkb * HALF, low_keys, 0)
    lo_thr = bisect16(sel_ref, kf - above)
    thr8 = (hi_thr << HALF_BITS) | ((lo_thr + HALF_BIAS) & LOW_MASK)

    def count(pred):
        def body(c_i, parts):
            chunk = keys_ref[pl.ds(pl.multiple_of(c_i * LANE, LANE), LANE), :]
            parts = list(parts)
            for j in range(LANE // SUBLANE):
                kk = chunk[j * SUBLANE:(j + 1) * SUBLANE, :]
                parts[j % n_part] = parts[j % n_part] + jnp.where(pred(kk), 1.0, 0.0)
            return tuple(parts)
        parts = lax.fori_loop(0, nkb * HALF, body, tuple(jnp.zeros((SUBLANE, tq), F32) for _ in range(n_part)))
        tot = (parts[0] + parts[1]) + (parts[2] + parts[3])
        return jnp.broadcast_to(jnp.sum(tot, axis=0, keepdims=True), (SUBLANE, tq))

    need8 = kf - count(lambda kk: kk > thr8)
    thr = thr8[0:1, :]
    need = need8[0:1, :]

    key_row = lax.broadcasted_iota(jnp.int32, (KEY_BLOCK, tq), 0)
    q_chunk_t = _chunk_of(q_pos_base + lax.broadcasted_iota(jnp.int32, (KEY_BLOCK, tq), 1))

    ties_ref[...] = jnp.zeros((SUBLANE, tq), F32)

    def mask_block(kb):
        r0 = pl.multiple_of(kb * KEY_BLOCK, KEY_BLOCK)
        kk = keys_ref[pl.ds(r0, KEY_BLOCK), :]
        eq = kk == thr
        eqf = jnp.where(eq, 1.0, 0.0)
        seen = ties_ref[...]
        rank = seen[0:1, :] + jnp.dot(tri_ref[...], eqf.astype(BF16), preferred_element_type=F32)
        sel = jnp.where(eq, jnp.where(rank <= need, 0.0, NEG), jnp.where(kk > thr, 0.0, NEG))
        kpos = r0 + key_row - cfg.lane_off
        ok = (kpos >= 0) & (kpos < cfg.n_pos) & (_chunk_of(kpos) <= q_chunk_t)
        madd_ref[kb] = jnp.where(ok, sel, NEG).T[0:rq]
        ties_ref[...] = seen + jnp.sum(eqf, axis=0, keepdims=True)

    _for_blocks(nkb, mask_block)

    def logits(g, p, kb, near_j):
        s = dot_keys(qm_ref[p], key_operand(k_ref, ke_ref, pair_feat(p), kb))
        ma = madd_ref[kb]
        if near_j is None:
            add = jnp.concatenate([ma, ma], axis=0)
        else:
            add = jnp.concatenate([ma + bias_ref[2 * p, near_j], ma + bias_ref[2 * p + 1, near_j]], axis=0)
        s = s + add
        s_ref[g, kb] = s
        mx_ref[g] = jnp.maximum(mx_ref[g], jnp.maximum(s[:, 0:LANE], s[:, LANE:KEY_BLOCK]))

    def weighted_values(g, p, kb):
        m = mx_ref[g]
        pe = jnp.exp(s_ref[g, kb] - jnp.concatenate([m] * HALF, axis=1))
        l_ref[g] = l_ref[g] + (pe[:, 0:LANE] + pe[:, LANE:KEY_BLOCK])
        acc_ref[g] = acc_ref[g] + dot_values(pe.astype(BF16), key_operand(v_ref, ve_ref, pair_feat(p), kb))

    for stage in range(N_PAIRS + 1):
        p_a = stage if stage < N_PAIRS else None
        p_b = stage - 1 if stage > 0 else None

        def both(kb, near_j, p_a=p_a, p_b=p_b):
            if p_a is not None:
                logits(p_a % 2, p_a, kb, near_j)
            if p_b is not None:
                weighted_values(p_b % 2, p_b, kb)

        if p_a is not None:
            mx_ref[p_a % 2] = jnp.full((2 * rq, LANE), M_INIT, F32)

        _for_blocks(near0, lambda kb, both=both: both(kb, None), group=4)
        for j in range(cfg.n_near):
            both(near0 + j, j)

        if p_a is not None:
            g = p_a % 2
            mx_ref[g] = jnp.broadcast_to(jnp.max(mx_ref[g], axis=1, keepdims=True), (2 * rq, LANE))
            l_ref[g] = jnp.zeros((2 * rq, LANE), F32)
            acc_ref[g] = jnp.zeros((2 * rq, LANE), F32)
        if p_b is not None:
            g = p_b % 2
            o = acc_ref[g] / jnp.sum(l_ref[g], axis=1, keepdims=True)
            o = jnp.where(low_half, o[0:rq], o[rq:2 * rq])
            z = za_ref[:, p_b * LANE:(p_b + 1) * LANE].astype(F32)
            o_ref[:, p_b * LANE:(p_b + 1) * LANE] = (o * _silu(z)).astype(BF16)


def _attention(cfg, q, qi, w, za, mains, edges, bias, tri, main_lead=()):
    nb, t, _ = q.shape
    tq, rq, s_pad = cfg.tq, cfg.n_rows, cfg.s_pad
    assert t == rq * cfg.n_qt
    qspec = lambda width: pl.BlockSpec((None, rq, width), lambda b, i: (b, i, 0))

    def kspec(a, lead, buffers):
        per_batch = a.shape[len(lead)] == nb
        assert per_batch or a.shape[len(lead)] == 1
        return pl.BlockSpec((None,) * (len(lead) + 1) + a.shape[-2:],
                            lambda b, i: lead + ((b if per_batch else 0), 0, 0),
                            pipeline_mode=pl.Buffered(buffers if per_batch and nb > 1 else 1))

    key_axis = -1 if cfg.dims_major else -2
    for a in mains:
        assert a.shape[key_axis] >= cfg.n_main * KEY_BLOCK
    for a in edges:
        assert a.shape[key_axis] == KEY_BLOCK
    nkb_max = s_pad // KEY_BLOCK
    return pl.pallas_call(
        functools.partial(_attn_kernel, cfg=cfg),
        grid=(nb, cfg.n_qt),
        in_specs=[
            qspec(D_ATT), qspec(D_ATT), qspec(LANE), qspec(D_ATT),
            *[kspec(a, tuple(main_lead), 2 if a.dtype == BF16 else 1) for a in mains],
            *[kspec(a, (), 1) for a in edges],
            pl.BlockSpec((N_HEADS, cfg.n_near, rq, KEY_BLOCK), lambda b, i: (0, 0, 0, 0),
                         pipeline_mode=pl.Buffered(1)),
            pl.BlockSpec((KEY_BLOCK, KEY_BLOCK), lambda b, i: (0, 0)),
        ],
        out_specs=qspec(D_ATT),
        out_shape=jax.ShapeDtypeStruct((nb, t, D_ATT), BF16),
        scratch_shapes=[
            pltpu.VMEM((N_PAIRS, 2 * rq, LANE), BF16),
            pltpu.VMEM((N_PAIRS, 2 * rq, LANE), BF16),
            pltpu.VMEM((IDX_HEADS, rq, LANE), F32),
            pltpu.VMEM((s_pad, tq), jnp.int32),
            pltpu.VMEM((nkb_max, rq, KEY_BLOCK), F32),
            pltpu.VMEM((PAIR_BUFFERS, nkb_max, 2 * rq, KEY_BLOCK), F32),
            pltpu.VMEM((PAIR_BUFFERS, 2 * rq, LANE), F32),
            pltpu.VMEM((PAIR_BUFFERS, 2 * rq, LANE), F32),
            pltpu.VMEM((PAIR_BUFFERS, 2 * rq, LANE), F32),
            pltpu.VMEM((SUBLANE, tq), F32),
            pltpu.VMEM((s_pad, tq), jnp.int16),
            pltpu.VMEM((s_pad, tq), jnp.int16),
            pltpu.VMEM((s_pad, tq), jnp.int16),
        ],
        compiler_params=pltpu.CompilerParams(
            dimension_semantics=("arbitrary", "arbitrary"), vmem_limit_bytes=VMEM_LIMIT),
        name="attention",
    )(q, qi, w, za, *mains, *edges, bias, tri)


def _edge_block(cfg, parts, lead_zeros):
    key_axis = 2 if cfg.dims_major else 1
    parts = [a.astype(BF16) for a in parts]

    def zeros(n):
        shape = list(parts[0].shape)
        shape[key_axis] = n
        return [jnp.zeros(shape, BF16)] if n else []

    n_end = KEY_BLOCK - lead_zeros - sum(a.shape[key_axis] for a in parts)
    return jnp.concatenate(zeros(lead_zeros) + parts + zeros(n_end), axis=key_axis)


def _attend(cfg, q, qi, tail, za, mains, edges, bias, tri, main_lead=()):
    rows = lambda a: jnp.pad(a, ((0, 0), (0, cfg.n_rows * cfg.n_qt - a.shape[1]), (0, 0)))
    out = _attention(cfg, rows(q), rows(qi), rows(tail), rows(za), mains, edges, bias, tri, main_lead)
    return out[:, :q.shape[1]]


def _forward(x_prompt, x_sample, cache_k, cache_v, cache_kidx, state_conv, meta_tokens,
             norm_g, w_in, conv_w, conv_b, conv_ln_g, conv_ln_b, w_out, rel_bias, final_g,
             *, tq_frames=256, tm_frames=512):
    depth = w_in.shape[0]
    bp, seq, d = x_prompt.shape
    bs, dec, _ = x_sample.shape
    past = cache_k.shape[2] - N_META
    assert seq % tq_frames == 0 and seq % CHUNK == 0 and (bp * seq) % tm_frames == 0

    n_qt = seq // tq_frames
    cfg_f = _AttnCfg(tq=tq_frames, n_rows=tq_frames, n_qt=n_qt, s_pad=KEY_BLOCK + seq,
                     lane_off=KEY_BLOCK - N_META, n_pos=N_META + seq, q_pos0=N_META,
                     near_step=tq_frames // KEY_BLOCK, near_base=0,
                     n_near=tq_frames // KEY_BLOCK + 1, top_k=min(TOPK_MAX, seq // 4),
                     edge_block=0, main_shift=1, n_main=seq // KEY_BLOCK)
    n_cache = N_META + past
    n_pos_s = n_cache + dec
    s_pad_s = -(-n_pos_s // KEY_BLOCK) * KEY_BLOCK
    last_s = s_pad_s // KEY_BLOCK - 1
    near_s = max(0, (n_cache - MAX_DISTANCE) // KEY_BLOCK)
    cfg_s = _AttnCfg(tq=LANE, n_rows=dec, n_qt=1, s_pad=s_pad_s, lane_off=0,
                     n_pos=n_pos_s, q_pos0=n_cache, near_step=0, near_base=near_s,
                     n_near=last_s - near_s + 1, top_k=min(TOPK_MAX, (past + dec) // 4),
                     edge_block=last_s, main_shift=0, n_main=last_s, dims_major=True)
    cfg_m = _AttnCfg(tq=LANE, n_rows=N_META, n_qt=1, s_pad=KEY_BLOCK, lane_off=KEY_BLOCK - N_META,
                     n_pos=N_META, q_pos0=0, near_step=0, near_base=0, n_near=1,
                     top_k=min(TOPK_MAX, seq // 4), edge_block=0, main_shift=1, n_main=0)
    assert dec <= LANE and n_cache >= last_s * KEY_BLOCK

    bias_f = _bias_tiles(cfg_f.bucket_table(), rel_bias)
    bias_s = _bias_tiles(cfg_s.bucket_table(), rel_bias)
    bias_m = _bias_tiles(cfg_m.bucket_table(), rel_bias)
    tri = (jnp.arange(KEY_BLOCK)[None, :] <= jnp.arange(KEY_BLOCK)[:, None]).astype(BF16)

    n_aux = bs * dec + N_META
    hf = x_prompt.reshape(bp * seq, d)
    haux = jnp.concatenate([x_sample.reshape(bs * dec, d), meta_tokens.astype(x_sample.dtype)], axis=0)

    w_all = w_in.astype(BF16)
    wo_all = w_out.astype(BF16)
    feat_major = lambda c: jnp.transpose(c, (0, 1, 3, 4, 2)).reshape(depth, bs, D_ATT, n_cache)
    ck_t, cv_t = feat_major(cache_k), feat_major(cache_v)
    cx_t = jnp.swapaxes(cache_kidx, 2, 3).astype(BF16)
    cx_t = jnp.concatenate([cx_t, cx_t], axis=2)
    outs = {name: [] for name in ("kp", "vp", "kip", "cp", "ks", "vs", "kis", "cs")}
    for l in range(depth):
        w_tail = jnp.pad(w_all[l, :, N_GROUPS * GROUP:], ((0, 0), (0, LANE - IDX_DIM - IDX_HEADS)))
        g = norm_g[l][None, :]
        cw = jnp.repeat(conv_w[l], SUBLANE, axis=0)
        cb, lg, lb = conv_b[l][None, :], conv_ln_g[l][None, :], conv_ln_b[l][None, :]
        final = l == depth - 1
        fg = final_g[None, :]

        uA, zcA, qA, kA, vA, zaA, qiA, tailA, kbA, vbA = _inproj(haux, g, w_all, l, w_tail, n_aux)
        uF, zcF, qF, kF, vF, zaF, qiF, tailF, kbF, vbF = _inproj(hf, g, w_all, l, w_tail, tm_frames)
        ns = bs * dec
        split = lambda a: (a[:ns].reshape(bs, dec, -1), a[ns:][None])
        uS, uM = split(uA); zcS, zcM = split(zcA); qS, qM = split(qA); kS, kM = split(kA)
        vS, vM = split(vA); zaS, zaM = split(zaA); qiS, qiM = split(qiA); tailS, tailM = split(tailA)
        kbS, kbM = split(kbA); vbS, vbM = split(vbA)
        b3 = lambda a: a.reshape(bp, seq, -1)
        uF, zcF, qF, kF, vF, zaF, qiF, tailF, kbF, vbF = map(
            b3, (uF, zcF, qF, kF, vF, zaF, qiF, tailF, kbF, vbF))

        zero_hist = jnp.zeros((1, HIST_ROWS, D_CONV), F32)
        cM = _conv(uM, zero_hist, zcM, cw, cb, lg, lb)
        hist_f = jnp.concatenate([jnp.zeros((1, HIST_ROWS - N_META, D_CONV), F32), uM], axis=1)
        cF = _conv(uF, hist_f, zcF, cw, cb, lg, lb)
        hist_s = jnp.pad(state_conv[l].astype(F32), ((0, 0), (HIST_PAD, 0), (0, 0)))
        cS = _conv(uS, hist_s, zcS, cw, cb, lg, lb)

        kidx = lambda tail: tail[..., :IDX_DIM]
        twice = lambda a: jnp.concatenate([a, a], axis=-1).astype(BF16)
        meta_edges = tuple(_edge_block(cfg_f, [a], cfg_f.lane_off) for a in (kbM, vbM, twice(kidx(tailM))))
        aM = _attend(cfg_m, qM, qiM, tailM, zaM, meta_edges, meta_edges, bias_m, tri)
        aF = _attend(cfg_f, qF, qiF, tailF, zaF, (kbF, vbF, twice(kidx(tailF))), meta_edges, bias_f, tri)
        tr = lambda a: jnp.swapaxes(a, 1, 2)
        lo = cfg_s.edge_block * KEY_BLOCK
        sample_edges = tuple(_edge_block(cfg_s, [c[l, :, :, lo:], tr(new)], 0)
                             for c, new in ((ck_t, kbS), (cv_t, vbS), (cx_t, twice(kidx(tailS)))))
        aS = _attend(cfg_s, qS, qiS, tailS, zaS, (ck_t, cv_t, cx_t), sample_edges, bias_s, tri, main_lead=(l,))

        cAux = jnp.concatenate([cS.reshape(ns, D_CONV), cM[0]], axis=0)
        aAux = jnp.concatenate([aS.reshape(ns, D_ATT), aM[0]], axis=0)
        haux = _outproj(cAux, aAux, haux, wo_all, l, fg, n_aux, final)
        hf = _outproj(cF.reshape(bp * seq, D_CONV), aF.reshape(bp * seq, D_ATT), hf, wo_all, l, fg,
                      tm_frames, final)

        with_meta = lambda m, f: jnp.concatenate([jnp.broadcast_to(m, (bp,) + m.shape[1:]), f], axis=1)
        outs["kp"].append(with_meta(kM, kF).reshape(bp, N_META + seq, N_HEADS, HEAD_DIM))
        outs["vp"].append(with_meta(vM, vF).reshape(bp, N_META + seq, N_HEADS, HEAD_DIM))
        outs["kip"].append(with_meta(kidx(tailM), kidx(tailF)))
        u_ext_p = jnp.concatenate([jnp.zeros((bp, CONV_WIDTH - 1, D_CONV), F32),
                                   jnp.broadcast_to(uM, (bp, N_META, D_CONV)), uF], axis=1)
        outs["cp"].append(u_ext_p[:, -(CONV_WIDTH - 1):])
        outs["ks"].append(kS.reshape(bs, dec, N_HEADS, HEAD_DIM))
        outs["vs"].append(vS.reshape(bs, dec, N_HEADS, HEAD_DIM))
        outs["kis"].append(kidx(tailS))
        u_ext_s = jnp.concatenate([state_conv[l].astype(F32), uS], axis=1)
        outs["cs"].append(u_ext_s[:, -(CONV_WIDTH - 1):])

    y_prompt = hf.reshape(bp, seq, d)
    y_sample = haux[:bs * dec].reshape(bs, dec, d)
    st = lambda name: jnp.stack(outs[name])
    return (y_prompt, y_sample, st("kp"), st("vp"), st("kip"), st("cp"),
            st("ks"), st("vs"), st("kis"), st("cs"))


def kernel(x_prompt, x_sample, cache_k, cache_v, cache_kidx, state_conv, meta_tokens, norm_g, w_in,
           conv_w, conv_b, conv_ln_g, conv_ln_b, w_out, rel_bias, final_g):
    return _forward(x_prompt, x_sample, cache_k, cache_v, cache_kidx, state_conv, meta_tokens,
                    norm_g, w_in, conv_w, conv_b, conv_ln_g, conv_ln_b, w_out, rel_bias, final_g)
```

```python
import functools
import math

import jax
import jax.numpy as jnp
from jax import lax
from jax.experimental import pallas as pl
from jax.experimental.pallas import tpu as pltpu

N_META = 16
CHUNK = 64
CHUNK_SHIFT = 6
D_CONV = 1024
CONV_WIDTH = 31
N_HEADS = 16
HEAD_DIM = 64
D_ATT = N_HEADS * HEAD_DIM
IDX_HEADS = 16
IDX_DIM = 64
TOPK_MAX = 256
N_BUCKETS = 32
MAX_DISTANCE = 128
NORM_EPS = 1e-6
NEG = -1e30

LANE = 128
SUBLANE = 8
KEY_BLOCK = 256
HALF = KEY_BLOCK // LANE
N_PAIRS = N_HEADS // 2
PAIR_BUFFERS = 2
HIST_ROWS = 32
HIST_PAD = HIST_ROWS - (CONV_WIDTH - 1)
GROUP = 1024
N_GROUPS = 8
VMEM_LIMIT = 56 * 1024 * 1024
INT_MIN = -2147483648
HALF_BITS = 16
LOW_MASK = (1 << HALF_BITS) - 1
HALF_BIAS = 1 << (HALF_BITS - 1)
M_INIT = -1e29
FAR_BUCKET = N_BUCKETS // 2 - 1

F32 = jnp.float32
BF16 = jnp.bfloat16
NT_DIMS = (((1,), (1,)), ((), ()))


def _silu(x):
    return x * jax.nn.sigmoid(x)


def _inproj_kernel(x_ref, g_ref, wm_ref, wt_ref,
                   u_ref, zc_ref, q_ref, k_ref, v_ref, za_ref, qi_ref, tail_ref, kb_ref, vb_ref,
                   xn_ref, a_ref):
    n = pl.program_id(1)
    j = pl.program_id(2)

    @pl.when(n == 0)
    def _():
        x = x_ref[...]
        ms = jnp.mean(x * x, axis=-1, keepdims=True)
        xn_ref[j] = (x * lax.rsqrt(ms + NORM_EPS) * g_ref[...]).astype(BF16)

    def group():
        return jnp.dot(xn_ref[j], wm_ref[...], preferred_element_type=F32)

    @pl.when(n == 0)
    def _():
        a_ref[j] = group()

    @pl.when(n == 1)
    def _():
        u_ref[...] = a_ref[j] * jax.nn.sigmoid(group())

    @pl.when(n == 2)
    def _():
        zc_ref[...] = group().astype(BF16)

    @pl.when(n == 3)
    def _():
        q_ref[...] = (group() * (HEAD_DIM ** -0.5)).astype(BF16)

    @pl.when(n == 4)
    def _():
        y = group()
        k_ref[...] = y
        kb_ref[...] = y.astype(BF16)

    @pl.when(n == 5)
    def _():
        y = group()
        v_ref[...] = y
        vb_ref[...] = y.astype(BF16)

    @pl.when(n == 6)
    def _():
        za_ref[...] = group().astype(BF16)

    @pl.when(n == 7)
    def _():
        qi_ref[...] = group().astype(BF16)
        tail_ref[...] = jnp.dot(xn_ref[j], wt_ref[...], preferred_element_type=F32)


def _inproj(x2d, g, w_all, layer, w_tail, tm):
    m, d = x2d.shape
    assert m % tm == 0 and w_all.shape[2] >= N_GROUPS * GROUP
    n_tiles = m // tm
    share = 2 if n_tiles % 2 == 0 else 1
    last = n_tiles - 1
    row = lambda width, dtype: jax.ShapeDtypeStruct((m, width), dtype)
    out_shape = (row(GROUP, F32), row(GROUP, BF16), row(GROUP, BF16), row(GROUP, F32),
                 row(GROUP, F32), row(GROUP, BF16), row(GROUP, BF16), row(LANE, F32),
                 row(GROUP, BF16), row(GROUP, BF16))

    def tile_at(n_w):
        def index(i, n, j):
            first = i * share
            return jnp.where(n < n_w, jnp.maximum(first - 1, 0),
                             jnp.where(n == n_w, first + j, jnp.minimum(first + share - 1, last)))
        return index

    ospec = lambda width, n_w: pl.BlockSpec((tm, width), lambda i, n, j: (tile_at(n_w)(i, n, j), 0))
    return pl.pallas_call(
        _inproj_kernel,
        grid=(n_tiles // share, N_GROUPS, share),
        in_specs=[
            pl.BlockSpec((tm, d), lambda i, n, j: (tile_at(0)(i, n, j), 0)),
            pl.BlockSpec((1, d), lambda i, n, j: (0, 0)),
            pl.BlockSpec((None, d, GROUP), lambda i, n, j: (layer, 0, n)),
            pl.BlockSpec((d, LANE), lambda i, n, j: (0, 0)),
        ],
        out_specs=([ospec(GROUP, n_w) for n_w in range(1, N_GROUPS)] + [ospec(LANE, N_GROUPS - 1)]
                   + [ospec(GROUP, 4), ospec(GROUP, 5)]),
        out_shape=out_shape,
        scratch_shapes=[pltpu.VMEM((share, tm, d), BF16), pltpu.VMEM((share, tm, GROUP), F32)],
        compiler_params=pltpu.CompilerParams(
            dimension_semantics=("arbitrary", "arbitrary", "arbitrary"), vmem_limit_bytes=VMEM_LIMIT),
        name="inproj",
    )(x2d, g, w_all, w_tail)


def _conv_kernel(u_ref, hist_ref, zc_ref, cw_ref, cb_ref, lg_ref, lb_ref, o_ref, ext_ref, c_ref,
                 *, tt, rc):
    t = pl.program_id(1)
    n_hist = CONV_WIDTH - 1

    for b in range(SUBLANE):
        keep = n_hist - b

        @pl.when(t == 0)
        def _(b=b, keep=keep):
            ext_ref[b, 0:keep, :] = hist_ref[HIST_PAD + b:HIST_ROWS, :]

        @pl.when(t > 0)
        def _(b=b, keep=keep):
            ext_ref[b, 0:keep, :] = ext_ref[b, tt:tt + keep, :]

    for b in range(SUBLANE):
        ext_ref[b, n_hist - b:n_hist - b + tt, :] = u_ref[...]

    half_c = D_CONV // 2

    def taps(r, carry):
        r0 = pl.multiple_of(r * (2 * SUBLANE), 2 * SUBLANE)
        for c0 in (0, half_c):
            cs = slice(c0, c0 + half_c)
            acc = [jnp.broadcast_to(cb_ref[:, cs], (SUBLANE, half_c)) for _ in range(2)]
            for b in range(SUBLANE):
                n_a = (CONV_WIDTH - 1 - b) // SUBLANE + 1
                win = ext_ref[b, pl.ds(r0, (n_a + 1) * SUBLANE), cs]
                for a in range(n_a):
                    j = a * SUBLANE + b
                    w8 = cw_ref[j * SUBLANE:(j + 1) * SUBLANE, cs]
                    for g in range(2):
                        lo = (a + g) * SUBLANE
                        acc[g] = acc[g] + w8 * win[lo:lo + SUBLANE, :]
            for g in range(2):
                c_ref[pl.ds(r0 + g * SUBLANE, SUBLANE), cs] = acc[g]
        return carry

    lax.fori_loop(0, tt // (2 * SUBLANE), taps, 0)

    def norm_gate(r, carry):
        r0 = pl.multiple_of(r * rc, rc)
        c = c_ref[pl.ds(r0, rc), :]
        mu = jnp.mean(c, axis=-1, keepdims=True)
        xc = c - mu
        var = jnp.mean(xc * xc, axis=-1, keepdims=True)
        y = xc * lax.rsqrt(var + NORM_EPS) * lg_ref[...] + lb_ref[...]
        z = zc_ref[pl.ds(r0, rc), :].astype(F32)
        o_ref[pl.ds(r0, rc), :] = (_silu(y) * _silu(z)).astype(BF16)
        return carry

    lax.fori_loop(0, tt // rc, norm_gate, 0)


def _conv(u, hist, zc, cw, cb, lg, lb):
    nb, t, c = u.shape
    tt = min(t, 512)
    rc = min(tt, 64)
    assert t % tt == 0 and tt % rc == 0 and tt % (2 * SUBLANE) == 0 and c == D_CONV
    hist_map = (lambda b, i: (b, 0, 0)) if hist.shape[0] == nb else (lambda b, i: (0, 0, 0))
    vec = pl.BlockSpec((1, c), lambda b, i: (0, 0))
    return pl.pallas_call(
        functools.partial(_conv_kernel, tt=tt, rc=rc),
        grid=(nb, t // tt),
        in_specs=[
            pl.BlockSpec((None, tt, c), lambda b, i: (b, i, 0)),
            pl.BlockSpec((None, HIST_ROWS, c), hist_map),
            pl.BlockSpec((None, tt, c), lambda b, i: (b, i, 0)),
            pl.BlockSpec((CONV_WIDTH * SUBLANE, c), lambda b, i: (0, 0)),
            vec, vec, vec,
        ],
        out_specs=pl.BlockSpec((None, tt, c), lambda b, i: (b, i, 0)),
        out_shape=jax.ShapeDtypeStruct((nb, t, c), BF16),
        scratch_shapes=[pltpu.VMEM((SUBLANE, HIST_ROWS + tt, c), F32), pltpu.VMEM((tt, c), F32)],
        compiler_params=pltpu.CompilerParams(
            dimension_semantics=("arbitrary", "arbitrary"), vmem_limit_bytes=VMEM_LIMIT),
        name="conv",
    )(u, hist, zc, cw, cb, lg, lb)


def _outproj_kernel(c_ref, a_ref, h_ref, wc_ref, wa_ref, fg_ref, o_ref, *, final):
    y = (jnp.dot(c_ref[...], wc_ref[...], preferred_element_type=F32)
         + jnp.dot(a_ref[...], wa_ref[...], preferred_element_type=F32))
    y = h_ref[...] + y
    if final:
        ms = jnp.mean(y * y, axis=-1, keepdims=True)
        y = y * lax.rsqrt(ms + NORM_EPS) * fg_ref[...]
    o_ref[...] = y


def _outproj(c, a, h, wo_all, layer, fg, tm, final):
    m, d = h.shape
    assert m % tm == 0 and D_CONV == D_ATT
    half = lambda: pl.BlockSpec((tm, D_CONV), lambda i: (i, 0))
    full = lambda: pl.BlockSpec((tm, d), lambda i: (i, 0))
    wspec = lambda part: pl.BlockSpec((None, D_CONV, d), lambda i: (layer, part, 0))
    return pl.pallas_call(
        functools.partial(_outproj_kernel, final=final),
        grid=(m // tm,),
        in_specs=[half(), half(), full(), wspec(0), wspec(1), pl.BlockSpec((1, d), lambda i: (0, 0))],
        out_specs=full(),
        out_shape=jax.ShapeDtypeStruct((m, d), F32),
        compiler_params=pltpu.CompilerParams(
            dimension_semantics=("arbitrary",), vmem_limit_bytes=VMEM_LIMIT),
        name="outproj",
    )(c, a, h, wo_all, wo_all, fg)


def _bias_kernel(bucket_ref, rb_ref, o_ref, *, n_near, tq):
    h = pl.program_id(0)
    far = rb_ref[FAR_BUCKET, h]
    for j in range(n_near):
        def body(r, carry):
            r0 = pl.multiple_of(r * SUBLANE, SUBLANE)
            bk = bucket_ref[j, pl.ds(r0, SUBLANE), :]
            acc = jnp.zeros(bk.shape, F32)
            for b in range(N_BUCKETS):
                acc = jnp.where(bk == b, rb_ref[b, h], acc)
            o_ref[j, pl.ds(r0, SUBLANE), :] = acc - far
            return carry
        lax.fori_loop(0, tq // SUBLANE, body, 0)


def _bias_tiles(bucket, rel_bias):
    n_near, tq, kb = bucket.shape
    return pl.pallas_call(
        functools.partial(_bias_kernel, n_near=n_near, tq=tq),
        grid=(N_HEADS,),
        in_specs=[pl.BlockSpec((n_near, tq, kb), lambda h: (0, 0, 0)),
                  pl.BlockSpec(memory_space=pltpu.SMEM)],
        out_specs=pl.BlockSpec((None, n_near, tq, kb), lambda h: (h, 0, 0, 0)),
        out_shape=jax.ShapeDtypeStruct((N_HEADS, n_near, tq, kb), F32),
        compiler_params=pltpu.CompilerParams(dimension_semantics=("arbitrary",)),
        name="bias_tiles",
    )(bucket, rel_bias)


def _t5_bucket(rel):
    half = N_BUCKETS // 2
    max_exact = half // 2
    n = jnp.abs(rel)
    large = max_exact + (jnp.log(jnp.maximum(n, 1).astype(jnp.float32) / max_exact)
                         / math.log(MAX_DISTANCE / max_exact) * (half - max_exact)).astype(jnp.int32)
    large = jnp.minimum(large, half - 1)
    return jnp.where(rel > 0, half, 0) + jnp.where(n < max_exact, n, large)


class _AttnCfg:
    def __init__(self, tq, n_rows, n_qt, s_pad, lane_off, n_pos, q_pos0, near_step, near_base, n_near, top_k,
                 edge_block, main_shift, n_main, dims_major=False):
        self.dims_major = dims_major
        self.edge_block, self.main_shift, self.n_main = edge_block, main_shift, n_main
        assert all(kb == edge_block or 0 <= kb - main_shift < n_main for kb in range(s_pad // KEY_BLOCK))
        self.tq, self.n_rows, self.n_qt, self.s_pad, self.lane_off = tq, n_rows, n_qt, s_pad, lane_off
        self.n_pos, self.q_pos0 = n_pos, q_pos0
        self.near_step, self.near_base, self.n_near, self.top_k = near_step, near_base, n_near, top_k
        self.rel0 = near_base * KEY_BLOCK - lane_off - q_pos0
        assert s_pad % KEY_BLOCK == 0 and tq % LANE == 0 and n_rows <= tq and n_rows % (2 * SUBLANE) == 0
        assert n_rows == tq or n_qt == 1
        assert near_step * KEY_BLOCK == tq or n_qt == 1
        assert (near_base == 0 and near_step == 0) or 1 - self.rel0 >= MAX_DISTANCE
        assert (near_base + near_step * (n_qt - 1) + n_near) * KEY_BLOCK <= s_pad

    def bucket_table(self):
        j = jnp.arange(self.n_near, dtype=jnp.int32)[:, None, None]
        r = jnp.arange(self.n_rows, dtype=jnp.int32)[None, :, None]
        c = jnp.arange(KEY_BLOCK, dtype=jnp.int32)[None, None, :]
        rel = self.rel0 + KEY_BLOCK * j + c - r
        return _t5_bucket(lax.optimization_barrier(rel))


def _chunk_of(pos):
    return jnp.where(pos < N_META, 0, 1 + ((pos - N_META) >> CHUNK_SHIFT))


def _for_blocks(n, fn, group=2):
    def body(kg, carry):
        for r in range(group):
            fn(group * kg + r)
        return carry

    if isinstance(n, int):
        lax.fori_loop(0, n // group, body, 0)
        for kb in range(n - n % group, n):
            fn(kb)
        return
    lax.fori_loop(0, n >> (group.bit_length() - 1), body, 0)
    part = group // 2
    while part:
        start = n & ~(2 * part - 1)

        @pl.when((n & part) != 0)
        def _(start=start, part=part):
            for r in range(part):
                fn(start + r)

        part //= 2


def _attn_kernel(q_ref, qi_ref, w_ref, za_ref, k_ref, v_ref, kx_ref, ke_ref, ve_ref, kxe_ref,
                 bias_ref, tri_ref, o_ref,
                 qm_ref, qim_ref, wb_ref, keys_ref, madd_ref, s_ref, mx_ref, l_ref, acc_ref, ties_ref,
                 hi_ref, lo_ref, sel_ref, *, cfg):
    tq, rq = cfg.tq, cfg.n_rows
    i = pl.program_id(1)
    near0 = cfg.near_base + cfg.near_step * i if cfg.near_step else cfg.near_base
    nkb = near0 + cfg.n_near
    lane = lax.broadcasted_iota(jnp.int32, (rq, LANE), 1)
    row = lax.broadcasted_iota(jnp.int32, (rq, LANE), 0)
    low_half = lane < HEAD_DIM
    q_pos_base = cfg.q_pos0 + i * tq
    q_chunk = _chunk_of(q_pos_base + row)

    def admissible(blk):
        kpos = blk * LANE + lane - cfg.lane_off
        return (kpos >= 0) & (kpos < cfg.n_pos) & (_chunk_of(kpos) <= q_chunk)

    def key_operand(main_ref, edge_ref, fs, kb):
        def main(idx):
            ks = pl.ds(pl.multiple_of(idx * KEY_BLOCK, KEY_BLOCK), KEY_BLOCK)
            return (main_ref[fs, ks] if cfg.dims_major else main_ref[ks, fs]).astype(BF16)

        edge = (edge_ref[fs, :] if cfg.dims_major else edge_ref[:, fs]).astype(BF16)
        if isinstance(kb, int):
            return edge if kb == cfg.edge_block else main(kb - cfg.main_shift)
        if cfg.n_main == 0:
            return edge
        return jnp.where(kb == cfg.edge_block, edge, main(jnp.clip(kb - cfg.main_shift, 0, cfg.n_main - 1)))

    def dot_keys(lhs, rhs):
        if cfg.dims_major:
            return jnp.dot(lhs, rhs, preferred_element_type=F32)
        return lax.dot_general(lhs, rhs, NT_DIMS, preferred_element_type=F32)

    def dot_values(lhs, rhs):
        if cfg.dims_major:
            return lax.dot_general(lhs, rhs, NT_DIMS, preferred_element_type=F32)
        return jnp.dot(lhs, rhs, preferred_element_type=F32)

    all_feat = slice(0, LANE)
    pair_feat = lambda p: slice(p * LANE, (p + 1) * LANE)

    zero = jnp.zeros((rq, LANE), BF16)
    for p in range(N_PAIRS):
        sl = slice(p * LANE, (p + 1) * LANE)
        qp = q_ref[:, sl]
        qm_ref[p, 0:rq, :] = jnp.where(low_half, qp, zero)
        qm_ref[p, rq:2 * rq, :] = jnp.where(low_half, zero, qp)
        qip = qi_ref[:, sl]
        qim_ref[p, 0:rq, :] = jnp.where(low_half, qip, zero)
        qim_ref[p, rq:2 * rq, :] = jnp.where(low_half, zero, qip)
    w_scale = (IDX_HEADS ** -0.5) * (IDX_DIM ** -0.5)
    for h in range(IDX_HEADS):
        col = IDX_DIM + h
        wb_ref[h] = jnp.broadcast_to(w_ref[:, col:col + 1] * w_scale, (rq, LANE))

    no_query = jnp.full((tq - rq, LANE), NEG, F32) if rq < tq else None

    def index_block(kb):
        kx = key_operand(kx_ref, kxe_ref, all_feat, kb)
        acc = [jnp.zeros((rq, LANE), F32) for _ in range(HALF)]
        for p in range(N_PAIRS):
            d = jnp.maximum(dot_keys(qim_ref[p], kx), 0.0)
            for hf in range(HALF):
                cs = slice(hf * LANE, (hf + 1) * LANE)
                acc[hf] = acc[hf] + wb_ref[2 * p] * d[0:rq, cs] + wb_ref[2 * p + 1] * d[rq:2 * rq, cs]
        for hf in range(HALF):
            blk = kb * HALF + hf
            sc = jnp.where(admissible(blk), acc[hf] + 0.0, NEG)
            if no_query is not None:
                sc = jnp.concatenate([sc, no_query], axis=0)
            bits = pltpu.bitcast(sc.T, jnp.int32)
            key = jnp.where(bits < 0, bits ^ jnp.int32(0x7FFFFFFF), bits)
            rows = pl.ds(pl.multiple_of(blk * LANE, LANE), LANE)
            keys_ref[rows, :] = key
            hi_ref[rows, :] = (key >> HALF_BITS).astype(jnp.int16)
            lo_ref[rows, :] = ((key & LOW_MASK) - HALF_BIAS).astype(jnp.int16)

    _for_blocks(nkb, index_block)

    kf = jnp.float32(cfg.top_k)
    n_part = 4
    packed_rows = 2 * SUBLANE

    def count16(ref, pred):
        def body(c_i, parts):
            chunk = ref[pl.ds(pl.multiple_of(c_i * LANE, LANE), LANE), :]
            parts = list(parts)
            for j in range(LANE // packed_rows):
                kk = chunk[j * packed_rows:(j + 1) * packed_rows, :]
                parts[j % n_part] = parts[j % n_part] + jnp.where(pred(kk), jnp.int16(1), jnp.int16(0))
            return tuple(parts)
        parts = lax.fori_loop(0, nkb * HALF, body,
                              tuple(jnp.zeros((packed_rows, tq), jnp.int16) for _ in range(n_part)))
        tot = ((parts[0] + parts[1]) + (parts[2] + parts[3])).astype(jnp.int32)
        return jnp.broadcast_to(jnp.sum(tot, axis=0, keepdims=True), (SUBLANE, tq)).astype(F32)

    def pack16(v):
        return jnp.concatenate([v, v], axis=0).astype(jnp.int16)

    def bisect16(ref, target):
        def step(it, lo):
            cand = lo + lax.shift_left(jnp.int32(1), HALF_BITS - 1 - it)
            cand16 = pack16(cand)
            return jnp.where(count16(ref, lambda kk: kk >= cand16) >= target, cand, lo)
        return lax.fori_loop(0, HALF_BITS, step, jnp.full((SUBLANE, tq), -HALF_BIAS, jnp.int32))

    hi_thr = bisect16(hi_ref, kf)
    hi_thr16 = pack16(hi_thr)
    above = count16(hi_ref, lambda kk: kk > hi_thr16)

    def low_keys(c_i, carry):
        rows = pl.ds(pl.multiple_of(c_i * LANE, LANE), LANE)
        hi, lo = hi_ref[rows, :], lo_ref[rows, :]
        for j in range(LANE // packed_rows):
            rs = slice(j * packed_rows, (j + 1) * packed_rows)
            sel_ref[pl.ds(pl.multiple_of(c_i * LANE, LANE) + j * packed_rows, packed_rows), :] = jnp.where(
                hi[rs, :] == hi_thr16, lo[rs, :], jnp.int16(-HALF_BIAS))
        return carry

    lax.fori_loop(0, nkb * HALF, low_keys, 0)
    lo_thr = bisect16(sel_ref, kf - above)
    thr8 = (hi_thr << HALF_BITS) | ((lo_thr + HALF_BIAS) & LOW_MASK)

    def count(pred):
        def body(c_i, parts):
            chunk = keys_ref[pl.ds(pl.multiple_of(c_i * LANE, LANE), LANE), :]
            parts = list(parts)
            for j in range(LANE // SUBLANE):
                kk = chunk[j * SUBLANE:(j + 1) * SUBLANE, :]
                parts[j % n_part] = parts[j % n_part] + jnp.where(pred(kk), 1.0, 0.0)
            return tuple(parts)
        parts = lax.fori_loop(0, nkb * HALF, body, tuple(jnp.zeros((SUBLANE, tq), F32) for _ in range(n_part)))
        tot = (parts[0] + parts[1]) + (parts[2] + parts[3])
        return jnp.broadcast_to(jnp.sum(tot, axis=0, keepdims=True), (SUBLANE, tq))

    need8 = kf - count(lambda kk: kk > thr8)
    thr = thr8[0:1, :]
    need = need8[0:1, :]

    key_row = lax.broadcasted_iota(jnp.int32, (KEY_BLOCK, tq), 0)
    q_chunk_t = _chunk_of(q_pos_base + lax.broadcasted_iota(jnp.int32, (KEY_BLOCK, tq), 1))

    ties_ref[...] = jnp.zeros((SUBLANE, tq), F32)

    def mask_block(kb):
        r0 = pl.multiple_of(kb * KEY_BLOCK, KEY_BLOCK)
        kk = keys_ref[pl.ds(r0, KEY_BLOCK), :]
        eq = kk == thr
        eqf = jnp.where(eq, 1.0, 0.0)
        seen = ties_ref[...]
        rank = seen[0:1, :] + jnp.dot(tri_ref[...], eqf.astype(BF16), preferred_element_type=F32)
        sel = jnp.where(eq, jnp.where(rank <= need, 0.0, NEG), jnp.where(kk > thr, 0.0, NEG))
        kpos = r0 + key_row - cfg.lane_off
        ok = (kpos >= 0) & (kpos < cfg.n_pos) & (_chunk_of(kpos) <= q_chunk_t)
        madd_ref[kb] = jnp.where(ok, sel, NEG).T[0:rq]
        ties_ref[...] = seen + jnp.sum(eqf, axis=0, keepdims=True)

    _for_blocks(nkb, mask_block)

    def logits(g, p, kb, near_j):
        s = dot_keys(qm_ref[p], key_operand(k_ref, ke_ref, pair_feat(p), kb))
        ma = madd_ref[kb]
        if near_j is None:
            add = jnp.concatenate([ma, ma], axis=0)
        else:
            add = jnp.concatenate([ma + bias_ref[2 * p, near_j], ma + bias_ref[2 * p + 1, near_j]], axis=0)
        s = s + add
        s_ref[g, kb] = s
        mx_ref[g] = jnp.maximum(mx_ref[g], jnp.maximum(s[:, 0:LANE], s[:, LANE:KEY_BLOCK]))

    def weighted_values(g, p, kb):
        m = mx_ref[g]
        pe = jnp.exp(s_ref[g, kb] - jnp.concatenate([m] * HALF, axis=1))
        l_ref[g] = l_ref[g] + (pe[:, 0:LANE] + pe[:, LANE:KEY_BLOCK])
        acc_ref[g] = acc_ref[g] + dot_values(pe.astype(BF16), key_operand(v_ref, ve_ref, pair_feat(p), kb))

    for stage in range(N_PAIRS + 1):
        p_a = stage if stage < N_PAIRS else None
        p_b = stage - 1 if stage > 0 else None

        def both(kb, near_j, p_a=p_a, p_b=p_b):
            if p_a is not None:
                logits(p_a % 2, p_a, kb, near_j)
            if p_b is not None:
                weighted_values(p_b % 2, p_b, kb)

        if p_a is not None:
            mx_ref[p_a % 2] = jnp.full((2 * rq, LANE), M_INIT, F32)

        _for_blocks(near0, lambda kb, both=both: both(kb, None), group=4)
        for j in range(cfg.n_near):
            both(near0 + j, j)

        if p_a is not None:
            g = p_a % 2
            mx_ref[g] = jnp.broadcast_to(jnp.max(mx_ref[g], axis=1, keepdims=True), (2 * rq, LANE))
            l_ref[g] = jnp.zeros((2 * rq, LANE), F32)
            acc_ref[g] = jnp.zeros((2 * rq, LANE), F32)
        if p_b is not None:
            g = p_b % 2
            o = acc_ref[g] / jnp.sum(l_ref[g], axis=1, keepdims=True)
            o = jnp.where(low_half, o[0:rq], o[rq:2 * rq])
            z = za_ref[:, p_b * LANE:(p_b + 1) * LANE].astype(F32)
            o_ref[:, p_b * LANE:(p_b + 1) * LANE] = (o * _silu(z)).astype(BF16)


def _attention(cfg, q, qi, w, za, mains, edges, bias, tri, main_lead=()):
    nb, t, _ = q.shape
    tq, rq, s_pad = cfg.tq, cfg.n_rows, cfg.s_pad
    assert t == rq * cfg.n_qt
    qspec = lambda width: pl.BlockSpec((None, rq, width), lambda b, i: (b, i, 0))

    def kspec(a, lead, buffers):
        per_batch = a.shape[len(lead)] == nb
        assert per_batch or a.shape[len(lead)] == 1
        return pl.BlockSpec((None,) * (len(lead) + 1) + a.shape[-2:],
                            lambda b, i: lead + ((b if per_batch else 0), 0, 0),
                            pipeline_mode=pl.Buffered(buffers if per_batch and nb > 1 else 1))

    key_axis = -1 if cfg.dims_major else -2
    for a in mains:
        assert a.shape[key_axis] >= cfg.n_main * KEY_BLOCK
    for a in edges:
        assert a.shape[key_axis] == KEY_BLOCK
    nkb_max = s_pad // KEY_BLOCK
    return pl.pallas_call(
        functools.partial(_attn_kernel, cfg=cfg),
        grid=(nb, cfg.n_qt),
        in_specs=[
            qspec(D_ATT), qspec(D_ATT), qspec(LANE), qspec(D_ATT),
            *[kspec(a, tuple(main_lead), 2 if a.dtype == BF16 else 1) for a in mains],
            *[kspec(a, (), 1) for a in edges],
            pl.BlockSpec((N_HEADS, cfg.n_near, rq, KEY_BLOCK), lambda b, i: (0, 0, 0, 0),
                         pipeline_mode=pl.Buffered(1)),
            pl.BlockSpec((KEY_BLOCK, KEY_BLOCK), lambda b, i: (0, 0)),
        ],
        out_specs=qspec(D_ATT),
        out_shape=jax.ShapeDtypeStruct((nb, t, D_ATT), BF16),
        scratch_shapes=[
            pltpu.VMEM((N_PAIRS, 2 * rq, LANE), BF16),
            pltpu.VMEM((N_PAIRS, 2 * rq, LANE), BF16),
            pltpu.VMEM((IDX_HEADS, rq, LANE), F32),
            pltpu.VMEM((s_pad, tq), jnp.int32),
            pltpu.VMEM((nkb_max, rq, KEY_BLOCK), F32),
            pltpu.VMEM((PAIR_BUFFERS, nkb_max, 2 * rq, KEY_BLOCK), F32),
            pltpu.VMEM((PAIR_BUFFERS, 2 * rq, LANE), F32),
            pltpu.VMEM((PAIR_BUFFERS, 2 * rq, LANE), F32),
            pltpu.VMEM((PAIR_BUFFERS, 2 * rq, LANE), F32),
            pltpu.VMEM((SUBLANE, tq), F32),
            pltpu.VMEM((s_pad, tq), jnp.int16),
            pltpu.VMEM((s_pad, tq), jnp.int16),
            pltpu.VMEM((s_pad, tq), jnp.int16),
        ],
        compiler_params=pltpu.CompilerParams(
            dimension_semantics=("arbitrary", "arbitrary"), vmem_limit_bytes=VMEM_LIMIT),
        name="attention",
    )(q, qi, w, za, *mains, *edges, bias, tri)


def _edge_block(cfg, parts, lead_zeros):
    key_axis = 2 if cfg.dims_major else 1
    dtype = parts[0].dtype
    parts = [a.astype(dtype) for a in parts]

    def zeros(n):
        shape = list(parts[0].shape)
        shape[key_axis] = n
        return [jnp.zeros(shape, dtype)] if n else []

    n_end = KEY_BLOCK - lead_zeros - sum(a.shape[key_axis] for a in parts)
    return jnp.concatenate(zeros(lead_zeros) + parts + zeros(n_end), axis=key_axis)


def _attend(cfg, q, qi, tail, za, mains, edges, bias, tri, main_lead=()):
    rows = lambda a: jnp.pad(a, ((0, 0), (0, cfg.n_rows * cfg.n_qt - a.shape[1]), (0, 0)))
    out = _attention(cfg, rows(q), rows(qi), rows(tail), rows(za), mains, edges, bias, tri, main_lead)
    return out[:, :q.shape[1]]


def _forward(x_prompt, x_sample, cache_k, cache_v, cache_kidx, state_conv, meta_tokens,
             norm_g, w_in, conv_w, conv_b, conv_ln_g, conv_ln_b, w_out, rel_bias, final_g,
             *, tq_frames=256, tm_frames=512):
    depth = w_in.shape[0]
    bp, seq, d = x_prompt.shape
    bs, dec, _ = x_sample.shape
    past = cache_k.shape[2] - N_META
    assert seq % tq_frames == 0 and seq % CHUNK == 0 and (bp * seq) % tm_frames == 0

    n_qt = seq // tq_frames
    cfg_f = _AttnCfg(tq=tq_frames, n_rows=tq_frames, n_qt=n_qt, s_pad=KEY_BLOCK + seq,
                     lane_off=KEY_BLOCK - N_META, n_pos=N_META + seq, q_pos0=N_META,
                     near_step=tq_frames // KEY_BLOCK, near_base=0,
                     n_near=tq_frames // KEY_BLOCK + 1, top_k=min(TOPK_MAX, seq // 4),
                     edge_block=0, main_shift=1, n_main=seq // KEY_BLOCK)
    n_cache = N_META + past
    n_pos_s = n_cache + dec
    s_pad_s = -(-n_pos_s // KEY_BLOCK) * KEY_BLOCK
    last_s = s_pad_s // KEY_BLOCK - 1
    near_s = max(0, (n_cache - MAX_DISTANCE) // KEY_BLOCK)
    cfg_s = _AttnCfg(tq=LANE, n_rows=dec, n_qt=1, s_pad=s_pad_s, lane_off=0,
                     n_pos=n_pos_s, q_pos0=n_cache, near_step=0, near_base=near_s,
                     n_near=last_s - near_s + 1, top_k=min(TOPK_MAX, (past + dec) // 4),
                     edge_block=last_s, main_shift=0, n_main=last_s, dims_major=True)
    cfg_m = _AttnCfg(tq=LANE, n_rows=N_META, n_qt=1, s_pad=KEY_BLOCK, lane_off=KEY_BLOCK - N_META,
                     n_pos=N_META, q_pos0=0, near_step=0, near_base=0, n_near=1,
                     top_k=min(TOPK_MAX, seq // 4), edge_block=0, main_shift=1, n_main=0)
    assert dec <= LANE and n_cache >= last_s * KEY_BLOCK

    bias_f = _bias_tiles(cfg_f.bucket_table(), rel_bias)
    bias_s = _bias_tiles(cfg_s.bucket_table(), rel_bias)
    bias_m = _bias_tiles(cfg_m.bucket_table(), rel_bias)
    tri = (jnp.arange(KEY_BLOCK)[None, :] <= jnp.arange(KEY_BLOCK)[:, None]).astype(BF16)

    n_aux = bs * dec + N_META
    hf = x_prompt.reshape(bp * seq, d)
    haux = jnp.concatenate([x_sample.reshape(bs * dec, d), meta_tokens.astype(x_sample.dtype)], axis=0)

    w_all = w_in.astype(BF16)
    wo_all = w_out.astype(BF16)
    feat_major = lambda c: jnp.transpose(c, (0, 1, 3, 4, 2)).reshape(depth, bs, D_ATT, n_cache)
    ck_t, cv_t = feat_major(cache_k), feat_major(cache_v)
    cx_t = jnp.swapaxes(cache_kidx, 2, 3).astype(BF16)
    cx_t = jnp.concatenate([cx_t, cx_t], axis=2)
    outs = {name: [] for name in ("kp", "vp", "kip", "cp", "ks", "vs", "kis", "cs")}
    for l in range(depth):
        w_tail = jnp.pad(w_all[l, :, N_GROUPS * GROUP:], ((0, 0), (0, LANE - IDX_DIM - IDX_HEADS)))
        g = norm_g[l][None, :]
        cw = jnp.repeat(conv_w[l], SUBLANE, axis=0)
        cb, lg, lb = conv_b[l][None, :], conv_ln_g[l][None, :], conv_ln_b[l][None, :]
        final = l == depth - 1
        fg = final_g[None, :]

        uA, zcA, qA, kA, vA, zaA, qiA, tailA, kbA, vbA = _inproj(haux, g, w_all, l, w_tail, n_aux)
        uF, zcF, qF, kF, vF, zaF, qiF, tailF, kbF, vbF = _inproj(hf, g, w_all, l, w_tail, tm_frames)
        ns = bs * dec
        split = lambda a: (a[:ns].reshape(bs, dec, -1), a[ns:][None])
        uS, uM = split(uA); zcS, zcM = split(zcA); qS, qM = split(qA); kS, kM = split(kA)
        vS, vM = split(vA); zaS, zaM = split(zaA); qiS, qiM = split(qiA); tailS, tailM = split(tailA)
        kbS, kbM = split(kbA); vbS, vbM = split(vbA)
        b3 = lambda a: a.reshape(bp, seq, -1)
        uF, zcF, qF, kF, vF, zaF, qiF, tailF, kbF, vbF = map(
            b3, (uF, zcF, qF, kF, vF, zaF, qiF, tailF, kbF, vbF))

        zero_hist = jnp.zeros((1, HIST_ROWS, D_CONV), F32)
        cM = _conv(uM, zero_hist, zcM, cw, cb, lg, lb)
        hist_f = jnp.concatenate([jnp.zeros((1, HIST_ROWS - N_META, D_CONV), F32), uM], axis=1)
        cF = _conv(uF, hist_f, zcF, cw, cb, lg, lb)
        hist_s = jnp.pad(state_conv[l].astype(F32), ((0, 0), (HIST_PAD, 0), (0, 0)))
        cS = _conv(uS, hist_s, zcS, cw, cb, lg, lb)

        kidx = lambda tail: tail[..., :IDX_DIM]
        twice = lambda a: jnp.concatenate([a, a], axis=-1).astype(BF16)
        meta_edges = tuple(_edge_block(cfg_f, [a], cfg_f.lane_off) for a in (kbM, vbM, twice(kidx(tailM))))
        aM = _attend(cfg_m, qM, qiM, tailM, zaM, meta_edges, meta_edges, bias_m, tri)
        aF = _attend(cfg_f, qF, qiF, tailF, zaF, (kbF, vbF, twice(kidx(tailF))), meta_edges, bias_f, tri)
        tr = lambda a: jnp.swapaxes(a, 1, 2)
        lo = cfg_s.edge_block * KEY_BLOCK
        sample_edges = tuple(_edge_block(cfg_s, [c[l, :, :, lo:], tr(new)], 0)
                             for c, new in ((ck_t, kS), (cv_t, vS), (cx_t, twice(kidx(tailS)))))
        aS = _attend(cfg_s, qS, qiS, tailS, zaS, (ck_t, cv_t, cx_t), sample_edges, bias_s, tri, main_lead=(l,))

        cAux = jnp.concatenate([cS.reshape(ns, D_CONV), cM[0]], axis=0)
        aAux = jnp.concatenate([aS.reshape(ns, D_ATT), aM[0]], axis=0)
        haux = _outproj(cAux, aAux, haux, wo_all, l, fg, n_aux, final)
        hf = _outproj(cF.reshape(bp * seq, D_CONV), aF.reshape(bp * seq, D_ATT), hf, wo_all, l, fg,
                      tm_frames, final)

        with_meta = lambda m, f: jnp.concatenate([jnp.broadcast_to(m, (bp,) + m.shape[1:]), f], axis=1)
        outs["kp"].append(with_meta(kM, kF).reshape(bp, N_META + seq, N_HEADS, HEAD_DIM))
        outs["vp"].append(with_meta(vM, vF).reshape(bp, N_META + seq, N_HEADS, HEAD_DIM))
        outs["kip"].append(with_meta(kidx(tailM), kidx(tailF)))
        u_ext_p = jnp.concatenate([jnp.zeros((bp, CONV_WIDTH - 1, D_CONV), F32),
                                   jnp.broadcast_to(uM, (bp, N_META, D_CONV)), uF], axis=1)
        outs["cp"].append(u_ext_p[:, -(CONV_WIDTH - 1):])
        outs["ks"].append(kS.reshape(bs, dec, N_HEADS, HEAD_DIM))
        outs["vs"].append(vS.reshape(bs, dec, N_HEADS, HEAD_DIM))
        outs["kis"].append(kidx(tailS))
        u_ext_s = jnp.concatenate([state_conv[l].astype(F32), uS], axis=1)
        outs["cs"].append(u_ext_s[:, -(CONV_WIDTH - 1):])

    y_prompt = hf.reshape(bp, seq, d)
    y_sample = haux[:bs * dec].reshape(bs, dec, d)
    st = lambda name: jnp.stack(outs[name])
    return (y_prompt, y_sample, st("kp"), st("vp"), st("kip"), st("cp"),
            st("ks"), st("vs"), st("kis"), st("cs"))


def kernel(x_prompt, x_sample, cache_k, cache_v, cache_kidx, state_conv, meta_tokens, norm_g, w_in,
           conv_w, conv_b, conv_ln_g, conv_ln_b, w_out, rel_bias, final_g):
    return _forward(x_prompt, x_sample, cache_k, cache_v, cache_kidx, state_conv, meta_tokens,
                    norm_g, w_in, conv_w, conv_b, conv_ln_g, conv_ln_b, w_out, rel_bias, final_g)
```

```python
import functools
import math

import jax
import jax.numpy as jnp
from jax import lax
from jax.experimental import pallas as pl
from jax.experimental.pallas import tpu as pltpu

N_META = 16
CHUNK = 64
CHUNK_SHIFT = 6
D_CONV = 1024
CONV_WIDTH = 31
N_HEADS = 16
HEAD_DIM = 64
D_ATT = N_HEADS * HEAD_DIM
IDX_HEADS = 16
IDX_DIM = 64
TOPK_MAX = 256
N_BUCKETS = 32
MAX_DISTANCE = 128
NORM_EPS = 1e-6
NEG = -1e30

LANE = 128
SUBLANE = 8
KEY_BLOCK = 256
HALF = KEY_BLOCK // LANE
N_PAIRS = N_HEADS // 2
PAIR_BUFFERS = 2
HIST_ROWS = 32
HIST_PAD = HIST_ROWS - (CONV_WIDTH - 1)
GROUP = 1024
N_GROUPS = 8
VMEM_LIMIT = 56 * 1024 * 1024
INT_MIN = -2147483648
HALF_BITS = 16
LOW_MASK = (1 << HALF_BITS) - 1
HALF_BIAS = 1 << (HALF_BITS - 1)
M_INIT = -1e29
FAR_BUCKET = N_BUCKETS // 2 - 1

F32 = jnp.float32
BF16 = jnp.bfloat16
NT_DIMS = (((1,), (1,)), ((), ()))


def _silu(x):
    return x * jax.nn.sigmoid(x)


def _inproj_kernel(x_ref, g_ref, wm_ref, wt_ref,
                   u_ref, zc_ref, q_ref, k_ref, v_ref, za_ref, qi_ref, tail_ref, kb_ref, vb_ref,
                   xn_ref, a_ref):
    n = pl.program_id(1)
    j = pl.program_id(2)

    @pl.when(n == 0)
    def _():
        x = x_ref[...]
        ms = jnp.mean(x * x, axis=-1, keepdims=True)
        xn_ref[j] = (x * lax.rsqrt(ms + NORM_EPS) * g_ref[...]).astype(BF16)

    def group():
        return jnp.dot(xn_ref[j], wm_ref[...], preferred_element_type=F32)

    @pl.when(n == 0)
    def _():
        a_ref[j] = group()

    @pl.when(n == 1)
    def _():
        u_ref[...] = a_ref[j] * jax.nn.sigmoid(group())

    @pl.when(n == 2)
    def _():
        zc_ref[...] = group().astype(BF16)

    @pl.when(n == 3)
    def _():
        q_ref[...] = (group() * (HEAD_DIM ** -0.5)).astype(BF16)

    @pl.when(n == 4)
    def _():
        y = group()
        k_ref[...] = y
        kb_ref[...] = y.astype(BF16)

    @pl.when(n == 5)
    def _():
        y = group()
        v_ref[...] = y
        vb_ref[...] = y.astype(BF16)

    @pl.when(n == 6)
    def _():
        za_ref[...] = group().astype(BF16)

    @pl.when(n == 7)
    def _():
        qi_ref[...] = group().astype(BF16)
        tail_ref[...] = jnp.dot(xn_ref[j], wt_ref[...], preferred_element_type=F32)


def _inproj(x2d, g, w_all, layer, w_tail, tm):
    m, d = x2d.shape
    assert m % tm == 0 and w_all.shape[2] >= N_GROUPS * GROUP
    n_tiles = m // tm
    share = 2 if n_tiles % 2 == 0 else 1
    last = n_tiles - 1
    row = lambda width, dtype: jax.ShapeDtypeStruct((m, width), dtype)
    out_shape = (row(GROUP, F32), row(GROUP, BF16), row(GROUP, BF16), row(GROUP, F32),
                 row(GROUP, F32), row(GROUP, BF16), row(GROUP, BF16), row(LANE, F32),
                 row(GROUP, BF16), row(GROUP, BF16))

    def tile_at(n_w):
        def index(i, n, j):
            first = i * share
            return jnp.where(n < n_w, jnp.maximum(first - 1, 0),
                             jnp.where(n == n_w, first + j, jnp.minimum(first + share - 1, last)))
        return index

    ospec = lambda width, n_w: pl.BlockSpec((tm, width), lambda i, n, j: (tile_at(n_w)(i, n, j), 0))
    return pl.pallas_call(
        _inproj_kernel,
        grid=(n_tiles // share, N_GROUPS, share),
        in_specs=[
            pl.BlockSpec((tm, d), lambda i, n, j: (tile_at(0)(i, n, j), 0)),
            pl.BlockSpec((1, d), lambda i, n, j: (0, 0)),
            pl.BlockSpec((None, d, GROUP), lambda i, n, j: (layer, 0, n)),
            pl.BlockSpec((d, LANE), lambda i, n, j: (0, 0)),
        ],
        out_specs=([ospec(GROUP, n_w) for n_w in range(1, N_GROUPS)] + [ospec(LANE, N_GROUPS - 1)]
                   + [ospec(GROUP, 4), ospec(GROUP, 5)]),
        out_shape=out_shape,
        scratch_shapes=[pltpu.VMEM((share, tm, d), BF16), pltpu.VMEM((share, tm, GROUP), F32)],
        compiler_params=pltpu.CompilerParams(
            dimension_semantics=("arbitrary", "arbitrary", "arbitrary"), vmem_limit_bytes=VMEM_LIMIT),
        name="inproj",
    )(x2d, g, w_all, w_tail)


def _conv_kernel(u_ref, hist_ref, zc_ref, cw_ref, cb_ref, lg_ref, lb_ref, o_ref, ext_ref, c_ref,
                 *, tt, rc):
    t = pl.program_id(1)
    n_hist = CONV_WIDTH - 1

    for b in range(SUBLANE):
        keep = n_hist - b

        @pl.when(t == 0)
        def _(b=b, keep=keep):
            ext_ref[b, 0:keep, :] = hist_ref[HIST_PAD + b:HIST_ROWS, :]

        @pl.when(t > 0)
        def _(b=b, keep=keep):
            ext_ref[b, 0:keep, :] = ext_ref[b, tt:tt + keep, :]

    for b in range(SUBLANE):
        ext_ref[b, n_hist - b:n_hist - b + tt, :] = u_ref[...]

    half_c = D_CONV // 2

    def taps(r, carry):
        r0 = pl.multiple_of(r * (2 * SUBLANE), 2 * SUBLANE)
        for c0 in (0, half_c):
            cs = slice(c0, c0 + half_c)
            acc = [jnp.broadcast_to(cb_ref[:, cs], (SUBLANE, half_c)) for _ in range(2)]
            for b in range(SUBLANE):
                n_a = (CONV_WIDTH - 1 - b) // SUBLANE + 1
                win = ext_ref[b, pl.ds(r0, (n_a + 1) * SUBLANE), cs]
                for a in range(n_a):
                    j = a * SUBLANE + b
                    w8 = cw_ref[j * SUBLANE:(j + 1) * SUBLANE, cs]
                    for g in range(2):
                        lo = (a + g) * SUBLANE
                        acc[g] = acc[g] + w8 * win[lo:lo + SUBLANE, :]
            for g in range(2):
                c_ref[pl.ds(r0 + g * SUBLANE, SUBLANE), cs] = acc[g]
        return carry

    lax.fori_loop(0, tt // (2 * SUBLANE), taps, 0)

    def norm_gate(r, carry):
        r0 = pl.multiple_of(r * rc, rc)
        c = c_ref[pl.ds(r0, rc), :]
        mu = jnp.mean(c, axis=-1, keepdims=True)
        xc = c - mu
        var = jnp.mean(xc * xc, axis=-1, keepdims=True)
        y = xc * lax.rsqrt(var + NORM_EPS) * lg_ref[...] + lb_ref[...]
        z = zc_ref[pl.ds(r0, rc), :].astype(F32)
        o_ref[pl.ds(r0, rc), :] = (_silu(y) * _silu(z)).astype(BF16)
        return carry

    lax.fori_loop(0, tt // rc, norm_gate, 0)


def _conv(u, hist, zc, cw, cb, lg, lb):
    nb, t, c = u.shape
    tt = min(t, 512)
    rc = min(tt, 64)
    assert t % tt == 0 and tt % rc == 0 and tt % (2 * SUBLANE) == 0 and c == D_CONV
    hist_map = (lambda b, i: (b, 0, 0)) if hist.shape[0] == nb else (lambda b, i: (0, 0, 0))
    vec = pl.BlockSpec((1, c), lambda b, i: (0, 0))
    return pl.pallas_call(
        functools.partial(_conv_kernel, tt=tt, rc=rc),
        grid=(nb, t // tt),
        in_specs=[
            pl.BlockSpec((None, tt, c), lambda b, i: (b, i, 0)),
            pl.BlockSpec((None, HIST_ROWS, c), hist_map),
            pl.BlockSpec((None, tt, c), lambda b, i: (b, i, 0)),
            pl.BlockSpec((CONV_WIDTH * SUBLANE, c), lambda b, i: (0, 0)),
            vec, vec, vec,
        ],
        out_specs=pl.BlockSpec((None, tt, c), lambda b, i: (b, i, 0)),
        out_shape=jax.ShapeDtypeStruct((nb, t, c), BF16),
        scratch_shapes=[pltpu.VMEM((SUBLANE, HIST_ROWS + tt, c), F32), pltpu.VMEM((tt, c), F32)],
        compiler_params=pltpu.CompilerParams(
            dimension_semantics=("arbitrary", "arbitrary"), vmem_limit_bytes=VMEM_LIMIT),
        name="conv",
    )(u, hist, zc, cw, cb, lg, lb)


def _outproj_kernel(c_ref, a_ref, h_ref, wc_ref, wa_ref, fg_ref, o_ref, *, final):
    y = (jnp.dot(c_ref[...], wc_ref[...], preferred_element_type=F32)
         + jnp.dot(a_ref[...], wa_ref[...], preferred_element_type=F32))
    y = h_ref[...] + y
    if final:
        ms = jnp.mean(y * y, axis=-1, keepdims=True)
        y = y * lax.rsqrt(ms + NORM_EPS) * fg_ref[...]
    o_ref[...] = y


def _outproj(c, a, h, wo_all, layer, fg, tm, final):
    m, d = h.shape
    assert m % tm == 0 and D_CONV == D_ATT
    half = lambda: pl.BlockSpec((tm, D_CONV), lambda i: (i, 0))
    full = lambda: pl.BlockSpec((tm, d), lambda i: (i, 0))
    wspec = lambda part: pl.BlockSpec((None, D_CONV, d), lambda i: (layer, part, 0))
    return pl.pallas_call(
        functools.partial(_outproj_kernel, final=final),
        grid=(m // tm,),
        in_specs=[half(), half(), full(), wspec(0), wspec(1), pl.BlockSpec((1, d), lambda i: (0, 0))],
        out_specs=full(),
        out_shape=jax.ShapeDtypeStruct((m, d), F32),
        compiler_params=pltpu.CompilerParams(
            dimension_semantics=("arbitrary",), vmem_limit_bytes=VMEM_LIMIT),
        name="outproj",
    )(c, a, h, wo_all, wo_all, fg)


def _bias_kernel(bucket_ref, rb_ref, o_ref, *, n_near, tq):
    h = pl.program_id(0)
    far = rb_ref[FAR_BUCKET, h]
    for j in range(n_near):
        def body(r, carry):
            r0 = pl.multiple_of(r * SUBLANE, SUBLANE)
            bk = bucket_ref[j, pl.ds(r0, SUBLANE), :]
            acc = jnp.zeros(bk.shape, F32)
            for b in range(N_BUCKETS):
                acc = jnp.where(bk == b, rb_ref[b, h], acc)
            o_ref[j, pl.ds(r0, SUBLANE), :] = acc - far
            return carry
        lax.fori_loop(0, tq // SUBLANE, body, 0)


def _bias_tiles(bucket, rel_bias):
    n_near, tq, kb = bucket.shape
    return pl.pallas_call(
        functools.partial(_bias_kernel, n_near=n_near, tq=tq),
        grid=(N_HEADS,),
        in_specs=[pl.BlockSpec((n_near, tq, kb), lambda h: (0, 0, 0)),
                  pl.BlockSpec(memory_space=pltpu.SMEM)],
        out_specs=pl.BlockSpec((None, n_near, tq, kb), lambda h: (h, 0, 0, 0)),
        out_shape=jax.ShapeDtypeStruct((N_HEADS, n_near, tq, kb), F32),
        compiler_params=pltpu.CompilerParams(dimension_semantics=("arbitrary",)),
        name="bias_tiles",
    )(bucket, rel_bias)


def _t5_bucket(rel):
    half = N_BUCKETS // 2
    max_exact = half // 2
    n = jnp.abs(rel)
    large = max_exact + (jnp.log(jnp.maximum(n, 1).astype(jnp.float32) / max_exact)
                         / math.log(MAX_DISTANCE / max_exact) * (half - max_exact)).astype(jnp.int32)
    large = jnp.minimum(large, half - 1)
    return jnp.where(rel > 0, half, 0) + jnp.where(n < max_exact, n, large)


class _AttnCfg:
    def __init__(self, tq, n_rows, n_qt, s_pad, lane_off, n_pos, q_pos0, near_step, near_base, n_near, top_k,
                 edge_block, main_shift, n_main, dims_major=False):
        self.dims_major = dims_major
        self.edge_block, self.main_shift, self.n_main = edge_block, main_shift, n_main
        assert all(kb == edge_block or 0 <= kb - main_shift < n_main for kb in range(s_pad // KEY_BLOCK))
        self.tq, self.n_rows, self.n_qt, self.s_pad, self.lane_off = tq, n_rows, n_qt, s_pad, lane_off
        self.n_pos, self.q_pos0 = n_pos, q_pos0
        self.near_step, self.near_base, self.n_near, self.top_k = near_step, near_base, n_near, top_k
        self.rel0 = near_base * KEY_BLOCK - lane_off - q_pos0
        assert s_pad % KEY_BLOCK == 0 and tq % LANE == 0 and n_rows <= tq and n_rows % (2 * SUBLANE) == 0
        assert n_rows == tq or n_qt == 1
        assert near_step * KEY_BLOCK == tq or n_qt == 1
        assert (near_base == 0 and near_step == 0) or 1 - self.rel0 >= MAX_DISTANCE
        assert (near_base + near_step * (n_qt - 1) + n_near) * KEY_BLOCK <= s_pad

    def bucket_table(self):
        j = jnp.arange(self.n_near, dtype=jnp.int32)[:, None, None]
        r = jnp.arange(self.n_rows, dtype=jnp.int32)[None, :, None]
        c = jnp.arange(KEY_BLOCK, dtype=jnp.int32)[None, None, :]
        rel = self.rel0 + KEY_BLOCK * j + c - r
        return _t5_bucket(lax.optimization_barrier(rel))


def _chunk_of(pos):
    return jnp.where(pos < N_META, 0, 1 + ((pos - N_META) >> CHUNK_SHIFT))


def _for_blocks(n, fn, group=2):
    def body(kg, carry):
        for r in range(group):
            fn(group * kg + r)
        return carry

    if isinstance(n, int):
        lax.fori_loop(0, n // group, body, 0)
        for kb in range(n - n % group, n):
            fn(kb)
        return
    lax.fori_loop(0, n >> (group.bit_length() - 1), body, 0)
    part = group // 2
    while part:
        start = n & ~(2 * part - 1)

        @pl.when((n & part) != 0)
        def _(start=start, part=part):
            for r in range(part):
                fn(start + r)

        part //= 2


def _attn_kernel(q_ref, qi_ref, w_ref, za_ref, k_ref, v_ref, kx_ref, ke_ref, ve_ref, kxe_ref,
                 bias_ref, tri_ref, o_ref,
                 qm_ref, qim_ref, wb_ref, keys_ref, madd_ref, s_ref, mx_ref, l_ref, acc_ref, ties_ref,
                 hi_ref, lo_ref, sel_ref, *, cfg):
    tq, rq = cfg.tq, cfg.n_rows
    i = pl.program_id(1)
    near0 = cfg.near_base + cfg.near_step * i if cfg.near_step else cfg.near_base
    nkb = near0 + cfg.n_near
    lane = lax.broadcasted_iota(jnp.int32, (rq, LANE), 1)
    row = lax.broadcasted_iota(jnp.int32, (rq, LANE), 0)
    low_half = lane < HEAD_DIM
    q_pos_base = cfg.q_pos0 + i * tq
    q_chunk = _chunk_of(q_pos_base + row)

    def admissible(blk):
        kpos = blk * LANE + lane - cfg.lane_off
        return (kpos >= 0) & (kpos < cfg.n_pos) & (_chunk_of(kpos) <= q_chunk)

    def key_operand(main_ref, edge_ref, fs, kb):
        def main(idx):
            ks = pl.ds(pl.multiple_of(idx * KEY_BLOCK, KEY_BLOCK), KEY_BLOCK)
            return (main_ref[fs, ks] if cfg.dims_major else main_ref[ks, fs]).astype(BF16)

        edge = (edge_ref[fs, :] if cfg.dims_major else edge_ref[:, fs]).astype(BF16)
        if isinstance(kb, int):
            return edge if kb == cfg.edge_block else main(kb - cfg.main_shift)
        if cfg.n_main == 0:
            return edge
        return jnp.where(kb == cfg.edge_block, edge, main(jnp.clip(kb - cfg.main_shift, 0, cfg.n_main - 1)))

    def dot_keys(lhs, rhs):
        if cfg.dims_major:
            return jnp.dot(lhs, rhs, preferred_element_type=F32)
        return lax.dot_general(lhs, rhs, NT_DIMS, preferred_element_type=F32)

    def dot_values(lhs, rhs):
        if cfg.dims_major:
            return lax.dot_general(lhs, rhs, NT_DIMS, preferred_element_type=F32)
        return jnp.dot(lhs, rhs, preferred_element_type=F32)

    all_feat = slice(0, LANE)
    pair_feat = lambda p: slice(p * LANE, (p + 1) * LANE)

    zero = jnp.zeros((rq, LANE), BF16)
    for p in range(N_PAIRS):
        sl = slice(p * LANE, (p + 1) * LANE)
        qp = q_ref[:, sl]
        qm_ref[p, 0:rq, :] = jnp.where(low_half, qp, zero)
        qm_ref[p, rq:2 * rq, :] = jnp.where(low_half, zero, qp)
        qip = qi_ref[:, sl]
        qim_ref[p, 0:rq, :] = jnp.where(low_half, qip, zero)
        qim_ref[p, rq:2 * rq, :] = jnp.where(low_half, zero, qip)
    w_scale = (IDX_HEADS ** -0.5) * (IDX_DIM ** -0.5)
    for h in range(IDX_HEADS):
        col = IDX_DIM + h
        wb_ref[h] = jnp.broadcast_to(w_ref[:, col:col + 1] * w_scale, (rq, LANE))

    no_query = jnp.full((tq - rq, LANE), NEG, F32) if rq < tq else None

    def index_block(kb):
        kx = key_operand(kx_ref, kxe_ref, all_feat, kb)
        acc = [jnp.zeros((rq, LANE), F32) for _ in range(HALF)]
        for p in range(N_PAIRS):
            d = jnp.maximum(dot_keys(qim_ref[p], kx), 0.0)
            for hf in range(HALF):
                cs = slice(hf * LANE, (hf + 1) * LANE)
                acc[hf] = acc[hf] + wb_ref[2 * p] * d[0:rq, cs] + wb_ref[2 * p + 1] * d[rq:2 * rq, cs]
        for hf in range(HALF):
            blk = kb * HALF + hf
            sc = jnp.where(admissible(blk), acc[hf] + 0.0, NEG)
            if no_query is not None:
                sc = jnp.concatenate([sc, no_query], axis=0)
            bits = pltpu.bitcast(sc.T, jnp.int32)
            key = jnp.where(bits < 0, bits ^ jnp.int32(0x7FFFFFFF), bits)
            rows = pl.ds(pl.multiple_of(blk * LANE, LANE), LANE)
            keys_ref[rows, :] = key
            hi_ref[rows, :] = (key >> HALF_BITS).astype(jnp.int16)
            lo_ref[rows, :] = ((key & LOW_MASK) - HALF_BIAS).astype(jnp.int16)

    _for_blocks(nkb, index_block)

    kf = jnp.float32(cfg.top_k)
    n_part = 4
    packed_rows = 2 * SUBLANE

    def count16(ref, pred):
        def body(kb, parts):
            chunk = ref[pl.ds(pl.multiple_of(kb * KEY_BLOCK, KEY_BLOCK), KEY_BLOCK), :]
            parts = list(parts)
            for j in range(KEY_BLOCK // packed_rows):
                kk = chunk[j * packed_rows:(j + 1) * packed_rows, :]
                parts[j % n_part] = parts[j % n_part] + jnp.where(pred(kk), jnp.int16(1), jnp.int16(0))
            return tuple(parts)
        parts = lax.fori_loop(0, nkb, body,
                              tuple(jnp.zeros((packed_rows, tq), jnp.int16) for _ in range(n_part)))
        tot = ((parts[0] + parts[1]) + (parts[2] + parts[3])).astype(jnp.int32)
        return jnp.broadcast_to(jnp.sum(tot, axis=0, keepdims=True), (SUBLANE, tq)).astype(F32)

    def pack16(v):
        return jnp.concatenate([v, v], axis=0).astype(jnp.int16)

    def bisect16(ref, target):
        def step(it, lo):
            cand = lo + lax.shift_left(jnp.int32(1), HALF_BITS - 1 - it)
            cand16 = pack16(cand)
            return jnp.where(count16(ref, lambda kk: kk >= cand16) >= target, cand, lo)
        return lax.fori_loop(0, HALF_BITS, step, jnp.full((SUBLANE, tq), -HALF_BIAS, jnp.int32))

    hi_thr = bisect16(hi_ref, kf)
    hi_thr16 = pack16(hi_thr)
    above = count16(hi_ref, lambda kk: kk > hi_thr16)

    def low_keys(c_i, carry):
        rows = pl.ds(pl.multiple_of(c_i * LANE, LANE), LANE)
        hi, lo = hi_ref[rows, :], lo_ref[rows, :]
        for j in range(LANE // packed_rows):
            rs = slice(j * packed_rows, (j + 1) * packed_rows)
            sel_ref[pl.ds(pl.multiple_of(c_i * LANE, LANE) + j * packed_rows, packed_rows), :] = jnp.where(
                hi[rs, :] == hi_thr16, lo[rs, :], jnp.int16(-HALF_BIAS))
        return carry

    lax.fori_loop(0, nkb * HALF, low_keys, 0)
    lo_thr = bisect16(sel_ref, kf - above)
    thr8 = (hi_thr << HALF_BITS) | ((lo_thr + HALF_BIAS) & LOW_MASK)

    def count(pred):
        def body(c_i, parts):
            chunk = keys_ref[pl.ds(pl.multiple_of(c_i * LANE, LANE), LANE), :]
            parts = list(parts)
            for j in range(LANE // SUBLANE):
                kk = chunk[j * SUBLANE:(j + 1) * SUBLANE, :]
                parts[j % n_part] = parts[j % n_part] + jnp.where(pred(kk), 1.0, 0.0)
            return tuple(parts)
        parts = lax.fori_loop(0, nkb * HALF, body, tuple(jnp.zeros((SUBLANE, tq), F32) for _ in range(n_part)))
        tot = (parts[0] + parts[1]) + (parts[2] + parts[3])
        return jnp.broadcast_to(jnp.sum(tot, axis=0, keepdims=True), (SUBLANE, tq))

    need8 = kf - count(lambda kk: kk > thr8)
    thr = thr8[0:1, :]
    need = need8[0:1, :]

    key_row = lax.broadcasted_iota(jnp.int32, (KEY_BLOCK, tq), 0)
    q_chunk_t = _chunk_of(q_pos_base + lax.broadcasted_iota(jnp.int32, (KEY_BLOCK, tq), 1))

    ties_ref[...] = jnp.zeros((SUBLANE, tq), F32)

    def mask_block(kb):
        r0 = pl.multiple_of(kb * KEY_BLOCK, KEY_BLOCK)
        kk = keys_ref[pl.ds(r0, KEY_BLOCK), :]
        eq = kk == thr
        eqf = jnp.where(eq, 1.0, 0.0)
        seen = ties_ref[...]
        rank = seen[0:1, :] + jnp.dot(tri_ref[...], eqf.astype(BF16), preferred_element_type=F32)
        sel = jnp.where(eq, jnp.where(rank <= need, 0.0, NEG), jnp.where(kk > thr, 0.0, NEG))
        kpos = r0 + key_row - cfg.lane_off
        ok = (kpos >= 0) & (kpos < cfg.n_pos) & (_chunk_of(kpos) <= q_chunk_t)
        madd_ref[kb] = jnp.where(ok, sel, NEG).T[0:rq]
        ties_ref[...] = seen + jnp.sum(eqf, axis=0, keepdims=True)

    _for_blocks(nkb, mask_block)

    def logits(g, p, kb, near_j):
        s = dot_keys(qm_ref[p], key_operand(k_ref, ke_ref, pair_feat(p), kb))
        ma = madd_ref[kb]
        if near_j is None:
            add = jnp.concatenate([ma, ma], axis=0)
        else:
            add = jnp.concatenate([ma + bias_ref[2 * p, near_j], ma + bias_ref[2 * p + 1, near_j]], axis=0)
        s = s + add
        s_ref[g, kb] = s
        mx_ref[g] = jnp.maximum(mx_ref[g], jnp.maximum(s[:, 0:LANE], s[:, LANE:KEY_BLOCK]))

    def weighted_values(g, p, kb):
        m = mx_ref[g]
        pe = jnp.exp(s_ref[g, kb] - jnp.concatenate([m] * HALF, axis=1))
        l_ref[g] = l_ref[g] + (pe[:, 0:LANE] + pe[:, LANE:KEY_BLOCK])
        acc_ref[g] = acc_ref[g] + dot_values(pe.astype(BF16), key_operand(v_ref, ve_ref, pair_feat(p), kb))

    for stage in range(N_PAIRS + 1):
        p_a = stage if stage < N_PAIRS else None
        p_b = stage - 1 if stage > 0 else None

        def both(kb, near_j, p_a=p_a, p_b=p_b):
            if p_a is not None:
                logits(p_a % 2, p_a, kb, near_j)
            if p_b is not None:
                weighted_values(p_b % 2, p_b, kb)

        if p_a is not None:
            mx_ref[p_a % 2] = jnp.full((2 * rq, LANE), M_INIT, F32)

        _for_blocks(near0, lambda kb, both=both: both(kb, None), group=4)
        for j in range(cfg.n_near):
            both(near0 + j, j)

        if p_a is not None:
            g = p_a % 2
            mx_ref[g] = jnp.broadcast_to(jnp.max(mx_ref[g], axis=1, keepdims=True), (2 * rq, LANE))
            l_ref[g] = jnp.zeros((2 * rq, LANE), F32)
            acc_ref[g] = jnp.zeros((2 * rq, LANE), F32)
        if p_b is not None:
            g = p_b % 2
            o = acc_ref[g] / jnp.sum(l_ref[g], axis=1, keepdims=True)
            o = jnp.where(low_half, o[0:rq], o[rq:2 * rq])
            z = za_ref[:, p_b * LANE:(p_b + 1) * LANE].astype(F32)
            o_ref[:, p_b * LANE:(p_b + 1) * LANE] = (o * _silu(z)).astype(BF16)


def _attention(cfg, q, qi, w, za, mains, edges, bias, tri, main_lead=()):
    nb, t, _ = q.shape
    tq, rq, s_pad = cfg.tq, cfg.n_rows, cfg.s_pad
    assert t == rq * cfg.n_qt
    qspec = lambda width: pl.BlockSpec((None, rq, width), lambda b, i: (b, i, 0))

    def kspec(a, lead, buffers):
        per_batch = a.shape[len(lead)] == nb
        assert per_batch or a.shape[len(lead)] == 1
        return pl.BlockSpec((None,) * (len(lead) + 1) + a.shape[-2:],
                            lambda b, i: lead + ((b if per_batch else 0), 0, 0),
                            pipeline_mode=pl.Buffered(buffers if per_batch and nb > 1 else 1))

    key_axis = -1 if cfg.dims_major else -2
    for a in mains:
        assert a.shape[key_axis] >= cfg.n_main * KEY_BLOCK
    for a in edges:
        assert a.shape[key_axis] == KEY_BLOCK
    nkb_max = s_pad // KEY_BLOCK
    return pl.pallas_call(
        functools.partial(_attn_kernel, cfg=cfg),
        grid=(nb, cfg.n_qt),
        in_specs=[
            qspec(D_ATT), qspec(D_ATT), qspec(LANE), qspec(D_ATT),
            *[kspec(a, tuple(main_lead), 2 if a.dtype == BF16 else 1) for a in mains],
            *[kspec(a, (), 1) for a in edges],
            pl.BlockSpec((N_HEADS, cfg.n_near, rq, KEY_BLOCK), lambda b, i: (0, 0, 0, 0),
                         pipeline_mode=pl.Buffered(1)),
            pl.BlockSpec((KEY_BLOCK, KEY_BLOCK), lambda b, i: (0, 0)),
        ],
        out_specs=qspec(D_ATT),
        out_shape=jax.ShapeDtypeStruct((nb, t, D_ATT), BF16),
        scratch_shapes=[
            pltpu.VMEM((N_PAIRS, 2 * rq, LANE), BF16),
            pltpu.VMEM((N_PAIRS, 2 * rq, LANE), BF16),
            pltpu.VMEM((IDX_HEADS, rq, LANE), F32),
            pltpu.VMEM((s_pad, tq), jnp.int32),
            pltpu.VMEM((nkb_max, rq, KEY_BLOCK), F32),
            pltpu.VMEM((PAIR_BUFFERS, nkb_max, 2 * rq, KEY_BLOCK), F32),
            pltpu.VMEM((PAIR_BUFFERS, 2 * rq, LANE), F32),
            pltpu.VMEM((PAIR_BUFFERS, 2 * rq, LANE), F32),
            pltpu.VMEM((PAIR_BUFFERS, 2 * rq, LANE), F32),
            pltpu.VMEM((SUBLANE, tq), F32),
            pltpu.VMEM((s_pad, tq), jnp.int16),
            pltpu.VMEM((s_pad, tq), jnp.int16),
            pltpu.VMEM((s_pad, tq), jnp.int16),
        ],
        compiler_params=pltpu.CompilerParams(
            dimension_semantics=("arbitrary", "arbitrary"), vmem_limit_bytes=VMEM_LIMIT),
        name="attention",
    )(q, qi, w, za, *mains, *edges, bias, tri)


def _edge_block(cfg, parts, lead_zeros):
    key_axis = 2 if cfg.dims_major else 1
    dtype = parts[0].dtype
    parts = [a.astype(dtype) for a in parts]

    def zeros(n):
        shape = list(parts[0].shape)
        shape[key_axis] = n
        return [jnp.zeros(shape, dtype)] if n else []

    n_end = KEY_BLOCK - lead_zeros - sum(a.shape[key_axis] for a in parts)
    return jnp.concatenate(zeros(lead_zeros) + parts + zeros(n_end), axis=key_axis)


def _attend(cfg, q, qi, tail, za, mains, edges, bias, tri, main_lead=()):
    rows = lambda a: jnp.pad(a, ((0, 0), (0, cfg.n_rows * cfg.n_qt - a.shape[1]), (0, 0)))
    out = _attention(cfg, rows(q), rows(qi), rows(tail), rows(za), mains, edges, bias, tri, main_lead)
    return out[:, :q.shape[1]]


def _forward(x_prompt, x_sample, cache_k, cache_v, cache_kidx, state_conv, meta_tokens,
             norm_g, w_in, conv_w, conv_b, conv_ln_g, conv_ln_b, w_out, rel_bias, final_g,
             *, tq_frames=256, tm_frames=512):
    depth = w_in.shape[0]
    bp, seq, d = x_prompt.shape
    bs, dec, _ = x_sample.shape
    past = cache_k.shape[2] - N_META
    assert seq % tq_frames == 0 and seq % CHUNK == 0 and (bp * seq) % tm_frames == 0

    n_qt = seq // tq_frames
    cfg_f = _AttnCfg(tq=tq_frames, n_rows=tq_frames, n_qt=n_qt, s_pad=KEY_BLOCK + seq,
                     lane_off=KEY_BLOCK - N_META, n_pos=N_META + seq, q_pos0=N_META,
                     near_step=tq_frames // KEY_BLOCK, near_base=0,
                     n_near=tq_frames // KEY_BLOCK + 1, top_k=min(TOPK_MAX, seq // 4),
                     edge_block=0, main_shift=1, n_main=seq // KEY_BLOCK)
    n_cache = N_META + past
    n_pos_s = n_cache + dec
    s_pad_s = -(-n_pos_s // KEY_BLOCK) * KEY_BLOCK
    last_s = s_pad_s // KEY_BLOCK - 1
    near_s = max(0, (n_cache - MAX_DISTANCE) // KEY_BLOCK)
    cfg_s = _AttnCfg(tq=LANE, n_rows=dec, n_qt=1, s_pad=s_pad_s, lane_off=0,
                     n_pos=n_pos_s, q_pos0=n_cache, near_step=0, near_base=near_s,
                     n_near=last_s - near_s + 1, top_k=min(TOPK_MAX, (past + dec) // 4),
                     edge_block=last_s, main_shift=0, n_main=last_s, dims_major=True)
    cfg_m = _AttnCfg(tq=LANE, n_rows=N_META, n_qt=1, s_pad=KEY_BLOCK, lane_off=KEY_BLOCK - N_META,
                     n_pos=N_META, q_pos0=0, near_step=0, near_base=0, n_near=1,
                     top_k=min(TOPK_MAX, seq // 4), edge_block=0, main_shift=1, n_main=0)
    assert dec <= LANE and n_cache >= last_s * KEY_BLOCK

    bias_f = _bias_tiles(cfg_f.bucket_table(), rel_bias)
    bias_s = _bias_tiles(cfg_s.bucket_table(), rel_bias)
    bias_m = _bias_tiles(cfg_m.bucket_table(), rel_bias)
    tri = (jnp.arange(KEY_BLOCK)[None, :] <= jnp.arange(KEY_BLOCK)[:, None]).astype(BF16)

    n_aux = bs * dec + N_META
    hf = x_prompt.reshape(bp * seq, d)
    haux = jnp.concatenate([x_sample.reshape(bs * dec, d), meta_tokens.astype(x_sample.dtype)], axis=0)

    w_all = w_in.astype(BF16)
    wo_all = w_out.astype(BF16)
    feat_major = lambda c: jnp.transpose(c, (0, 1, 3, 4, 2)).reshape(depth, bs, D_ATT, n_cache)
    ck_t, cv_t = feat_major(cache_k), feat_major(cache_v)
    cx_t = jnp.swapaxes(cache_kidx, 2, 3).astype(BF16)
    cx_t = jnp.concatenate([cx_t, cx_t], axis=2)
    outs = {name: [] for name in ("kp", "vp", "kip", "cp", "ks", "vs", "kis", "cs")}
    for l in range(depth):
        w_tail = jnp.pad(w_all[l, :, N_GROUPS * GROUP:], ((0, 0), (0, LANE - IDX_DIM - IDX_HEADS)))
        g = norm_g[l][None, :]
        cw = jnp.repeat(conv_w[l], SUBLANE, axis=0)
        cb, lg, lb = conv_b[l][None, :], conv_ln_g[l][None, :], conv_ln_b[l][None, :]
        final = l == depth - 1
        fg = final_g[None, :]

        uA, zcA, qA, kA, vA, zaA, qiA, tailA, kbA, vbA = _inproj(haux, g, w_all, l, w_tail, n_aux)
        uF, zcF, qF, kF, vF, zaF, qiF, tailF, kbF, vbF = _inproj(hf, g, w_all, l, w_tail, tm_frames)
        ns = bs * dec
        split = lambda a: (a[:ns].reshape(bs, dec, -1), a[ns:][None])
        uS, uM = split(uA); zcS, zcM = split(zcA); qS, qM = split(qA); kS, kM = split(kA)
        vS, vM = split(vA); zaS, zaM = split(zaA); qiS, qiM = split(qiA); tailS, tailM = split(tailA)
        kbS, kbM = split(kbA); vbS, vbM = split(vbA)
        b3 = lambda a: a.reshape(bp, seq, -1)
        uF, zcF, qF, kF, vF, zaF, qiF, tailF, kbF, vbF = map(
            b3, (uF, zcF, qF, kF, vF, zaF, qiF, tailF, kbF, vbF))

        zero_hist = jnp.zeros((1, HIST_ROWS, D_CONV), F32)
        cM = _conv(uM, zero_hist, zcM, cw, cb, lg, lb)
        hist_f = jnp.concatenate([jnp.zeros((1, HIST_ROWS - N_META, D_CONV), F32), uM], axis=1)
        cF = _conv(uF, hist_f, zcF, cw, cb, lg, lb)
        hist_s = jnp.pad(state_conv[l].astype(F32), ((0, 0), (HIST_PAD, 0), (0, 0)))
        cS = _conv(uS, hist_s, zcS, cw, cb, lg, lb)

        kidx = lambda tail: tail[..., :IDX_DIM]
        twice = lambda a: jnp.concatenate([a, a], axis=-1).astype(BF16)
        meta_edges = tuple(_edge_block(cfg_f, [a], cfg_f.lane_off) for a in (kbM, vbM, twice(kidx(tailM))))
        aM = _attend(cfg_m, qM, qiM, tailM, zaM, meta_edges, meta_edges, bias_m, tri)
        aF = _attend(cfg_f, qF, qiF, tailF, zaF, (kbF, vbF, twice(kidx(tailF))), meta_edges, bias_f, tri)
        tr = lambda a: jnp.swapaxes(a, 1, 2)
        lo = cfg_s.edge_block * KEY_BLOCK
        sample_edges = tuple(_edge_block(cfg_s, [c[l, :, :, lo:], tr(new)], 0)
                             for c, new in ((ck_t, kS), (cv_t, vS), (cx_t, twice(kidx(tailS)))))
        aS = _attend(cfg_s, qS, qiS, tailS, zaS, (ck_t, cv_t, cx_t), sample_edges, bias_s, tri, main_lead=(l,))

        cAux = jnp.concatenate([cS.reshape(ns, D_CONV), cM[0]], axis=0)
        aAux = jnp.concatenate([aS.reshape(ns, D_ATT), aM[0]], axis=0)
        haux = _outproj(cAux, aAux, haux, wo_all, l, fg, n_aux, final)
        hf = _outproj(cF.reshape(bp * seq, D_CONV), aF.reshape(bp * seq, D_ATT), hf, wo_all, l, fg,
                      tm_frames, final)

        with_meta = lambda m, f: jnp.concatenate([jnp.broadcast_to(m, (bp,) + m.shape[1:]), f], axis=1)
        outs["kp"].append(with_meta(kM, kF).reshape(bp, N_META + seq, N_HEADS, HEAD_DIM))
        outs["vp"].append(with_meta(vM, vF).reshape(bp, N_META + seq, N_HEADS, HEAD_DIM))
        outs["kip"].append(with_meta(kidx(tailM), kidx(tailF)))
        u_ext_p = jnp.concatenate([jnp.zeros((bp, CONV_WIDTH - 1, D_CONV), F32),
                                   jnp.broadcast_to(uM, (bp, N_META, D_CONV)), uF], axis=1)
        outs["cp"].append(u_ext_p[:, -(CONV_WIDTH - 1):])
        outs["ks"].append(kS.reshape(bs, dec, N_HEADS, HEAD_DIM))
        outs["vs"].append(vS.reshape(bs, dec, N_HEADS, HEAD_DIM))
        outs["kis"].append(kidx(tailS))
        u_ext_s = jnp.concatenate([state_conv[l].astype(F32), uS], axis=1)
        outs["cs"].append(u_ext_s[:, -(CONV_WIDTH - 1):])

    y_prompt = hf.reshape(bp, seq, d)
    y_sample = haux[:bs * dec].reshape(bs, dec, d)
    st = lambda name: jnp.stack(outs[name])
    return (y_prompt, y_sample, st("kp"), st("vp"), st("kip"), st("cp"),
            st("ks"), st("vs"), st("kis"), st("cs"))


def kernel(x_prompt, x_sample, cache_k, cache_v, cache_kidx, state_conv, meta_tokens, norm_g, w_in,
           conv_w, conv_b, conv_ln_g, conv_ln_b, w_out, rel_bias, final_g):
    return _forward(x_prompt, x_sample, cache_k, cache_v, cache_kidx, state_conv, meta_tokens,
                    norm_g, w_in, conv_w, conv_b, conv_ln_g, conv_ln_b, w_out, rel_bias, final_g)
```

```python
import functools
import math

import jax
import jax.numpy as jnp
from jax import lax
from jax.experimental import pallas as pl
from jax.experimental.pallas import tpu as pltpu

N_META = 16
CHUNK = 64
CHUNK_SHIFT = 6
D_CONV = 1024
CONV_WIDTH = 31
N_HEADS = 16
HEAD_DIM = 64
D_ATT = N_HEADS * HEAD_DIM
IDX_HEADS = 16
IDX_DIM = 64
TOPK_MAX = 256
N_BUCKETS = 32
MAX_DISTANCE = 128
NORM_EPS = 1e-6
NEG = -1e30

LANE = 128
SUBLANE = 8
KEY_BLOCK = 256
HALF = KEY_BLOCK // LANE
N_PAIRS = N_HEADS // 2
PAIR_BUFFERS = 2
HIST_ROWS = 32
HIST_PAD = HIST_ROWS - (CONV_WIDTH - 1)
GROUP = 1024
N_GROUPS = 8
VMEM_LIMIT = 56 * 1024 * 1024
INT_MIN = -2147483648
HALF_BITS = 16
LOW_MASK = (1 << HALF_BITS) - 1
HALF_BIAS = 1 << (HALF_BITS - 1)
M_INIT = -1e29
FAR_BUCKET = N_BUCKETS // 2 - 1

F32 = jnp.float32
BF16 = jnp.bfloat16
NT_DIMS = (((1,), (1,)), ((), ()))


def _silu(x):
    return x * jax.nn.sigmoid(x)


def _inproj_kernel(x_ref, g_ref, wm_ref, wt_ref,
                   u_ref, zc_ref, q_ref, k_ref, v_ref, za_ref, qi_ref, tail_ref, kb_ref, vb_ref,
                   xn_ref, a_ref):
    n = pl.program_id(1)
    j = pl.program_id(2)

    @pl.when(n == 0)
    def _():
        x = x_ref[...]
        ms = jnp.mean(x * x, axis=-1, keepdims=True)
        xn_ref[j] = (x * lax.rsqrt(ms + NORM_EPS) * g_ref[...]).astype(BF16)

    def group():
        return jnp.dot(xn_ref[j], wm_ref[...], preferred_element_type=F32)

    @pl.when(n == 0)
    def _():
        a_ref[j] = group()

    @pl.when(n == 1)
    def _():
        u_ref[...] = a_ref[j] * jax.nn.sigmoid(group())

    @pl.when(n == 2)
    def _():
        zc_ref[...] = group().astype(BF16)

    @pl.when(n == 3)
    def _():
        q_ref[...] = (group() * (HEAD_DIM ** -0.5)).astype(BF16)

    @pl.when(n == 4)
    def _():
        y = group()
        k_ref[...] = y
        kb_ref[...] = y.astype(BF16)

    @pl.when(n == 5)
    def _():
        y = group()
        v_ref[...] = y
        vb_ref[...] = y.astype(BF16)

    @pl.when(n == 6)
    def _():
        za_ref[...] = group().astype(BF16)

    @pl.when(n == 7)
    def _():
        qi_ref[...] = group().astype(BF16)
        tail_ref[...] = jnp.dot(xn_ref[j], wt_ref[...], preferred_element_type=F32)


def _inproj(x2d, g, w_all, layer, w_tail, tm):
    m, d = x2d.shape
    assert m % tm == 0 and w_all.shape[2] >= N_GROUPS * GROUP
    n_tiles = m // tm
    share = 2 if n_tiles % 2 == 0 else 1
    last = n_tiles - 1
    row = lambda width, dtype: jax.ShapeDtypeStruct((m, width), dtype)
    out_shape = (row(GROUP, F32), row(GROUP, BF16), row(GROUP, BF16), row(GROUP, F32),
                 row(GROUP, F32), row(GROUP, BF16), row(GROUP, BF16), row(LANE, F32),
                 row(GROUP, BF16), row(GROUP, BF16))

    def tile_at(n_w):
        def index(i, n, j):
            first = i * share
            return jnp.where(n < n_w, jnp.maximum(first - 1, 0),
                             jnp.where(n == n_w, first + j, jnp.minimum(first + share - 1, last)))
        return index

    ospec = lambda width, n_w: pl.BlockSpec((tm, width), lambda i, n, j: (tile_at(n_w)(i, n, j), 0))
    return pl.pallas_call(
        _inproj_kernel,
        grid=(n_tiles // share, N_GROUPS, share),
        in_specs=[
            pl.BlockSpec((tm, d), lambda i, n, j: (tile_at(0)(i, n, j), 0)),
            pl.BlockSpec((1, d), lambda i, n, j: (0, 0)),
            pl.BlockSpec((None, d, GROUP), lambda i, n, j: (layer, 0, n)),
            pl.BlockSpec((d, LANE), lambda i, n, j: (0, 0)),
        ],
        out_specs=([ospec(GROUP, n_w) for n_w in range(1, N_GROUPS)] + [ospec(LANE, N_GROUPS - 1)]
                   + [ospec(GROUP, 4), ospec(GROUP, 5)]),
        out_shape=out_shape,
        scratch_shapes=[pltpu.VMEM((share, tm, d), BF16), pltpu.VMEM((share, tm, GROUP), F32)],
        compiler_params=pltpu.CompilerParams(
            dimension_semantics=("arbitrary", "arbitrary", "arbitrary"), vmem_limit_bytes=VMEM_LIMIT),
        name="inproj",
    )(x2d, g, w_all, w_tail)


def _conv_kernel(u_ref, hist_ref, zc_ref, cw_ref, cb_ref, lg_ref, lb_ref, o_ref, ext_ref, c_ref,
                 *, tt, rc):
    t = pl.program_id(1)
    n_hist = CONV_WIDTH - 1

    for b in range(SUBLANE):
        keep = n_hist - b

        @pl.when(t == 0)
        def _(b=b, keep=keep):
            ext_ref[b, 0:keep, :] = hist_ref[HIST_PAD + b:HIST_ROWS, :]

        @pl.when(t > 0)
        def _(b=b, keep=keep):
            ext_ref[b, 0:keep, :] = ext_ref[b, tt:tt + keep, :]

    for b in range(SUBLANE):
        ext_ref[b, n_hist - b:n_hist - b + tt, :] = u_ref[...]

    half_c = D_CONV // 2

    def taps(r, carry):
        r0 = pl.multiple_of(r * (2 * SUBLANE), 2 * SUBLANE)
        for c0 in (0, half_c):
            cs = slice(c0, c0 + half_c)
            acc = [jnp.broadcast_to(cb_ref[:, cs], (SUBLANE, half_c)) for _ in range(2)]
            for b in range(SUBLANE):
                n_a = (CONV_WIDTH - 1 - b) // SUBLANE + 1
                win = ext_ref[b, pl.ds(r0, (n_a + 1) * SUBLANE), cs]
                for a in range(n_a):
                    j = a * SUBLANE + b
                    w8 = cw_ref[j * SUBLANE:(j + 1) * SUBLANE, cs]
                    for g in range(2):
                        lo = (a + g) * SUBLANE
                        acc[g] = acc[g] + w8 * win[lo:lo + SUBLANE, :]
            for g in range(2):
                c_ref[pl.ds(r0 + g * SUBLANE, SUBLANE), cs] = acc[g]
        return carry

    lax.fori_loop(0, tt // (2 * SUBLANE), taps, 0)

    def norm_gate(r, carry):
        r0 = pl.multiple_of(r * rc, rc)
        c = c_ref[pl.ds(r0, rc), :]
        mu = jnp.mean(c, axis=-1, keepdims=True)
        xc = c - mu
        var = jnp.mean(xc * xc, axis=-1, keepdims=True)
        y = xc * lax.rsqrt(var + NORM_EPS) * lg_ref[...] + lb_ref[...]
        z = zc_ref[pl.ds(r0, rc), :].astype(F32)
        o_ref[pl.ds(r0, rc), :] = (_silu(y) * _silu(z)).astype(BF16)
        return carry

    lax.fori_loop(0, tt // rc, norm_gate, 0)


def _conv(u, hist, zc, cw, cb, lg, lb):
    nb, t, c = u.shape
    tt = min(t, 512)
    rc = min(tt, 64)
    assert t % tt == 0 and tt % rc == 0 and tt % (2 * SUBLANE) == 0 and c == D_CONV
    hist_map = (lambda b, i: (b, 0, 0)) if hist.shape[0] == nb else (lambda b, i: (0, 0, 0))
    vec = pl.BlockSpec((1, c), lambda b, i: (0, 0))
    return pl.pallas_call(
        functools.partial(_conv_kernel, tt=tt, rc=rc),
        grid=(nb, t // tt),
        in_specs=[
            pl.BlockSpec((None, tt, c), lambda b, i: (b, i, 0)),
            pl.BlockSpec((None, HIST_ROWS, c), hist_map),
            pl.BlockSpec((None, tt, c), lambda b, i: (b, i, 0)),
            pl.BlockSpec((CONV_WIDTH * SUBLANE, c), lambda b, i: (0, 0)),
            vec, vec, vec,
        ],
        out_specs=pl.BlockSpec((None, tt, c), lambda b, i: (b, i, 0)),
        out_shape=jax.ShapeDtypeStruct((nb, t, c), BF16),
        scratch_shapes=[pltpu.VMEM((SUBLANE, HIST_ROWS + tt, c), F32), pltpu.VMEM((tt, c), F32)],
        compiler_params=pltpu.CompilerParams(
            dimension_semantics=("arbitrary", "arbitrary"), vmem_limit_bytes=VMEM_LIMIT),
        name="conv",
    )(u, hist, zc, cw, cb, lg, lb)


def _outproj_kernel(c_ref, a_ref, h_ref, wc_ref, wa_ref, fg_ref, o_ref, *, final):
    y = (jnp.dot(c_ref[...], wc_ref[...], preferred_element_type=F32)
         + jnp.dot(a_ref[...], wa_ref[...], preferred_element_type=F32))
    y = h_ref[...] + y
    if final:
        ms = jnp.mean(y * y, axis=-1, keepdims=True)
        y = y * lax.rsqrt(ms + NORM_EPS) * fg_ref[...]
    o_ref[...] = y


def _outproj(c, a, h, wo_all, layer, fg, tm, final):
    m, d = h.shape
    assert m % tm == 0 and D_CONV == D_ATT
    half = lambda: pl.BlockSpec((tm, D_CONV), lambda i: (i, 0))
    full = lambda: pl.BlockSpec((tm, d), lambda i: (i, 0))
    wspec = lambda part: pl.BlockSpec((None, D_CONV, d), lambda i: (layer, part, 0))
    return pl.pallas_call(
        functools.partial(_outproj_kernel, final=final),
        grid=(m // tm,),
        in_specs=[half(), half(), full(), wspec(0), wspec(1), pl.BlockSpec((1, d), lambda i: (0, 0))],
        out_specs=full(),
        out_shape=jax.ShapeDtypeStruct((m, d), F32),
        compiler_params=pltpu.CompilerParams(
            dimension_semantics=("arbitrary",), vmem_limit_bytes=VMEM_LIMIT),
        name="outproj",
    )(c, a, h, wo_all, wo_all, fg)


def _bias_kernel(bucket_ref, rb_ref, o_ref, *, n_near, tq):
    h = pl.program_id(0)
    far = rb_ref[FAR_BUCKET, h]
    for j in range(n_near):
        def body(r, carry):
            r0 = pl.multiple_of(r * SUBLANE, SUBLANE)
            bk = bucket_ref[j, pl.ds(r0, SUBLANE), :]
            acc = jnp.zeros(bk.shape, F32)
            for b in range(N_BUCKETS):
                acc = jnp.where(bk == b, rb_ref[b, h], acc)
            o_ref[j, pl.ds(r0, SUBLANE), :] = acc - far
            return carry
        lax.fori_loop(0, tq // SUBLANE, body, 0)


def _bias_tiles(bucket, rel_bias):
    n_near, tq, kb = bucket.shape
    return pl.pallas_call(
        functools.partial(_bias_kernel, n_near=n_near, tq=tq),
        grid=(N_HEADS,),
        in_specs=[pl.BlockSpec((n_near, tq, kb), lambda h: (0, 0, 0)),
                  pl.BlockSpec(memory_space=pltpu.SMEM)],
        out_specs=pl.BlockSpec((None, n_near, tq, kb), lambda h: (h, 0, 0, 0)),
        out_shape=jax.ShapeDtypeStruct((N_HEADS, n_near, tq, kb), F32),
        compiler_params=pltpu.CompilerParams(dimension_semantics=("arbitrary",)),
        name="bias_tiles",
    )(bucket, rel_bias)


def _t5_bucket(rel):
    half = N_BUCKETS // 2
    max_exact = half // 2
    n = jnp.abs(rel)
    large = max_exact + (jnp.log(jnp.maximum(n, 1).astype(jnp.float32) / max_exact)
                         / math.log(MAX_DISTANCE / max_exact) * (half - max_exact)).astype(jnp.int32)
    large = jnp.minimum(large, half - 1)
    return jnp.where(rel > 0, half, 0) + jnp.where(n < max_exact, n, large)


class _AttnCfg:
    def __init__(self, tq, n_rows, n_qt, s_pad, lane_off, n_pos, q_pos0, near_step, near_base, n_near, top_k,
                 edge_block, main_shift, n_main, dims_major=False):
        self.dims_major = dims_major
        self.edge_block, self.main_shift, self.n_main = edge_block, main_shift, n_main
        assert all(kb == edge_block or 0 <= kb - main_shift < n_main for kb in range(s_pad // KEY_BLOCK))
        self.tq, self.n_rows, self.n_qt, self.s_pad, self.lane_off = tq, n_rows, n_qt, s_pad, lane_off
        self.n_pos, self.q_pos0 = n_pos, q_pos0
        self.near_step, self.near_base, self.n_near, self.top_k = near_step, near_base, n_near, top_k
        self.rel0 = near_base * KEY_BLOCK - lane_off - q_pos0
        assert s_pad % KEY_BLOCK == 0 and tq % LANE == 0 and n_rows <= tq and n_rows % (2 * SUBLANE) == 0
        assert n_rows == tq or n_qt == 1
        assert near_step * KEY_BLOCK == tq or n_qt == 1
        assert (near_base == 0 and near_step == 0) or 1 - self.rel0 >= MAX_DISTANCE
        assert (near_base + near_step * (n_qt - 1) + n_near) * KEY_BLOCK <= s_pad

    def bucket_table(self):
        j = jnp.arange(self.n_near, dtype=jnp.int32)[:, None, None]
        r = jnp.arange(self.n_rows, dtype=jnp.int32)[None, :, None]
        c = jnp.arange(KEY_BLOCK, dtype=jnp.int32)[None, None, :]
        rel = self.rel0 + KEY_BLOCK * j + c - r
        return _t5_bucket(lax.optimization_barrier(rel))


def _chunk_of(pos):
    return jnp.where(pos < N_META, 0, 1 + ((pos - N_META) >> CHUNK_SHIFT))


def _for_blocks(n, fn, group=2):
    def body(kg, carry):
        for r in range(group):
            fn(group * kg + r)
        return carry

    if isinstance(n, int):
        lax.fori_loop(0, n // group, body, 0)
        for kb in range(n - n % group, n):
            fn(kb)
        return
    lax.fori_loop(0, n >> (group.bit_length() - 1), body, 0)
    part = group // 2
    while part:
        start = n & ~(2 * part - 1)

        @pl.when((n & part) != 0)
        def _(start=start, part=part):
            for r in range(part):
                fn(start + r)

        part //= 2


def _attn_kernel(q_ref, qi_ref, w_ref, za_ref, k_ref, v_ref, kx_ref, ke_ref, ve_ref, kxe_ref,
                 bias_ref, tri_ref, o_ref,
                 qm_ref, qim_ref, wb_ref, keys_ref, madd_ref, s_ref, mx_ref, l_ref, acc_ref, ties_ref,
                 hi_ref, lo_ref, sel_ref, *, cfg):
    tq, rq = cfg.tq, cfg.n_rows
    i = pl.program_id(1)
    near0 = cfg.near_base + cfg.near_step * i if cfg.near_step else cfg.near_base
    nkb = near0 + cfg.n_near
    lane = lax.broadcasted_iota(jnp.int32, (rq, LANE), 1)
    row = lax.broadcasted_iota(jnp.int32, (rq, LANE), 0)
    low_half = lane < HEAD_DIM
    q_pos_base = cfg.q_pos0 + i * tq
    q_chunk = _chunk_of(q_pos_base + row)

    def admissible(blk):
        kpos = blk * LANE + lane - cfg.lane_off
        return (kpos >= 0) & (kpos < cfg.n_pos) & (_chunk_of(kpos) <= q_chunk)

    def key_operand(main_ref, edge_ref, fs, kb):
        def main(idx):
            ks = pl.ds(pl.multiple_of(idx * KEY_BLOCK, KEY_BLOCK), KEY_BLOCK)
            return (main_ref[fs, ks] if cfg.dims_major else main_ref[ks, fs]).astype(BF16)

        edge = (edge_ref[fs, :] if cfg.dims_major else edge_ref[:, fs]).astype(BF16)
        if isinstance(kb, int):
            return edge if kb == cfg.edge_block else main(kb - cfg.main_shift)
        if cfg.n_main == 0:
            return edge
        return jnp.where(kb == cfg.edge_block, edge, main(jnp.clip(kb - cfg.main_shift, 0, cfg.n_main - 1)))

    def dot_keys(lhs, rhs):
        if cfg.dims_major:
            return jnp.dot(lhs, rhs, preferred_element_type=F32)
        return lax.dot_general(lhs, rhs, NT_DIMS, preferred_element_type=F32)

    def dot_values(lhs, rhs):
        if cfg.dims_major:
            return lax.dot_general(lhs, rhs, NT_DIMS, preferred_element_type=F32)
        return jnp.dot(lhs, rhs, preferred_element_type=F32)

    all_feat = slice(0, LANE)
    pair_feat = lambda p: slice(p * LANE, (p + 1) * LANE)

    zero = jnp.zeros((rq, LANE), BF16)
    for p in range(N_PAIRS):
        sl = slice(p * LANE, (p + 1) * LANE)
        qp = q_ref[:, sl]
        qm_ref[p, 0:rq, :] = jnp.where(low_half, qp, zero)
        qm_ref[p, rq:2 * rq, :] = jnp.where(low_half, zero, qp)
        qip = qi_ref[:, sl]
        qim_ref[p, 0:rq, :] = jnp.where(low_half, qip, zero)
        qim_ref[p, rq:2 * rq, :] = jnp.where(low_half, zero, qip)
    w_scale = (IDX_HEADS ** -0.5) * (IDX_DIM ** -0.5)
    for h in range(IDX_HEADS):
        col = IDX_DIM + h
        wb_ref[h] = jnp.broadcast_to(w_ref[:, col:col + 1] * w_scale, (rq, LANE))

    no_query = jnp.full((tq - rq, LANE), NEG, F32) if rq < tq else None

    def index_block(kb):
        kx = key_operand(kx_ref, kxe_ref, all_feat, kb)
        acc = [jnp.zeros((rq, LANE), F32) for _ in range(HALF)]
        for p in range(N_PAIRS):
            d = jnp.maximum(dot_keys(qim_ref[p], kx), 0.0)
            for hf in range(HALF):
                cs = slice(hf * LANE, (hf + 1) * LANE)
                acc[hf] = acc[hf] + wb_ref[2 * p] * d[0:rq, cs] + wb_ref[2 * p + 1] * d[rq:2 * rq, cs]
        for hf in range(HALF):
            blk = kb * HALF + hf
            sc = jnp.where(admissible(blk), acc[hf] + 0.0, NEG)
            if no_query is not None:
                sc = jnp.concatenate([sc, no_query], axis=0)
            bits = pltpu.bitcast(sc.T, jnp.int32)
            key = jnp.where(bits < 0, bits ^ jnp.int32(0x7FFFFFFF), bits)
            rows = pl.ds(pl.multiple_of(blk * LANE, LANE), LANE)
            keys_ref[rows, :] = key
            hi_ref[rows, :] = (key >> HALF_BITS).astype(jnp.int16)
            lo_ref[rows, :] = ((key & LOW_MASK) - HALF_BIAS).astype(jnp.int16)

    _for_blocks(nkb, index_block, group=4)

    kf = jnp.float32(cfg.top_k)
    n_part = 4
    packed_rows = 2 * SUBLANE

    def count16(ref, pred):
        def body(kb, parts):
            chunk = ref[pl.ds(pl.multiple_of(kb * KEY_BLOCK, KEY_BLOCK), KEY_BLOCK), :]
            parts = list(parts)
            for j in range(KEY_BLOCK // packed_rows):
                kk = chunk[j * packed_rows:(j + 1) * packed_rows, :]
                parts[j % n_part] = parts[j % n_part] + jnp.where(pred(kk), jnp.int16(1), jnp.int16(0))
            return tuple(parts)
        parts = lax.fori_loop(0, nkb, body,
                              tuple(jnp.zeros((packed_rows, tq), jnp.int16) for _ in range(n_part)))
        tot = ((parts[0] + parts[1]) + (parts[2] + parts[3])).astype(jnp.int32)
        return jnp.broadcast_to(jnp.sum(tot, axis=0, keepdims=True), (SUBLANE, tq)).astype(F32)

    def pack16(v):
        return jnp.concatenate([v, v], axis=0).astype(jnp.int16)

    def bisect16(ref, target):
        def step(it, lo):
            cand = lo + lax.shift_left(jnp.int32(1), HALF_BITS - 1 - it)
            cand16 = pack16(cand)
            return jnp.where(count16(ref, lambda kk: kk >= cand16) >= target, cand, lo)
        return lax.fori_loop(0, HALF_BITS, step, jnp.full((SUBLANE, tq), -HALF_BIAS, jnp.int32))

    hi_thr = bisect16(hi_ref, kf)
    hi_thr16 = pack16(hi_thr)
    above = count16(hi_ref, lambda kk: kk > hi_thr16)

    def low_keys(c_i, carry):
        rows = pl.ds(pl.multiple_of(c_i * LANE, LANE), LANE)
        hi, lo = hi_ref[rows, :], lo_ref[rows, :]
        for j in range(LANE // packed_rows):
            rs = slice(j * packed_rows, (j + 1) * packed_rows)
            sel_ref[pl.ds(pl.multiple_of(c_i * LANE, LANE) + j * packed_rows, packed_rows), :] = jnp.where(
                hi[rs, :] == hi_thr16, lo[rs, :], jnp.int16(-HALF_BIAS))
        return carry

    lax.fori_loop(0, nkb * HALF, low_keys, 0)
    lo_thr = bisect16(sel_ref, kf - above)
    thr8 = (hi_thr << HALF_BITS) | ((lo_thr + HALF_BIAS) & LOW_MASK)

    def count(pred):
        def body(c_i, parts):
            chunk = keys_ref[pl.ds(pl.multiple_of(c_i * LANE, LANE), LANE), :]
            parts = list(parts)
            for j in range(LANE // SUBLANE):
                kk = chunk[j * SUBLANE:(j + 1) * SUBLANE, :]
                parts[j % n_part] = parts[j % n_part] + jnp.where(pred(kk), 1.0, 0.0)
            return tuple(parts)
        parts = lax.fori_loop(0, nkb * HALF, body, tuple(jnp.zeros((SUBLANE, tq), F32) for _ in range(n_part)))
        tot = (parts[0] + parts[1]) + (parts[2] + parts[3])
        return jnp.broadcast_to(jnp.sum(tot, axis=0, keepdims=True), (SUBLANE, tq))

    need8 = kf - count(lambda kk: kk > thr8)
    thr = thr8[0:1, :]
    need = need8[0:1, :]

    key_row = lax.broadcasted_iota(jnp.int32, (KEY_BLOCK, tq), 0)
    q_chunk_t = _chunk_of(q_pos_base + lax.broadcasted_iota(jnp.int32, (KEY_BLOCK, tq), 1))

    ties_ref[...] = jnp.zeros((SUBLANE, tq), F32)

    def mask_block(kb):
        r0 = pl.multiple_of(kb * KEY_BLOCK, KEY_BLOCK)
        kk = keys_ref[pl.ds(r0, KEY_BLOCK), :]
        eq = kk == thr
        eqf = jnp.where(eq, 1.0, 0.0)
        seen = ties_ref[...]
        rank = seen[0:1, :] + jnp.dot(tri_ref[...], eqf.astype(BF16), preferred_element_type=F32)
        sel = jnp.where(eq, jnp.where(rank <= need, 0.0, NEG), jnp.where(kk > thr, 0.0, NEG))
        kpos = r0 + key_row - cfg.lane_off
        ok = (kpos >= 0) & (kpos < cfg.n_pos) & (_chunk_of(kpos) <= q_chunk_t)
        madd_ref[kb] = jnp.where(ok, sel, NEG).T[0:rq]
        ties_ref[...] = seen + jnp.sum(eqf, axis=0, keepdims=True)

    _for_blocks(nkb, mask_block)

    def logits(g, p, kb, near_j):
        s = dot_keys(qm_ref[p], key_operand(k_ref, ke_ref, pair_feat(p), kb))
        ma = madd_ref[kb]
        if near_j is None:
            add = jnp.concatenate([ma, ma], axis=0)
        else:
            add = jnp.concatenate([ma + bias_ref[2 * p, near_j], ma + bias_ref[2 * p + 1, near_j]], axis=0)
        s = s + add
        s_ref[g, kb] = s
        mx_ref[g] = jnp.maximum(mx_ref[g], jnp.maximum(s[:, 0:LANE], s[:, LANE:KEY_BLOCK]))

    def weighted_values(g, p, kb):
        m = mx_ref[g]
        pe = jnp.exp(s_ref[g, kb] - jnp.concatenate([m] * HALF, axis=1))
        l_ref[g] = l_ref[g] + (pe[:, 0:LANE] + pe[:, LANE:KEY_BLOCK])
        acc_ref[g] = acc_ref[g] + dot_values(pe.astype(BF16), key_operand(v_ref, ve_ref, pair_feat(p), kb))

    for stage in range(N_PAIRS + 1):
        p_a = stage if stage < N_PAIRS else None
        p_b = stage - 1 if stage > 0 else None

        def both(kb, near_j, p_a=p_a, p_b=p_b):
            if p_a is not None:
                logits(p_a % 2, p_a, kb, near_j)
            if p_b is not None:
                weighted_values(p_b % 2, p_b, kb)

        if p_a is not None:
            mx_ref[p_a % 2] = jnp.full((2 * rq, LANE), M_INIT, F32)

        _for_blocks(near0, lambda kb, both=both: both(kb, None), group=4)
        for j in range(cfg.n_near):
            both(near0 + j, j)

        if p_a is not None:
            g = p_a % 2
            mx_ref[g] = jnp.broadcast_to(jnp.max(mx_ref[g], axis=1, keepdims=True), (2 * rq, LANE))
            l_ref[g] = jnp.zeros((2 * rq, LANE), F32)
            acc_ref[g] = jnp.zeros((2 * rq, LANE), F32)
        if p_b is not None:
            g = p_b % 2
            o = acc_ref[g] / jnp.sum(l_ref[g], axis=1, keepdims=True)
            o = jnp.where(low_half, o[0:rq], o[rq:2 * rq])
            z = za_ref[:, p_b * LANE:(p_b + 1) * LANE].astype(F32)
            o_ref[:, p_b * LANE:(p_b + 1) * LANE] = (o * _silu(z)).astype(BF16)


def _attention(cfg, q, qi, w, za, mains, edges, bias, tri, main_lead=()):
    nb, t, _ = q.shape
    tq, rq, s_pad = cfg.tq, cfg.n_rows, cfg.s_pad
    assert t == rq * cfg.n_qt
    qspec = lambda width: pl.BlockSpec((None, rq, width), lambda b, i: (b, i, 0))

    def kspec(a, lead, buffers):
        per_batch = a.shape[len(lead)] == nb
        assert per_batch or a.shape[len(lead)] == 1
        return pl.BlockSpec((None,) * (len(lead) + 1) + a.shape[-2:],
                            lambda b, i: lead + ((b if per_batch else 0), 0, 0),
                            pipeline_mode=pl.Buffered(buffers if per_batch and nb > 1 else 1))

    key_axis = -1 if cfg.dims_major else -2
    for a in mains:
        assert a.shape[key_axis] >= cfg.n_main * KEY_BLOCK
    for a in edges:
        assert a.shape[key_axis] == KEY_BLOCK
    nkb_max = s_pad // KEY_BLOCK
    return pl.pallas_call(
        functools.partial(_attn_kernel, cfg=cfg),
        grid=(nb, cfg.n_qt),
        in_specs=[
            qspec(D_ATT), qspec(D_ATT), qspec(LANE), qspec(D_ATT),
            *[kspec(a, tuple(main_lead), 2 if a.dtype == BF16 else 1) for a in mains],
            *[kspec(a, (), 1) for a in edges],
            pl.BlockSpec((N_HEADS, cfg.n_near, rq, KEY_BLOCK), lambda b, i: (0, 0, 0, 0),
                         pipeline_mode=pl.Buffered(1)),
            pl.BlockSpec((KEY_BLOCK, KEY_BLOCK), lambda b, i: (0, 0)),
        ],
        out_specs=qspec(D_ATT),
        out_shape=jax.ShapeDtypeStruct((nb, t, D_ATT), BF16),
        scratch_shapes=[
            pltpu.VMEM((N_PAIRS, 2 * rq, LANE), BF16),
            pltpu.VMEM((N_PAIRS, 2 * rq, LANE), BF16),
            pltpu.VMEM((IDX_HEADS, rq, LANE), F32),
            pltpu.VMEM((s_pad, tq), jnp.int32),
            pltpu.VMEM((nkb_max, rq, KEY_BLOCK), F32),
            pltpu.VMEM((PAIR_BUFFERS, nkb_max, 2 * rq, KEY_BLOCK), F32),
            pltpu.VMEM((PAIR_BUFFERS, 2 * rq, LANE), F32),
            pltpu.VMEM((PAIR_BUFFERS, 2 * rq, LANE), F32),
            pltpu.VMEM((PAIR_BUFFERS, 2 * rq, LANE), F32),
            pltpu.VMEM((SUBLANE, tq), F32),
            pltpu.VMEM((s_pad, tq), jnp.int16),
            pltpu.VMEM((s_pad, tq), jnp.int16),
            pltpu.VMEM((s_pad, tq), jnp.int16),
        ],
        compiler_params=pltpu.CompilerParams(
            dimension_semantics=("arbitrary", "arbitrary"), vmem_limit_bytes=VMEM_LIMIT),
        name="attention",
    )(q, qi, w, za, *mains, *edges, bias, tri)


def _edge_block(cfg, parts, lead_zeros):
    key_axis = 2 if cfg.dims_major else 1
    dtype = parts[0].dtype
    parts = [a.astype(dtype) for a in parts]

    def zeros(n):
        shape = list(parts[0].shape)
        shape[key_axis] = n
        return [jnp.zeros(shape, dtype)] if n else []

    n_end = KEY_BLOCK - lead_zeros - sum(a.shape[key_axis] for a in parts)
    return jnp.concatenate(zeros(lead_zeros) + parts + zeros(n_end), axis=key_axis)


def _attend(cfg, q, qi, tail, za, mains, edges, bias, tri, main_lead=()):
    rows = lambda a: jnp.pad(a, ((0, 0), (0, cfg.n_rows * cfg.n_qt - a.shape[1]), (0, 0)))
    out = _attention(cfg, rows(q), rows(qi), rows(tail), rows(za), mains, edges, bias, tri, main_lead)
    return out[:, :q.shape[1]]


def _forward(x_prompt, x_sample, cache_k, cache_v, cache_kidx, state_conv, meta_tokens,
             norm_g, w_in, conv_w, conv_b, conv_ln_g, conv_ln_b, w_out, rel_bias, final_g,
             *, tq_frames=256, tm_frames=512):
    depth = w_in.shape[0]
    bp, seq, d = x_prompt.shape
    bs, dec, _ = x_sample.shape
    past = cache_k.shape[2] - N_META
    assert seq % tq_frames == 0 and seq % CHUNK == 0 and (bp * seq) % tm_frames == 0

    n_qt = seq // tq_frames
    cfg_f = _AttnCfg(tq=tq_frames, n_rows=tq_frames, n_qt=n_qt, s_pad=KEY_BLOCK + seq,
                     lane_off=KEY_BLOCK - N_META, n_pos=N_META + seq, q_pos0=N_META,
                     near_step=tq_frames // KEY_BLOCK, near_base=0,
                     n_near=tq_frames // KEY_BLOCK + 1, top_k=min(TOPK_MAX, seq // 4),
                     edge_block=0, main_shift=1, n_main=seq // KEY_BLOCK)
    n_cache = N_META + past
    n_pos_s = n_cache + dec
    s_pad_s = -(-n_pos_s // KEY_BLOCK) * KEY_BLOCK
    last_s = s_pad_s // KEY_BLOCK - 1
    near_s = max(0, (n_cache - MAX_DISTANCE) // KEY_BLOCK)
    cfg_s = _AttnCfg(tq=LANE, n_rows=dec, n_qt=1, s_pad=s_pad_s, lane_off=0,
                     n_pos=n_pos_s, q_pos0=n_cache, near_step=0, near_base=near_s,
                     n_near=last_s - near_s + 1, top_k=min(TOPK_MAX, (past + dec) // 4),
                     edge_block=last_s, main_shift=0, n_main=last_s, dims_major=True)
    cfg_m = _AttnCfg(tq=LANE, n_rows=N_META, n_qt=1, s_pad=KEY_BLOCK, lane_off=KEY_BLOCK - N_META,
                     n_pos=N_META, q_pos0=0, near_step=0, near_base=0, n_near=1,
                     top_k=min(TOPK_MAX, seq // 4), edge_block=0, main_shift=1, n_main=0)
    assert dec <= LANE and n_cache >= last_s * KEY_BLOCK

    bias_f = _bias_tiles(cfg_f.bucket_table(), rel_bias)
    bias_s = _bias_tiles(cfg_s.bucket_table(), rel_bias)
    bias_m = _bias_tiles(cfg_m.bucket_table(), rel_bias)
    tri = (jnp.arange(KEY_BLOCK)[None, :] <= jnp.arange(KEY_BLOCK)[:, None]).astype(BF16)

    n_aux = bs * dec + N_META
    hf = x_prompt.reshape(bp * seq, d)
    haux = jnp.concatenate([x_sample.reshape(bs * dec, d), meta_tokens.astype(x_sample.dtype)], axis=0)

    w_all = w_in.astype(BF16)
    wo_all = w_out.astype(BF16)
    feat_major = lambda c: jnp.transpose(c, (0, 1, 3, 4, 2)).reshape(depth, bs, D_ATT, n_cache)
    ck_t, cv_t = feat_major(cache_k), feat_major(cache_v)
    cx_t = jnp.swapaxes(cache_kidx, 2, 3).astype(BF16)
    cx_t = jnp.concatenate([cx_t, cx_t], axis=2)
    outs = {name: [] for name in ("kp", "vp", "kip", "cp", "ks", "vs", "kis", "cs")}
    for l in range(depth):
        w_tail = jnp.pad(w_all[l, :, N_GROUPS * GROUP:], ((0, 0), (0, LANE - IDX_DIM - IDX_HEADS)))
        g = norm_g[l][None, :]
        cw = jnp.repeat(conv_w[l], SUBLANE, axis=0)
        cb, lg, lb = conv_b[l][None, :], conv_ln_g[l][None, :], conv_ln_b[l][None, :]
        final = l == depth - 1
        fg = final_g[None, :]

        uA, zcA, qA, kA, vA, zaA, qiA, tailA, kbA, vbA = _inproj(haux, g, w_all, l, w_tail, n_aux)
        uF, zcF, qF, kF, vF, zaF, qiF, tailF, kbF, vbF = _inproj(hf, g, w_all, l, w_tail, tm_frames)
        ns = bs * dec
        split = lambda a: (a[:ns].reshape(bs, dec, -1), a[ns:][None])
        uS, uM = split(uA); zcS, zcM = split(zcA); qS, qM = split(qA); kS, kM = split(kA)
        vS, vM = split(vA); zaS, zaM = split(zaA); qiS, qiM = split(qiA); tailS, tailM = split(tailA)
        kbS, kbM = split(kbA); vbS, vbM = split(vbA)
        b3 = lambda a: a.reshape(bp, seq, -1)
        uF, zcF, qF, kF, vF, zaF, qiF, tailF, kbF, vbF = map(
            b3, (uF, zcF, qF, kF, vF, zaF, qiF, tailF, kbF, vbF))

        zero_hist = jnp.zeros((1, HIST_ROWS, D_CONV), F32)
        cM = _conv(uM, zero_hist, zcM, cw, cb, lg, lb)
        hist_f = jnp.concatenate([jnp.zeros((1, HIST_ROWS - N_META, D_CONV), F32), uM], axis=1)
        cF = _conv(uF, hist_f, zcF, cw, cb, lg, lb)
        hist_s = jnp.pad(state_conv[l].astype(F32), ((0, 0), (HIST_PAD, 0), (0, 0)))
        cS = _conv(uS, hist_s, zcS, cw, cb, lg, lb)

        kidx = lambda tail: tail[..., :IDX_DIM]
        twice = lambda a: jnp.concatenate([a, a], axis=-1).astype(BF16)
        meta_edges = tuple(_edge_block(cfg_f, [a], cfg_f.lane_off) for a in (kbM, vbM, twice(kidx(tailM))))
        aM = _attend(cfg_m, qM, qiM, tailM, zaM, meta_edges, meta_edges, bias_m, tri)
        aF = _attend(cfg_f, qF, qiF, tailF, zaF, (kbF, vbF, twice(kidx(tailF))), meta_edges, bias_f, tri)
        tr = lambda a: jnp.swapaxes(a, 1, 2)
        lo = cfg_s.edge_block * KEY_BLOCK
        sample_edges = tuple(_edge_block(cfg_s, [c[l, :, :, lo:], tr(new)], 0)
                             for c, new in ((ck_t, kS), (cv_t, vS), (cx_t, twice(kidx(tailS)))))
        aS = _attend(cfg_s, qS, qiS, tailS, zaS, (ck_t, cv_t, cx_t), sample_edges, bias_s, tri, main_lead=(l,))

        cAux = jnp.concatenate([cS.reshape(ns, D_CONV), cM[0]], axis=0)
        aAux = jnp.concatenate([aS.reshape(ns, D_ATT), aM[0]], axis=0)
        haux = _outproj(cAux, aAux, haux, wo_all, l, fg, n_aux, final)
        hf = _outproj(cF.reshape(bp * seq, D_CONV), aF.reshape(bp * seq, D_ATT), hf, wo_all, l, fg,
                      tm_frames, final)

        with_meta = lambda m, f: jnp.concatenate([jnp.broadcast_to(m, (bp,) + m.shape[1:]), f], axis=1)
        outs["kp"].append(with_meta(kM, kF).reshape(bp, N_META + seq, N_HEADS, HEAD_DIM))
        outs["vp"].append(with_meta(vM, vF).reshape(bp, N_META + seq, N_HEADS, HEAD_DIM))
        outs["kip"].append(with_meta(kidx(tailM), kidx(tailF)))
        u_ext_p = jnp.concatenate([jnp.zeros((bp, CONV_WIDTH - 1, D_CONV), F32),
                                   jnp.broadcast_to(uM, (bp, N_META, D_CONV)), uF], axis=1)
        outs["cp"].append(u_ext_p[:, -(CONV_WIDTH - 1):])
        outs["ks"].append(kS.reshape(bs, dec, N_HEADS, HEAD_DIM))
        outs["vs"].append(vS.reshape(bs, dec, N_HEADS, HEAD_DIM))
        outs["kis"].append(kidx(tailS))
        u_ext_s = jnp.concatenate([state_conv[l].astype(F32), uS], axis=1)
        outs["cs"].append(u_ext_s[:, -(CONV_WIDTH - 1):])

    y_prompt = hf.reshape(bp, seq, d)
    y_sample = haux[:bs * dec].reshape(bs, dec, d)
    st = lambda name: jnp.stack(outs[name])
    return (y_prompt, y_sample, st("kp"), st("vp"), st("kip"), st("cp"),
            st("ks"), st("vs"), st("kis"), st("cs"))


def kernel(x_prompt, x_sample, cache_k, cache_v, cache_kidx, state_conv, meta_tokens, norm_g, w_in,
           conv_w, conv_b, conv_ln_g, conv_ln_b, w_out, rel_bias, final_g):
    return _forward(x_prompt, x_sample, cache_k, cache_v, cache_kidx, state_conv, meta_tokens,
                    norm_g, w_in, conv_w, conv_b, conv_ln_g, conv_ln_b, w_out, rel_bias, final_g)
```
